```python
import math
import jax, jax.numpy as jnp
from jax import lax
import numpy as np


D_MODEL = 2048
BATCH = 2
SEQ = 4096
DEPTH = 4
DEC_BATCH = 32
DEC_SEQ = 1
PAST_LEN = 16384
PAGE_SIZE = 128

D_MIX = D_MODEL
SSM_HEAD_DIM = 64
D_SSD = D_MIX // 2
SSM_HEADS = D_SSD // SSM_HEAD_DIM
SSM_GROUPS = 2
SSM_HEADS_PER_GROUP = SSM_HEADS // SSM_GROUPS
D_STATE = 128
SSM_CONV = 4
SSD_CHUNK = 128
D_XBC = D_SSD + 2 * SSM_GROUPS * D_STATE
HEAD_DIM = 64
D_ATTN = D_MIX // 4
ATT_HEADS = D_ATTN // HEAD_DIM
KV_HEADS = 2
Q_PER_KV = ATT_HEADS // KV_HEADS
WINDOW = 128
ATT_BLOCK = 128
ROPE_THETA = 10000.0
D_CONV = D_MIX - D_SSD - D_ATTN
SCONV_K = 3
D_FF = ((8 * D_MODEL // 3 + 255) // 256) * 256
FFN_CONV = 3
PROJ_W = D_SSD + D_XBC + SSM_HEADS + D_ATTN + 2 * KV_HEADS * HEAD_DIM + 3 * D_CONV
EPS = 1e-6

kernel_name = "hymba_ssd_swa_shortconv_decoder_step"


def _proj_offsets():
    sizes = [D_SSD, D_XBC, SSM_HEADS, D_ATTN, KV_HEADS * HEAD_DIM, KV_HEADS * HEAD_DIM,
             D_CONV, D_CONV, D_CONV]
    offs, acc = [], 0
    for s in sizes[:-1]:
        acc += s
        offs.append(acc)
    return offs


def rmsnorm(x, w):
    x32 = x.astype(jnp.float32)
    y = x32 * lax.rsqrt(jnp.mean(x32 * x32, axis=-1, keepdims=True) + EPS)
    return (y * w.astype(jnp.float32)).astype(x.dtype)


def causal_dwconv(x, w, prev):
    K = w.shape[0]
    L = x.shape[1]
    xc = jnp.concatenate([prev.astype(x.dtype), x], axis=1)
    y = sum(xc[:, k:k + L] * w[k] for k in range(K))
    return y, xc[:, L:]


def rope(x, pos):
    half = HEAD_DIM // 2
    inv = jnp.power(ROPE_THETA, -jnp.arange(half, dtype=jnp.float32) / half)
    ang = pos[:, None] * inv[None, :]
    cos = jnp.cos(ang)[None, :, None, :]
    sin = jnp.sin(ang)[None, :, None, :]
    x32 = x.astype(jnp.float32)
    x1, x2 = x32[..., :half], x32[..., half:]
    return jnp.concatenate([x1 * cos - x2 * sin, x2 * cos + x1 * sin], axis=-1).astype(x.dtype)


def ssd_scan(x, dt, a, bm, cm, h0, chunk):
    b, L, G, R, P = x.shape
    N = bm.shape[-1]
    nc = L // chunk
    f32 = jnp.float32
    xdt = (x.astype(f32) * dt[..., None]).reshape(b, nc, chunk, G, R, P)
    bc = bm.astype(f32).reshape(b, nc, chunk, G, N)
    cc = cm.astype(f32).reshape(b, nc, chunk, G, N)
    cs = jnp.cumsum((dt * a).reshape(b, nc, chunk, G, R), axis=2)
    causal = jnp.tril(jnp.ones((chunk, chunk), dtype=bool))[None, None, :, :, None, None]
    seg = cs[:, :, :, None] - cs[:, :, None, :]
    decay = jnp.exp(jnp.where(causal, seg, -jnp.inf))
    cb = jnp.einsum('bclgn,bcsgn->bclsg', cc, bc)
    y_diag = jnp.einsum('bclsg,bclsgr,bcsgrp->bclgrp', cb, decay, xdt)
    to_end = jnp.exp(cs[:, :, -1:] - cs)
    chunk_states = jnp.einsum('bclgn,bclgr,bclgrp->bcgrpn', bc, to_end, xdt)
    chunk_decay = jnp.exp(cs[:, :, -1])

    def step(h, inp):
        dec, st = inp
        return dec[..., None, None] * h + st, h

    h_final, h_start = lax.scan(step, h0.astype(f32),
                                (jnp.moveaxis(chunk_decay, 1, 0), jnp.moveaxis(chunk_states, 1, 0)))
    h_start = jnp.moveaxis(h_start, 0, 1)
    y_off = jnp.einsum('bclgn,bcgrpn,bclgr->bclgrp', cc, h_start, jnp.exp(cs))
    return (y_diag + y_off).reshape(b, L, G, R, P), h_final


def sink_attention(q, k, v, valid, sinks):
    s = jnp.einsum('bnqgrd,bnkgd->bngrqk', q, k).astype(jnp.float32) * (HEAD_DIM ** -0.5)
    s = jnp.where(valid[None, :, None, None], s, -jnp.inf)
    sink = sinks.astype(jnp.float32).reshape(KV_HEADS, Q_PER_KV)[None, None, :, :, None, None]
    m = jnp.maximum(jnp.max(s, axis=-1, keepdims=True), sink)
    p = jnp.exp(s - m)
    p = p / (jnp.sum(p, axis=-1, keepdims=True) + jnp.exp(sink - m))
    return jnp.einsum('bngrqk,bnkgd->bnqgrd', p.astype(v.dtype), v)


def hybrid_layer(x, mod, pos0, ssm_h0, ssm_conv_prev, win_k, win_v, sconv_prev, fconv_prev, p):
    b, L, _ = x.shape
    G, R, P, N = SSM_GROUPS, SSM_HEADS_PER_GROUP, SSM_HEAD_DIM, D_STATE
    shift1, scale1, gate1, shift2, scale2, gate2 = jnp.split(mod[:, None, :], 6, axis=-1)

    h = rmsnorm(x, p['norm1_w']) * (1 + scale1) + shift1
    proj = h @ p['w_in']
    z, xbc, dt_raw, q, k, v, g_b, g_c, hc = jnp.split(proj, _proj_offsets(), axis=-1)

    xbc, ssm_conv_new = causal_dwconv(xbc, p['ssm_conv_w'], ssm_conv_prev)
    xbc = jax.nn.silu(xbc + p['ssm_conv_b'])
    xs, bm, cm = jnp.split(xbc, [D_SSD, D_SSD + G * N], axis=-1)
    xs = xs.reshape(b, L, G, R, P)
    bm = bm.reshape(b, L, G, N)
    cm = cm.reshape(b, L, G, N)
    dt = jax.nn.softplus(dt_raw.astype(jnp.float32) + p['dt_bias'].astype(jnp.float32)).reshape(b, L, G, R)
    a = -jnp.exp(p['a_log'].astype(jnp.float32)).reshape(G, R)
    y, ssm_h = ssd_scan(xs, dt, a, bm, cm, ssm_h0.reshape(b, G, R, P, N), math.gcd(L, SSD_CHUNK))
    y = y + p['d_skip'].astype(jnp.float32).reshape(G, R)[:, :, None] * xs.astype(jnp.float32)
    y = y.astype(x.dtype).reshape(b, L, D_SSD) * jax.nn.silu(z)
    y_ssm = rmsnorm(y.reshape(b, L, G, D_SSD // G),
                    p['ssm_norm_w'].reshape(G, D_SSD // G)).reshape(b, L, D_SSD)

    pos = pos0 + jnp.arange(L, dtype=jnp.float32)
    q = rope(rmsnorm(q.reshape(b, L, ATT_HEADS, HEAD_DIM), p['q_norm_w']), pos)
    k = rope(rmsnorm(k.reshape(b, L, KV_HEADS, HEAD_DIM), p['k_norm_w']), pos)
    v = v.reshape(b, L, KV_HEADS, HEAD_DIM)
    if win_k is None:
        nb = L // ATT_BLOCK
        qb = q.reshape(b, nb, ATT_BLOCK, KV_HEADS, Q_PER_KV, HEAD_DIM)
        kb = k.reshape(b, nb, ATT_BLOCK, KV_HEADS, HEAD_DIM)
        vb = v.reshape(b, nb, ATT_BLOCK, KV_HEADS, HEAD_DIM)
        k_prev = jnp.concatenate([jnp.zeros_like(kb[:, :1]), kb[:, :-1]], axis=1)
        v_prev = jnp.concatenate([jnp.zeros_like(vb[:, :1]), vb[:, :-1]], axis=1)
        k_band = jnp.concatenate([k_prev, kb], axis=2)
        v_band = jnp.concatenate([v_prev, vb], axis=2)
        blk = jnp.arange(nb)[:, None, None]
        qpos = blk * ATT_BLOCK + jnp.arange(ATT_BLOCK)[None, :, None]
        kpos = (blk - 1) * ATT_BLOCK + jnp.arange(2 * ATT_BLOCK)[None, None, :]
        diff = qpos - kpos
        valid = (diff >= 0) & (diff <= WINDOW) & (kpos >= 0)
        o = sink_attention(qb, k_band, v_band, valid, p['sinks']).reshape(b, L, D_ATTN)
        new_k, new_v = k[:, L - WINDOW:], v[:, L - WINDOW:]
    else:
        n_buf = win_k.shape[1]
        kc = jnp.concatenate([win_k.astype(k.dtype), k], axis=1)
        vc = jnp.concatenate([win_v.astype(v.dtype), v], axis=1)
        qpos = jnp.arange(L)[:, None] + n_buf
        kpos = jnp.arange(n_buf + L)[None, :]
        diff = qpos - kpos
        valid = ((diff >= 0) & (diff <= WINDOW))[None]
        o = sink_attention(q.reshape(b, 1, L, KV_HEADS, Q_PER_KV, HEAD_DIM), kc[:, None], vc[:, None],
                           valid, p['sinks']).reshape(b, L, D_ATTN)
        new_k, new_v = kc[:, L:], vc[:, L:]

    conv, sconv_new = causal_dwconv(g_c * hc, p['sconv_w'], sconv_prev)
    y_conv = g_b * conv

    mixed = jnp.concatenate([y_ssm, o, y_conv], axis=-1) @ p['w_out']
    x = x + gate1 * mixed

    h2 = rmsnorm(x, p['norm2_w']) * (1 + scale2) + shift2
    gt, u = jnp.split(h2 @ p['w_up'], 2, axis=-1)
    gt, fconv_new = causal_dwconv(gt, p['ffn_conv_w'], fconv_prev)
    y_ff = (jax.nn.silu(gt + p['ffn_conv_b']) * u) @ p['w_down']
    x = x + gate2 * y_ff
    return x, (ssm_h.reshape(b, SSM_HEADS, P, N).astype(ssm_h0.dtype), ssm_conv_new, new_k, new_v,
               sconv_new, fconv_new)


def setup_inputs(seed: int = 0) -> dict:
    key = jax.random.key(seed)
    ks = jax.random.split(key, 40)
    f32 = jnp.float32

    def nrm(k, shape, s=1.0):
        return s * jax.random.normal(k, shape, f32)

    n_buf = min(WINDOW, PAST_LEN)
    dt0 = jnp.exp(jax.random.uniform(ks[20], (DEPTH, SSM_HEADS), f32, math.log(1e-3), math.log(1e-1)))
    return {
        'x_prompt': nrm(ks[0], (BATCH, SEQ, D_MODEL)),
        'x_sample': nrm(ks[1], (DEC_BATCH, DEC_SEQ, D_MODEL)),
        'state_ssm': nrm(ks[2], (DEPTH, DEC_BATCH, SSM_HEADS, SSM_HEAD_DIM, D_STATE), 0.5),
        'state_ssm_conv': nrm(ks[3], (DEPTH, DEC_BATCH, SSM_CONV - 1, D_XBC)),
        'cache_win_k': nrm(ks[4], (DEPTH, DEC_BATCH, n_buf, KV_HEADS, HEAD_DIM)),
        'cache_win_v': nrm(ks[5], (DEPTH, DEC_BATCH, n_buf, KV_HEADS, HEAD_DIM)),
        'state_short_conv': nrm(ks[6], (DEPTH, DEC_BATCH, SCONV_K - 1, D_CONV)),
        'state_ffn_conv': nrm(ks[7], (DEPTH, DEC_BATCH, FFN_CONV - 1, D_FF)),
        'c_prompt': nrm(ks[8], (BATCH, D_MODEL)),
        'c_sample': nrm(ks[9], (DEC_BATCH, D_MODEL)),
        'norm1_w': 1.0 + nrm(ks[10], (DEPTH, D_MODEL), 0.1),
        'norm2_w': 1.0 + nrm(ks[11], (DEPTH, D_MODEL), 0.1),
        'w_ada': nrm(ks[12], (DEPTH, D_MODEL, 6 * D_MODEL), 0.5 * D_MODEL ** -0.5),
        'b_ada': nrm(ks[13], (DEPTH, 6 * D_MODEL), 0.02),
        'w_in': nrm(ks[14], (DEPTH, D_MODEL, PROJ_W), D_MODEL ** -0.5),
        'ssm_conv_w': nrm(ks[15], (DEPTH, SSM_CONV, D_XBC), SSM_CONV ** -0.5),
        'ssm_conv_b': nrm(ks[16], (DEPTH, D_XBC), 0.02),
        'dt_bias': dt0 + jnp.log(-jnp.expm1(-dt0)),
        'a_log': jnp.log(jax.random.uniform(ks[17], (DEPTH, SSM_HEADS), f32, 1.0, 16.0)),
        'd_skip': 1.0 + nrm(ks[18], (DEPTH, SSM_HEADS), 0.1),
        'ssm_norm_w': 1.0 + nrm(ks[19], (DEPTH, D_SSD), 0.1),
        'q_norm_w': 1.0 + nrm(ks[21], (DEPTH, HEAD_DIM), 0.1),
        'k_norm_w': 1.0 + nrm(ks[22], (DEPTH, HEAD_DIM), 0.1),
        'sinks': nrm(ks[23], (DEPTH, ATT_HEADS), 0.5),
        'sconv_w': nrm(ks[24], (DEPTH, SCONV_K, D_CONV), SCONV_K ** -0.5),
        'w_out': nrm(ks[25], (DEPTH, D_MIX, D_MODEL), D_MIX ** -0.5),
        'w_up': nrm(ks[26], (DEPTH, D_MODEL, 2 * D_FF), D_MODEL ** -0.5),
        'ffn_conv_w': nrm(ks[27], (DEPTH, FFN_CONV, D_FF), FFN_CONV ** -0.5),
        'ffn_conv_b': nrm(ks[28], (DEPTH, D_FF), 0.02),
        'w_down': nrm(ks[29], (DEPTH, D_FF, D_MODEL), D_FF ** -0.5),
    }


def reference(x_prompt, x_sample, state_ssm, state_ssm_conv, cache_win_k, cache_win_v,
              state_short_conv, state_ffn_conv, c_prompt, c_sample,
              norm1_w, norm2_w, w_ada, b_ada, w_in, ssm_conv_w, ssm_conv_b, dt_bias, a_log, d_skip,
              ssm_norm_w, q_norm_w, k_norm_w, sinks, sconv_w, w_out, w_up, ffn_conv_w, ffn_conv_b,
              w_down):
    dt_ = x_prompt.dtype
    xp, xs = x_prompt, x_sample
    p_out = [[] for _ in range(6)]
    s_out = [[] for _ in range(6)]
    for l in range(DEPTH):
        p = {'norm1_w': norm1_w[l], 'norm2_w': norm2_w[l], 'w_in': w_in[l],
             'ssm_conv_w': ssm_conv_w[l], 'ssm_conv_b': ssm_conv_b[l], 'dt_bias': dt_bias[l],
             'a_log': a_log[l], 'd_skip': d_skip[l], 'ssm_norm_w': ssm_norm_w[l],
             'q_norm_w': q_norm_w[l], 'k_norm_w': k_norm_w[l], 'sinks': sinks[l],
             'sconv_w': sconv_w[l], 'w_out': w_out[l], 'w_up': w_up[l],
             'ffn_conv_w': ffn_conv_w[l], 'ffn_conv_b': ffn_conv_b[l], 'w_down': w_down[l]}
        mod_p = jax.nn.silu(c_prompt) @ w_ada[l] + b_ada[l]
        mod_s = jax.nn.silu(c_sample) @ w_ada[l] + b_ada[l]
        xp, st_p = hybrid_layer(
            xp, mod_p, 0.0,
            jnp.zeros((BATCH, SSM_HEADS, SSM_HEAD_DIM, D_STATE), dt_),
            jnp.zeros((BATCH, SSM_CONV - 1, D_XBC), dt_),
            None, None,
            jnp.zeros((BATCH, SCONV_K - 1, D_CONV), dt_),
            jnp.zeros((BATCH, FFN_CONV - 1, D_FF), dt_), p)
        xs, st_s = hybrid_layer(
            xs, mod_s, float(PAST_LEN), state_ssm[l], state_ssm_conv[l], cache_win_k[l], cache_win_v[l],
            state_short_conv[l], state_ffn_conv[l], p)
        for i in range(6):
            p_out[i].append(st_p[i])
            s_out[i].append(st_s[i])
    p_ssm, p_ssm_conv, p_win_k, p_win_v, p_short_conv, p_ffn_conv = [jnp.stack(a, axis=0) for a in p_out]
    s_ssm, s_ssm_conv, s_win_k, s_win_v, s_short_conv, s_ffn_conv = [jnp.stack(a, axis=0) for a in s_out]
    return (xp, xs, p_ssm, p_ssm_conv, p_win_k, p_win_v, p_short_conv, p_ffn_conv,
            s_ssm, s_ssm_conv, s_win_k, s_win_v, s_short_conv, s_ffn_conv)
```

```python
import functools
import math

import jax
import jax.numpy as jnp
from jax import lax
from jax.experimental import pallas as pl
from jax.experimental.pallas import tpu as pltpu

F32 = jnp.float32
BF16 = jnp.bfloat16

D_MODEL = 2048
D_SSD = 1024
SSM_HEADS = 16
SSM_HEAD_DIM = 64
D_STATE = 128
SSM_CONV = 4
CHUNK = 128
D_XBC = 1536
HEAD_DIM = 64
D_ATTN = 512
ATT_HEADS = 8
KV_HEADS = 2
WINDOW = 128
ROPE_THETA = 10000.0
D_CONV = 512
D_FF = 5632
PROJ_W = 4880
EPS = 1e-6
PAST_LEN = 16384

C_Z, C_XBC, C_Q, C_G, C_K, C_V, C_DT, PROJ_P = 0, 1024, 2560, 3072, 4608, 4736, 4864, 5120

VMEM_LIMIT = 56 * 1024 * 1024


def _cparams(sem):
    return pltpu.CompilerParams(dimension_semantics=sem, vmem_limit_bytes=VMEM_LIMIT)


def _silu(x):
    return x * jax.nn.sigmoid(x)


def _rms_mod(x, w, scale, shift):
    ms = jnp.mean(x * x, axis=-1, keepdims=True)
    return (x * lax.rsqrt(ms + EPS) * w) * (1.0 + scale) + shift


def _split3(x):
    h1 = x.astype(BF16)
    r1 = x - h1.astype(F32)
    h2 = r1.astype(BF16)
    r2 = r1 - h2.astype(F32)
    return h1, h2, r2.astype(BF16)


def _dot(a, b):
    return jnp.dot(a, b, preferred_element_type=F32)


def _dot_nt(a, b):
    return lax.dot_general(a, b, (((1,), (1,)), ((), ())), preferred_element_type=F32)


def _mod_kernel(c_ref, w_ref, b_ref, o_ref):
    a = _silu(c_ref[...]).astype(BF16)
    o_ref[...] = _dot(a, w_ref[...].astype(BF16)) + b_ref[...]


def _modulation(c_all, w_ada, b_ada, tn=1024):
    depth, d, n = w_ada.shape
    r = c_all.shape[0]
    return pl.pallas_call(
        _mod_kernel,
        grid=(depth, n // tn),
        in_specs=[
            pl.BlockSpec((r, d), lambda l, j: (0, 0)),
            pl.BlockSpec((None, d, tn), lambda l, j: (l, 0, j)),
            pl.BlockSpec((None, 1, tn), lambda l, j: (l, 0, j)),
        ],
        out_specs=pl.BlockSpec((None, r, tn), lambda l, j: (l, 0, j)),
        out_shape=jax.ShapeDtypeStruct((depth, r, n), F32),
        compiler_params=_cparams(("arbitrary", "arbitrary")),
        name="modulation",
    )(c_all, w_ada, b_ada.reshape(depth, 1, n))


def _norm_kernel(x_ref, w_ref, sc_ref, sh_ref, o_ref):
    o_ref[...] = _rms_mod(x_ref[...], w_ref[...], sc_ref[...], sh_ref[...]).astype(BF16)


def _mod_spec(mod, l, j, tm, rpb, grid_rank, m_axis):
    r = mod.shape[2]

    def imap(*idx):
        return (l, (idx[m_axis] * tm) // rpb, 0, j)

    del grid_rank
    return pl.BlockSpec((None, None, r, D_MODEL), imap)


def _norm_mod(x, w, mod, l, j_scale, j_shift, tm, rpb):
    m = x.shape[0]
    return pl.pallas_call(
        _norm_kernel,
        grid=(m // tm,),
        in_specs=[
            pl.BlockSpec((tm, D_MODEL), lambda i: (i, 0)),
            pl.BlockSpec((1, D_MODEL), lambda i: (0, 0)),
            _mod_spec(mod, l, j_scale, tm, rpb, 1, 0),
            _mod_spec(mod, l, j_shift, tm, rpb, 1, 0),
        ],
        out_specs=pl.BlockSpec((tm, D_MODEL), lambda i: (i, 0)),
        out_shape=jax.ShapeDtypeStruct((m, D_MODEL), BF16),
        compiler_params=_cparams(("arbitrary",)),
        name="norm_mod",
    )(x, w.reshape(1, D_MODEL), mod, mod)


def _mm_kernel(a_ref, w_ref, o_ref):
    o_ref[...] = _dot(a_ref[...], w_ref[...])


def _matmul(a, w, tm, tn):
    m, k = a.shape
    n = w.shape[1]
    return pl.pallas_call(
        _mm_kernel,
        grid=(n // tn, m // tm),
        in_specs=[
            pl.BlockSpec((tm, k), lambda j, i: (i, 0)),
            pl.BlockSpec((k, tn), lambda j, i: (0, j)),
        ],
        out_specs=pl.BlockSpec((tm, tn), lambda j, i: (i, j)),
        out_shape=jax.ShapeDtypeStruct((m, n), F32),
        compiler_params=_cparams(("arbitrary", "arbitrary")),
        name="in_proj",
    )(a, w)


def _ssd_kernel(zx_ref, g_ref, kvdt_ref, cw_ref, cb_ref, dtb_ref, alog_ref, dsk_ref, nw_ref, scw_ref,
                y_ref, yc_ref, st_out_ref, cst_ref, ust_ref,
                xpad_ref, upad_ref, st_ref):
    c = pl.program_id(1)
    nc = pl.num_programs(1)
    L = CHUNK

    @pl.when(c == 0)
    def _():
        xpad_ref[0:8, :] = jnp.zeros((8, D_XBC), F32)
        upad_ref[0:8, :] = jnp.zeros((8, D_CONV), F32)
        st_ref[...] = jnp.zeros_like(st_ref)

    zx = zx_ref[...]
    z = zx[:, :D_SSD]
    raw = zx[:, D_SSD:]

    xpad_ref[8:8 + L, :] = raw
    cw = cw_ref[...]
    conv = (cw[3:4] * raw + cw[2:3] * xpad_ref[7:7 + L, :] + cw[1:2] * xpad_ref[6:6 + L, :]
            + cw[0:1] * xpad_ref[5:5 + L, :])
    xpad_ref[0:8, :] = raw[L - 8:, :]
    cst_ref[...] = raw[L - 8:, :]
    xbc = _silu(conv + cb_ref[...])
    xs = xbc[:, :D_SSD]

    lane = lax.broadcasted_iota(jnp.int32, (L, 128), 1)
    row = lax.broadcasted_iota(jnp.int32, (L, 128), 0)
    lo = (lane & 64) == 0
    causal = row >= lane

    head_ok = lane[0:1] < SSM_HEADS
    a_row = jnp.where(head_ok, -jnp.exp(alog_ref[...]), 0.0)
    dtr = kvdt_ref[:, 256:384] + dtb_ref[...]
    dt = jnp.maximum(dtr, 0.0) + jnp.log1p(jnp.exp(-jnp.abs(dtr)))
    da = dt * a_row

    tril = jnp.where(causal, 1.0, 0.0).astype(BF16)
    d1, d2, d3 = _split3(da)
    cs = _dot(tril, d1) + _dot(tril, d2) + _dot(tril, d3)
    cs_t = cs.T
    ecs = jnp.exp(cs)
    te = jnp.exp(cs[L - 1:L, :] - cs)
    ecl = ecs[L - 1:L, :]

    def pair(arr, j):
        rows = arr.shape[0]
        return jnp.where(lo[:rows], arr[:, 2 * j:2 * j + 1], arr[:, 2 * j + 1:2 * j + 2])

    cbs, yoffs, xdt_end = [], [], []
    xdts = []
    for j in range(8):
        xdts.append(xs[:, 128 * j:128 * (j + 1)] * pair(dt, j))
    for g in range(2):
        bm = xbc[:, D_SSD + 128 * g:D_SSD + 128 * (g + 1)]
        cm = xbc[:, D_SSD + 256 + 128 * g:D_SSD + 256 + 128 * (g + 1)]
        cm16 = cm.astype(BF16)
        cbs.append(_dot_nt(cm16, bm.astype(BF16)))
        yoffs.append(_dot(cm16, st_ref[:, 512 * g:512 * (g + 1)].astype(BF16)))
        xe = jnp.concatenate([xdts[4 * g + i] * pair(te, 4 * g + i) for i in range(4)], axis=1)
        s_new = _dot(bm.T.astype(BF16), xe.astype(BF16))
        cd = jnp.concatenate([pair(ecl, 4 * g + i) for i in range(4)], axis=1)
        xdt_end.append((cd, s_new))
    for g in range(2):
        cd, s_new = xdt_end[g]
        st_ref[:, 512 * g:512 * (g + 1)] = cd * st_ref[:, 512 * g:512 * (g + 1)] + s_new

    ys = []
    for j in range(8):
        g = j // 4
        ca, cb_ = cs[:, 2 * j:2 * j + 1], cs[:, 2 * j + 1:2 * j + 2]
        ra, rb = cs_t[2 * j:2 * j + 1, :], cs_t[2 * j + 1:2 * j + 2, :]
        dec_a = jnp.where(causal, jnp.exp(jnp.minimum(ca - ra, 0.0)), 0.0)
        dec_b = jnp.where(causal, jnp.exp(jnp.minimum(cb_ - rb, 0.0)), 0.0)
        mm = jnp.concatenate([cbs[g] * dec_a, cbs[g] * dec_b], axis=1).astype(BF16)
        xp = xdts[j]
        rhs = jnp.concatenate([jnp.where(lo, xp, 0.0), jnp.where(lo, 0.0, xp)], axis=0).astype(BF16)
        yd = _dot(mm, rhs)
        yo = yoffs[g][:, 128 * (j % 4):128 * (j % 4 + 1)] * pair(ecs, j)
        ys.append(yd + yo + dsk_ref[:, 128 * j:128 * (j + 1)] * xs[:, 128 * j:128 * (j + 1)])
    y = jnp.concatenate(ys, axis=1) * _silu(z)
    nw = nw_ref[...]
    outs = []
    for g in range(2):
        yg = y[:, 512 * g:512 * (g + 1)]
        ms = jnp.mean(yg * yg, axis=-1, keepdims=True)
        outs.append(yg * lax.rsqrt(ms + EPS) * nw[:, 512 * g:512 * (g + 1)])
    y_ref[...] = jnp.concatenate(outs, axis=1).astype(BF16)

    @pl.when(c == nc - 1)
    def _():
        st_out_ref[...] = st_ref[...].T

    gg = g_ref[...]
    u = gg[:, D_CONV:2 * D_CONV] * gg[:, 2 * D_CONV:]
    upad_ref[8:8 + L, :] = u
    sw = scw_ref[...]
    sconv = sw[2:3] * u + sw[1:2] * upad_ref[7:7 + L, :] + sw[0:1] * upad_ref[6:6 + L, :]
    upad_ref[0:8, :] = u[L - 8:, :]
    ust_ref[...] = u[L - 8:, :]
    yc_ref[...] = (gg[:, :D_CONV] * sconv).astype(BF16)


def _ssd_prompt(proj, nb, seq, cw, cb, dtb, alog, dsk, nw, scw):
    nc = seq // CHUNK
    m = nb * seq
    row = lambda b, c: (b * nc + c)
    return pl.pallas_call(
        _ssd_kernel,
        grid=(nb, nc),
        in_specs=[
            pl.BlockSpec((CHUNK, 2560), lambda b, c: (row(b, c), 0)),
            pl.BlockSpec((CHUNK, 1536), lambda b, c: (row(b, c), 2)),
            pl.BlockSpec((CHUNK, 512), lambda b, c: (row(b, c), 9)),
            pl.BlockSpec((SSM_CONV, D_XBC), lambda b, c: (0, 0)),
            pl.BlockSpec((1, D_XBC), lambda b, c: (0, 0)),
            pl.BlockSpec((1, 128), lambda b, c: (0, 0)),
            pl.BlockSpec((1, 128), lambda b, c: (0, 0)),
            pl.BlockSpec((1, D_SSD), lambda b, c: (0, 0)),
            pl.BlockSpec((1, D_SSD), lambda b, c: (0, 0)),
            pl.BlockSpec((3, D_CONV), lambda b, c: (0, 0)),
        ],
        out_specs=[
            pl.BlockSpec((CHUNK, D_SSD), lambda b, c: (row(b, c), 0)),
            pl.BlockSpec((CHUNK, D_CONV), lambda b, c: (row(b, c), 0)),
            pl.BlockSpec((None, D_SSD, D_STATE), lambda b, c: (b, 0, 0)),
            pl.BlockSpec((None, 8, D_XBC), lambda b, c: (b, 0, 0)),
            pl.BlockSpec((None, 8, D_CONV), lambda b, c: (b, 0, 0)),
        ],
        out_shape=[
            jax.ShapeDtypeStruct((m, D_SSD), BF16),
            jax.ShapeDtypeStruct((m, D_CONV), BF16),
            jax.ShapeDtypeStruct((nb, D_SSD, D_STATE), F32),
            jax.ShapeDtypeStruct((nb, 8, D_XBC), F32),
            jax.ShapeDtypeStruct((nb, 8, D_CONV), F32),
        ],
        scratch_shapes=[
            pltpu.VMEM((8 + CHUNK, D_XBC), F32),
            pltpu.VMEM((8 + CHUNK, D_CONV), F32),
            pltpu.VMEM((D_STATE, D_SSD), F32),
        ],
        compiler_params=_cparams(("arbitrary", "arbitrary")),
        name="ssd_prompt",
    )(proj, proj, proj, cw, cb, dtb, alog, dsk, nw, scw)


def _head_norm(x, w, bd):
    sq = x * x
    hi = sq.astype(BF16)
    lo = (sq - hi.astype(F32)).astype(BF16)
    ms = _dot(hi, bd) + _dot(lo, bd)
    return x * lax.rsqrt(ms + EPS) * w


def _rope(x, cos, sin_signed, first_half):
    partner = jnp.where(first_half, pltpu.roll(x, 96, axis=1), pltpu.roll(x, 32, axis=1))
    return x * cos + partner * sin_signed


def _seg_mean_matrix():
    r = lax.broadcasted_iota(jnp.int32, (128, 128), 0)
    c = lax.broadcasted_iota(jnp.int32, (128, 128), 1)
    return jnp.where((r & 64) == (c & 64), 1.0 / HEAD_DIM, 0.0).astype(BF16)


def _attn_kernel(sink_ref, q_ref, kv_ref, cos_ref, sin_ref, qw_ref, kw_ref,
                 o_ref, kr_ref, kprev_ref, vprev_ref):
    c = pl.program_id(1)
    L = CHUNK

    @pl.when(c == 0)
    def _():
        kprev_ref[...] = jnp.zeros_like(kprev_ref)
        vprev_ref[...] = jnp.zeros_like(vprev_ref)

    bd = _seg_mean_matrix()
    lane = lax.broadcasted_iota(jnp.int32, (L, 128), 1)
    lo = (lane & 64) == 0
    first_half = (lane & 32) == 0
    cos = cos_ref[...]
    sin = sin_ref[...]

    k_new = _rope(_head_norm(kv_ref[:, 0:128], kw_ref[...], bd), cos, sin, first_half)
    v_new = kv_ref[:, 128:256]
    kr_ref[...] = k_new
    kband = jnp.concatenate([kprev_ref[...], k_new], axis=0)
    vband = jnp.concatenate([vprev_ref[...], v_new], axis=0)
    kprev_ref[...] = k_new
    vprev_ref[...] = v_new
    lo2 = jnp.concatenate([lo, lo], axis=0)
    kswap = pltpu.roll(kband, 64, axis=1)
    vswap = pltpu.roll(vband, 64, axis=1)
    k2 = [jnp.where(lo2, kband, kswap).astype(BF16), jnp.where(lo2, kswap, kband).astype(BF16)]
    v2 = [jnp.where(lo2, vband, vswap).astype(BF16), jnp.where(lo2, vswap, vband).astype(BF16)]

    qi = lax.broadcasted_iota(jnp.int32, (L, 2 * L), 0)
    kj = lax.broadcasted_iota(jnp.int32, (L, 2 * L), 1)
    diff = L + qi - kj
    valid = (diff >= 0) & (diff <= WINDOW) & ((kj >= L) | (c > 0))

    qt = []
    for t in range(4):
        qn = _head_norm(q_ref[:, 128 * t:128 * (t + 1)], qw_ref[...], bd)
        qt.append(_rope(qn, cos, sin, first_half) * (HEAD_DIM ** -0.5))

    otiles = []
    for g in range(KV_HEADS):
        lhs = []
        for t in (2 * g, 2 * g + 1):
            lhs.append(jnp.where(lo, qt[t], 0.0))
            lhs.append(jnp.where(lo, 0.0, qt[t]))
        s = _dot_nt(jnp.concatenate(lhs, axis=0).astype(BF16), k2[g])
        ps = []
        for i in range(4):
            sink = sink_ref[4 * g + i]
            sh = jnp.where(valid, s[L * i:L * (i + 1)], -1e30)
            mx = jnp.maximum(jnp.max(sh, axis=-1, keepdims=True), sink)
            p = jnp.exp(sh - mx)
            den = jnp.sum(p, axis=-1, keepdims=True) + jnp.exp(sink - mx)
            ps.append(p / den)
        og = _dot(jnp.concatenate(ps, axis=0).astype(BF16), v2[g])
        otiles.append(jnp.where(lo, og[0:L], og[L:2 * L]))
        otiles.append(jnp.where(lo, og[2 * L:3 * L], og[3 * L:4 * L]))
    o_ref[...] = jnp.concatenate(otiles, axis=1).astype(BF16)


def _attn_prompt(proj, nb, seq, sinks, cos_t, sin_t, qw, kw):
    nc = seq // CHUNK
    m = nb * seq
    row = lambda b, c: (b * nc + c)
    return pl.pallas_call(
        _attn_kernel,
        grid=(nb, nc),
        in_specs=[
            pl.BlockSpec(memory_space=pltpu.SMEM),
            pl.BlockSpec((CHUNK, 512), lambda b, c: (row(b, c), 5)),
            pl.BlockSpec((CHUNK, 512), lambda b, c: (row(b, c), 9)),
            pl.BlockSpec((CHUNK, 128), lambda b, c: (c, 0)),
            pl.BlockSpec((CHUNK, 128), lambda b, c: (c, 0)),
            pl.BlockSpec((1, 128), lambda b, c: (0, 0)),
            pl.BlockSpec((1, 128), lambda b, c: (0, 0)),
        ],
        out_specs=[
            pl.BlockSpec((CHUNK, D_ATTN), lambda b, c: (row(b, c), 0)),
            pl.BlockSpec((CHUNK, 128), lambda b, c: (row(b, c), 0)),
        ],
        out_shape=[
            jax.ShapeDtypeStruct((m, D_ATTN), BF16),
            jax.ShapeDtypeStruct((m, 128), F32),
        ],
        scratch_shapes=[pltpu.VMEM((CHUNK, 128), F32), pltpu.VMEM((CHUNK, 128), F32)],
        compiler_params=_cparams(("arbitrary", "arbitrary")),
        name="attn_prompt",
    )(sinks, proj, proj, cos_t, sin_t, qw, kw)


def _outproj_kernel(ys_ref, o_ref, yc_ref, w_ref, x_ref, gate_ref, sc_ref, sh_ref, nw_ref, x1_ref, h2_ref):
    acc = _dot(ys_ref[...], w_ref[0:D_SSD, :])
    acc += _dot(o_ref[...], w_ref[D_SSD:D_SSD + D_ATTN, :])
    acc += _dot(yc_ref[...], w_ref[D_SSD + D_ATTN:, :])
    x1 = x_ref[...] + gate_ref[...] * acc
    x1_ref[...] = x1
    h2_ref[...] = _rms_mod(x1, nw_ref[...], sc_ref[...], sh_ref[...]).astype(BF16)


def _outproj(ys, o, yc, w, x, mod, l, nw2, tm, rpb):
    m = x.shape[0]
    return pl.pallas_call(
        _outproj_kernel,
        grid=(m // tm,),
        in_specs=[
            pl.BlockSpec((tm, D_SSD), lambda i: (i, 0)),
            pl.BlockSpec((tm, D_ATTN), lambda i: (i, 0)),
            pl.BlockSpec((tm, D_CONV), lambda i: (i, 0)),
            pl.BlockSpec((D_MODEL, D_MODEL), lambda i: (0, 0)),
            pl.BlockSpec((tm, D_MODEL), lambda i: (i, 0)),
            _mod_spec(mod, l, 2, tm, rpb, 1, 0),
            _mod_spec(mod, l, 4, tm, rpb, 1, 0),
            _mod_spec(mod, l, 3, tm, rpb, 1, 0),
            pl.BlockSpec((1, D_MODEL), lambda i: (0, 0)),
        ],
        out_specs=[
            pl.BlockSpec((tm, D_MODEL), lambda i: (i, 0)),
            pl.BlockSpec((tm, D_MODEL), lambda i: (i, 0)),
        ],
        out_shape=[
            jax.ShapeDtypeStruct((m, D_MODEL), F32),
            jax.ShapeDtypeStruct((m, D_MODEL), BF16),
        ],
        compiler_params=_cparams(("arbitrary",)),
        name="out_proj",
    )(ys, o, yc, w, x, mod, mod, mod, nw2.reshape(1, D_MODEL))


def _up_seq_kernel(h_ref, wg_ref, wu_ref, cw_ref, cb_ref, a_ref, gst_ref, pad_ref, *, tm, rpb):
    i = pl.program_id(1)

    @pl.when((i * tm) % rpb == 0)
    def _():
        pad_ref[0:8, :] = jnp.zeros((8, pad_ref.shape[1]), F32)

    h = h_ref[...]
    gt = _dot(h, wg_ref[...])
    u = _dot(h, wu_ref[...])
    pad_ref[8:8 + tm, :] = gt
    cw = cw_ref[...]
    conv = cw[2:3] * gt + cw[1:2] * pad_ref[7:7 + tm, :] + cw[0:1] * pad_ref[6:6 + tm, :]
    pad_ref[0:8, :] = gt[tm - 8:, :]
    gst_ref[...] = gt[tm - 8:, :]
    a_ref[...] = (_silu(conv + cb_ref[...]) * u).astype(BF16)


def _up_seq(h, w_up, cw, cb, nb, tm, tn):
    m = h.shape[0]
    rpb = m // nb
    nt = D_FF // tn
    return pl.pallas_call(
        functools.partial(_up_seq_kernel, tm=tm, rpb=rpb),
        grid=(nt, m // tm),
        in_specs=[
            pl.BlockSpec((tm, D_MODEL), lambda j, i: (i, 0)),
            pl.BlockSpec((D_MODEL, tn), lambda j, i: (0, j)),
            pl.BlockSpec((D_MODEL, tn), lambda j, i: (0, j + nt)),
            pl.BlockSpec((3, tn), lambda j, i: (0, j)),
            pl.BlockSpec((1, tn), lambda j, i: (0, j)),
        ],
        out_specs=[
            pl.BlockSpec((tm, tn), lambda j, i: (i, j)),
            pl.BlockSpec((None, 8, tn), lambda j, i: ((i * tm) // rpb, 0, j)),
        ],
        out_shape=[
            jax.ShapeDtypeStruct((m, D_FF), BF16),
            jax.ShapeDtypeStruct((nb, 8, D_FF), F32),
        ],
        scratch_shapes=[pltpu.VMEM((8 + tm, tn), F32)],
        compiler_params=_cparams(("arbitrary", "arbitrary")),
        name="up_proj_seq",
    )(h, w_up, w_up, cw, cb)


def _up_step_kernel(h_ref, wg_ref, wu_ref, cw_ref, cb_ref, prev_ref, a_ref, gt_ref):
    h = h_ref[...]
    gt = _dot(h, wg_ref[...])
    u = _dot(h, wu_ref[...])
    cw = cw_ref[...]
    conv = cw[2:3] * gt + cw[1:2] * prev_ref[1] + cw[0:1] * prev_ref[0]
    gt_ref[...] = gt
    a_ref[...] = (_silu(conv + cb_ref[...]) * u).astype(BF16)


def _up_step(h, w_up, cw, cb, prev, tn):
    m = h.shape[0]
    nt = D_FF // tn
    return pl.pallas_call(
        _up_step_kernel,
        grid=(nt,),
        in_specs=[
            pl.BlockSpec((m, D_MODEL), lambda j: (0, 0)),
            pl.BlockSpec((D_MODEL, tn), lambda j: (0, j)),
            pl.BlockSpec((D_MODEL, tn), lambda j: (0, j + nt)),
            pl.BlockSpec((3, tn), lambda j: (0, j)),
            pl.BlockSpec((1, tn), lambda j: (0, j)),
            pl.BlockSpec((2, m, tn), lambda j: (0, 0, j)),
        ],
        out_specs=[
            pl.BlockSpec((m, tn), lambda j: (0, j)),
            pl.BlockSpec((m, tn), lambda j: (0, j)),
        ],
        out_shape=[
            jax.ShapeDtypeStruct((m, D_FF), BF16),
            jax.ShapeDtypeStruct((m, D_FF), F32),
        ],
        compiler_params=_cparams(("arbitrary",)),
        name="up_proj_step",
    )(h, w_up, w_up, cw, cb, prev)


def _down_kernel(a_ref, w_ref, x_ref, gate_ref, *rest, with_norm):
    if with_norm:
        sc_ref, sh_ref, nw_ref, x2_ref, hn_ref, acc_ref = rest
    else:
        x2_ref, acc_ref = rest
    k = pl.program_id(1)

    @pl.when(k == 0)
    def _():
        acc_ref[...] = jnp.zeros_like(acc_ref)

    acc_ref[...] += _dot(a_ref[...], w_ref[...])

    @pl.when(k == pl.num_programs(1) - 1)
    def _():
        x2 = x_ref[...] + gate_ref[...] * acc_ref[...]
        x2_ref[...] = x2
        if with_norm:
            hn_ref[...] = _rms_mod(x2, nw_ref[...], sc_ref[...], sh_ref[...]).astype(BF16)


def _down(a, w, x, mod, l, nw_next, tm, tk, rpb):
    m = x.shape[0]
    with_norm = nw_next is not None
    in_specs = [
        pl.BlockSpec((tm, tk), lambda i, k: (i, k)),
        pl.BlockSpec((tk, D_MODEL), lambda i, k: (k, 0)),
        pl.BlockSpec((tm, D_MODEL), lambda i, k: (i, 0)),
        _mod_spec(mod, l, 5, tm, rpb, 2, 0),
    ]
    args = [a, w, x, mod]
    out_specs = [pl.BlockSpec((tm, D_MODEL), lambda i, k: (i, 0))]
    out_shape = [jax.ShapeDtypeStruct((m, D_MODEL), F32)]
    if with_norm:
        in_specs += [
            _mod_spec(mod, l + 1, 1, tm, rpb, 2, 0),
            _mod_spec(mod, l + 1, 0, tm, rpb, 2, 0),
            pl.BlockSpec((1, D_MODEL), lambda i, k: (0, 0)),
        ]
        args += [mod, mod, nw_next.reshape(1, D_MODEL)]
        out_specs.append(pl.BlockSpec((tm, D_MODEL), lambda i, k: (i, 0)))
        out_shape.append(jax.ShapeDtypeStruct((m, D_MODEL), BF16))
    res = pl.pallas_call(
        functools.partial(_down_kernel, with_norm=with_norm),
        grid=(m // tm, D_FF // tk),
        in_specs=in_specs,
        out_specs=out_specs,
        out_shape=out_shape,
        scratch_shapes=[pltpu.VMEM((tm, D_MODEL), F32)],
        compiler_params=_cparams(("arbitrary", "arbitrary")),
        name="down_proj",
    )(*args)
    return (res[0], res[1]) if with_norm else (res[0], None)


def _expand_matrix():
    r = lax.broadcasted_iota(jnp.int32, (128, D_SSD), 0)
    c = lax.broadcasted_iota(jnp.int32, (128, D_SSD), 1)
    return jnp.where(r == (c >> 6), 1.0, 0.0).astype(BF16)


def _ssd_step_kernel(p_ref, st_ref, cprev_ref, sprev_ref, cw_ref, cb_ref, dtb_ref, alog_ref, dsk_ref, nw_ref,
                     scw_ref, y_ref, yc_ref, u_ref, stn_ref):
    nbb = p_ref.shape[0]
    z = p_ref[:, C_Z:C_Z + D_SSD]
    raw = p_ref[:, C_XBC:C_XBC + D_XBC]
    cw = cw_ref[...]
    conv = cw[3:4] * raw + cw[2:3] * cprev_ref[2] + cw[1:2] * cprev_ref[1] + cw[0:1] * cprev_ref[0]
    xbc = _silu(conv + cb_ref[...])
    xs = xbc[:, :D_SSD]
    bm = xbc[:, D_SSD:D_SSD + 256]
    cm = xbc[:, D_SSD + 256:]

    lane = lax.broadcasted_iota(jnp.int32, (1, 128), 1)
    a_row = jnp.where(lane < SSM_HEADS, -jnp.exp(alog_ref[...]), 0.0)
    dtr = p_ref[:, C_DT:C_DT + 128] + dtb_ref[...]
    dt = jnp.maximum(dtr, 0.0) + jnp.log1p(jnp.exp(-jnp.abs(dtr)))
    dec = jnp.exp(dt * a_row)
    em = _expand_matrix()

    def expand(v):
        v1, v2, v3 = _split3(v)
        return _dot(v1, em) + _dot(v2, em) + _dot(v3, em)

    xdt_t = (xs * expand(dt)).T
    dec_t = expand(dec).T
    rowid = lax.broadcasted_iota(jnp.int32, (nbb, 512), 0)
    ysum = [jnp.zeros((nbb, 512), F32), jnp.zeros((nbb, 512), F32)]
    for i in range(nbb):
        for g in range(2):
            rs = slice(512 * g, 512 * (g + 1))
            hn = st_ref[i, rs, :] * dec_t[rs, i:i + 1] + xdt_t[rs, i:i + 1] * bm[i:i + 1, 128 * g:128 * (g + 1)]
            stn_ref[i, rs, :] = hn
            r = _dot_nt(cm[:, 128 * g:128 * (g + 1)].astype(BF16), hn.astype(BF16))
            ysum[g] = ysum[g] + jnp.where(rowid == i, r, 0.0)
    y = jnp.concatenate(ysum, axis=1) + dsk_ref[...] * xs
    y = y * _silu(z)
    nw = nw_ref[...]
    outs = []
    for g in range(2):
        yg = y[:, 512 * g:512 * (g + 1)]
        ms = jnp.mean(yg * yg, axis=-1, keepdims=True)
        outs.append(yg * lax.rsqrt(ms + EPS) * nw[:, 512 * g:512 * (g + 1)])
    y_ref[...] = jnp.concatenate(outs, axis=1).astype(BF16)

    u = p_ref[:, C_G + D_CONV:C_G + 2 * D_CONV] * p_ref[:, C_G + 2 * D_CONV:C_G + 3 * D_CONV]
    sw = scw_ref[...]
    sconv = sw[2:3] * u + sw[1:2] * sprev_ref[1] + sw[0:1] * sprev_ref[0]
    u_ref[...] = u
    yc_ref[...] = (p_ref[:, C_G:C_G + D_CONV] * sconv).astype(BF16)


def _ssd_step(proj, state, cprev, sprev, cw, cb, dtb, alog, dsk, nw, scw, nbb=8):
    nb = proj.shape[0]
    full = lambda shape: pl.BlockSpec(shape, lambda i: tuple(0 for _ in shape))
    return pl.pallas_call(
        _ssd_step_kernel,
        grid=(nb // nbb,),
        in_specs=[
            pl.BlockSpec((nbb, PROJ_P), lambda i: (i, 0)),
            pl.BlockSpec((nbb, D_SSD, D_STATE), lambda i: (i, 0, 0)),
            pl.BlockSpec((3, nbb, D_XBC), lambda i: (0, i, 0)),
            pl.BlockSpec((2, nbb, D_CONV), lambda i: (0, i, 0)),
            full((SSM_CONV, D_XBC)), full((1, D_XBC)), full((1, 128)), full((1, 128)),
            full((1, D_SSD)), full((1, D_SSD)), full((3, D_CONV)),
        ],
        out_specs=[
            pl.BlockSpec((nbb, D_SSD), lambda i: (i, 0)),
            pl.BlockSpec((nbb, D_CONV), lambda i: (i, 0)),
            pl.BlockSpec((nbb, D_CONV), lambda i: (i, 0)),
            pl.BlockSpec((nbb, D_SSD, D_STATE), lambda i: (i, 0, 0)),
        ],
        out_shape=[
            jax.ShapeDtypeStruct((nb, D_SSD), BF16),
            jax.ShapeDtypeStruct((nb, D_CONV), BF16),
            jax.ShapeDtypeStruct((nb, D_CONV), F32),
            jax.ShapeDtypeStruct((nb, D_SSD, D_STATE), F32),
        ],
        compiler_params=_cparams(("arbitrary",)),
        name="ssd_step",
    )(proj, state, cprev, sprev, cw, cb, dtb, alog, dsk, nw, scw)


def _qk_step_kernel(p_ref, cos_ref, sin_ref, qw_ref, kw_ref, q_ref, k_ref):
    bd = _seg_mean_matrix()
    lane = lax.broadcasted_iota(jnp.int32, (p_ref.shape[0], 128), 1)
    first_half = (lane & 32) == 0
    cos = cos_ref[...]
    sin = sin_ref[...]
    for t in range(4):
        qn = _head_norm(p_ref[:, C_Q + 128 * t:C_Q + 128 * (t + 1)], qw_ref[...], bd)
        q_ref[:, 128 * t:128 * (t + 1)] = _rope(qn, cos, sin, first_half) * (HEAD_DIM ** -0.5)
    k_ref[...] = _rope(_head_norm(p_ref[:, C_K:C_K + 128], kw_ref[...], bd), cos, sin, first_half)


def _qk_step(proj, cos_r, sin_r, qw, kw):
    nb = proj.shape[0]
    return pl.pallas_call(
        _qk_step_kernel,
        out_shape=[jax.ShapeDtypeStruct((nb, D_ATTN), F32), jax.ShapeDtypeStruct((nb, 128), F32)],
        name="qk_step",
    )(proj, cos_r, sin_r, qw, kw)


def _attn_step_kernel(q_ref, kc_ref, vc_ref, kold_ref, vold_ref, sink_ref, o_ref):
    q = q_ref[...]
    s = lax.dot_general(q.astype(BF16), kc_ref[...].astype(BF16), (((2,), (2,)), ((0,), (0,))),
                        preferred_element_type=F32)
    s_old = jnp.sum(q * kold_ref[...], axis=-1, keepdims=True)
    sink = sink_ref[...]
    mx = jnp.maximum(jnp.maximum(jnp.max(s, axis=-1, keepdims=True), s_old), sink)
    p = jnp.exp(s - mx)
    p_old = jnp.exp(s_old - mx)
    den = jnp.sum(p, axis=-1, keepdims=True) + p_old + jnp.exp(sink - mx)
    o = lax.dot_general((p / den).astype(BF16), vc_ref[...].astype(BF16), (((2,), (1,)), ((0,), (0,))),
                        preferred_element_type=F32)
    o_ref[...] = o + (p_old / den) * vold_ref[...]


def _attn_step(q3, kc, vc, kold, vold, sink3):
    nb = q3.shape[0]
    return pl.pallas_call(
        _attn_step_kernel,
        out_shape=jax.ShapeDtypeStruct((nb, ATT_HEADS, 128), F32),
        name="attn_step",
    )(q3, kc, vc, kold, vold, sink3)


def _rope_tables(pos):
    half = HEAD_DIM // 2
    inv = jnp.power(ROPE_THETA, -jnp.arange(half, dtype=F32) / half)
    ang = pos[:, None] * inv[None, :]
    cos = jnp.cos(ang)
    sin = jnp.sin(ang)
    cos_t = jnp.concatenate([cos, cos, cos, cos], axis=-1)
    sin_t = jnp.concatenate([-sin, sin, -sin, sin], axis=-1)
    return cos_t, sin_t


def _pad128(v):
    return jnp.pad(v, (0, 128 - v.shape[0])).reshape(1, 128)


def kernel(x_prompt, x_sample, state_ssm, state_ssm_conv, cache_win_k, cache_win_v, state_short_conv,
           state_ffn_conv, c_prompt, c_sample, norm1_w, norm2_w, w_ada, b_ada, w_in, ssm_conv_w, ssm_conv_b,
           dt_bias, a_log, d_skip, ssm_norm_w, q_norm_w, k_norm_w, sinks, sconv_w, w_out, w_up, ffn_conv_w,
           ffn_conv_b, w_down):
    nbp, seq, _ = x_prompt.shape
    nbs = x_sample.shape[0]
    depth = w_in.shape[0]
    mp = nbp * seq

    w_in_r = jnp.concatenate(
        [w_in[:, :, :2560], w_in[:, :, 2576:3088], w_in[:, :, 3344:4880], w_in[:, :, 3088:3344],
         w_in[:, :, 2560:2576], jnp.zeros((depth, D_MODEL, PROJ_P - PROJ_W), w_in.dtype)], axis=-1).astype(BF16)
    w_out_b = w_out.astype(BF16)
    w_up_b = w_up.astype(BF16)
    w_down_b = w_down.astype(BF16)

    c_all = jnp.concatenate([c_prompt, jnp.zeros((8 - nbp, D_MODEL), F32), c_sample], axis=0)
    mod_all = _modulation(c_all, w_ada, b_ada)
    mod_p = mod_all[:, :nbp].reshape(depth, nbp, 1, 6 * D_MODEL)
    mod_s = mod_all[:, 8:].reshape(depth, 1, nbs, 6 * D_MODEL)

    cos_p, sin_p = _rope_tables(jnp.arange(seq, dtype=F32))
    cos_s, sin_s = _rope_tables(jnp.full((1,), float(PAST_LEN), F32))

    head_g = jnp.arange(ATT_HEADS) // (ATT_HEADS // KV_HEADS)
    q_place = (head_g[:, None] == jnp.arange(KV_HEADS)[None, :]).astype(F32)[None, :, :, None]

    xp = x_prompt.reshape(mp, D_MODEL)
    xs = x_sample.reshape(nbs, D_MODEL)
    TM = 512
    hp = _norm_mod(xp, norm1_w[0], mod_p, 0, 1, 0, TM, seq)
    hs = _norm_mod(xs, norm1_w[0], mod_s, 0, 1, 0, nbs, nbs)

    outs_p = [[] for _ in range(6)]
    outs_s = [[] for _ in range(6)]
    for l in range(depth):
        cw, cb = ssm_conv_w[l], ssm_conv_b[l].reshape(1, D_XBC)
        dtb, alog = _pad128(dt_bias[l]), _pad128(a_log[l])
        dsk = jnp.repeat(d_skip[l], SSM_HEAD_DIM).reshape(1, D_SSD)
        nw = ssm_norm_w[l].reshape(1, D_SSD)
        qw = jnp.tile(q_norm_w[l], 2).reshape(1, 128)
        kw = jnp.tile(k_norm_w[l], 2).reshape(1, 128)
        fcw, fcb = ffn_conv_w[l], ffn_conv_b[l].reshape(1, D_FF)
        nw_next = norm1_w[l + 1] if l + 1 < depth else None

        proj = _matmul(hp, w_in_r[l], 1024, 1024)
        y_ssm, y_conv, st_t, cst, ust = _ssd_prompt(proj, nbp, seq, cw, cb, dtb, alog, dsk, nw, sconv_w[l])
        o, k_rot = _attn_prompt(proj, nbp, seq, sinks[l], cos_p, sin_p, qw, kw)
        xp, h2 = _outproj(y_ssm, o, y_conv, w_out_b[l], xp, mod_p, l, norm2_w[l], TM, seq)
        act, gst = _up_seq(h2, w_up_b[l], fcw, fcb, nbp, 1024, 512)
        xp, hp = _down(act, w_down_b[l], xp, mod_p, l, nw_next, TM, 512, seq)

        proj3 = proj.reshape(nbp, seq, PROJ_P)
        outs_p[0].append(st_t.reshape(nbp, SSM_HEADS, SSM_HEAD_DIM, D_STATE))
        outs_p[1].append(cst[:, 8 - (SSM_CONV - 1):])
        outs_p[2].append(k_rot.reshape(nbp, seq, KV_HEADS, HEAD_DIM)[:, seq - WINDOW:])
        outs_p[3].append(proj3[:, seq - WINDOW:, C_V:C_V + 128].reshape(nbp, WINDOW, KV_HEADS, HEAD_DIM))
        outs_p[4].append(ust[:, 6:])
        outs_p[5].append(gst[:, 6:])

        projs = _matmul(hs, w_in_r[l], nbs, 1024)
        cprev = jnp.transpose(state_ssm_conv[l], (1, 0, 2))
        sprev = jnp.transpose(state_short_conv[l], (1, 0, 2))
        y_s, yc_s, u_s, st_new = _ssd_step(
            projs, state_ssm[l].reshape(nbs, D_SSD, D_STATE), cprev, sprev, cw, cb, dtb, alog, dsk, nw,
            sconv_w[l])
        q_r, k_r = _qk_step(projs, cos_s, sin_s, qw, kw)
        v_r = projs[:, C_V:C_V + 128]
        ck = cache_win_k[l].reshape(nbs, -1, 128)
        cv = cache_win_v[l].reshape(nbs, -1, 128)
        kc = jnp.concatenate([ck[:, 1:], k_r[:, None, :]], axis=1)
        vc = jnp.concatenate([cv[:, 1:], v_r[:, None, :]], axis=1)
        q3 = (q_r.reshape(nbs, ATT_HEADS, 1, HEAD_DIM) * q_place).reshape(nbs, ATT_HEADS, 128)
        o3 = _attn_step(q3, kc, vc, ck[:, 0:1], cv[:, 0:1], sinks[l].reshape(1, ATT_HEADS, 1))
        o4 = o3.reshape(nbs, ATT_HEADS, KV_HEADS, HEAD_DIM)
        o_s = jnp.concatenate([o4[:, :4, 0], o4[:, 4:, 1]], axis=1).reshape(nbs, D_ATTN).astype(BF16)
        xs, h2s = _outproj(y_s, o_s, yc_s, w_out_b[l], xs, mod_s, l, norm2_w[l], nbs, nbs)
        fprev = jnp.transpose(state_ffn_conv[l], (1, 0, 2))
        act_s, gt_s = _up_step(h2s, w_up_b[l], fcw, fcb, fprev, 512)
        xs, hs = _down(act_s, w_down_b[l], xs, mod_s, l, nw_next, nbs, 512, nbs)

        outs_s[0].append(st_new.reshape(nbs, SSM_HEADS, SSM_HEAD_DIM, D_STATE))
        outs_s[1].append(jnp.concatenate(
            [state_ssm_conv[l][:, 1:], projs[:, None, C_XBC:C_XBC + D_XBC]], axis=1))
        outs_s[2].append(kc.reshape(nbs, -1, KV_HEADS, HEAD_DIM))
        outs_s[3].append(vc.reshape(nbs, -1, KV_HEADS, HEAD_DIM))
        outs_s[4].append(jnp.concatenate([state_short_conv[l][:, 1:], u_s[:, None]], axis=1))
        outs_s[5].append(jnp.concatenate([state_ffn_conv[l][:, 1:], gt_s[:, None]], axis=1))

    res_p = [jnp.stack(a, axis=0) for a in outs_p]
    res_s = [jnp.stack(a, axis=0) for a in outs_s]
    return (xp.reshape(nbp, seq, D_MODEL), xs.reshape(nbs, 1, D_MODEL), *res_p, *res_s)
```

```python
import functools

import jax
import jax.numpy as jnp
from jax import lax
from jax.experimental import pallas as pl
from jax.experimental.pallas import tpu as pltpu

F32 = jnp.float32
BF16 = jnp.bfloat16

D_MODEL = 2048
D_SSD = 1024
SSM_HEADS = 16
SSM_HEAD_DIM = 64
D_STATE = 128
SSM_CONV = 4
CHUNK = 128
D_XBC = 1536
HEAD_DIM = 64
D_ATTN = 512
ATT_HEADS = 8
KV_HEADS = 2
WINDOW = 128
ROPE_THETA = 10000.0
D_CONV = 512
D_FF = 5632
PROJ_W = 4880
EPS = 1e-6
PAST_LEN = 16384

C_Z, C_XBC, C_Q, C_G, C_K, C_V, C_DT, PROJ_P = 0, 1024, 2560, 3072, 4608, 4736, 4864, 5120

VMEM_LIMIT = 56 * 1024 * 1024


def _cparams(sem):
    return pltpu.CompilerParams(dimension_semantics=sem, vmem_limit_bytes=VMEM_LIMIT)


def _silu(x):
    return x * jax.nn.sigmoid(x)


def _rms_mod(x, w, scale, shift):
    ms = jnp.mean(x * x, axis=-1, keepdims=True)
    return (x * lax.rsqrt(ms + EPS) * w) * (1.0 + scale) + shift


def _split3(x):
    h1 = x.astype(BF16)
    r1 = x - h1.astype(F32)
    h2 = r1.astype(BF16)
    r2 = r1 - h2.astype(F32)
    return h1, h2, r2.astype(BF16)


def _dot(a, b):
    return jnp.dot(a, b, preferred_element_type=F32)


def _dot_nt(a, b):
    return lax.dot_general(a, b, (((1,), (1,)), ((), ())), preferred_element_type=F32)


def _layer_spec(arr, l):
    shape = arr.shape[1:]
    return pl.BlockSpec((None,) + shape, lambda *_: (l,) + (0,) * len(shape))


def _mod_kernel(c_ref, w_ref, b_ref, o_ref):
    a = _silu(c_ref[...]).astype(BF16)
    o_ref[...] = _dot(a, w_ref[...].astype(BF16)) + b_ref[...]


def _modulation(c_all, w_ada, b_ada, tn=1024):
    depth, d, n = w_ada.shape
    r = c_all.shape[0]
    return pl.pallas_call(
        _mod_kernel,
        grid=(depth, n // tn),
        in_specs=[
            pl.BlockSpec((r, d), lambda l, j: (0, 0)),
            pl.BlockSpec((None, d, tn), lambda l, j: (l, 0, j)),
            pl.BlockSpec((None, 1, tn), lambda l, j: (l, 0, j)),
        ],
        out_specs=pl.BlockSpec((None, r, tn), lambda l, j: (l, 0, j)),
        out_shape=jax.ShapeDtypeStruct((depth, r, n), F32),
        compiler_params=_cparams(("arbitrary", "arbitrary")),
        name="modulation",
    )(c_all, w_ada, b_ada.reshape(depth, 1, n))


def _norm_kernel(x_ref, w_ref, sc_ref, sh_ref, o_ref):
    o_ref[...] = _rms_mod(x_ref[...], w_ref[...], sc_ref[...], sh_ref[...]).astype(BF16)


def _mod_spec(mod, l, j, tm, rpb, m_axis):
    r = mod.shape[2]

    def imap(*idx):
        return (l, (idx[m_axis] * tm) // rpb, 0, j)

    return pl.BlockSpec((None, None, r, D_MODEL), imap)


def _norm_mod(x, w_all, mod, l, j_scale, j_shift, tm, rpb):
    m = x.shape[0]
    return pl.pallas_call(
        _norm_kernel,
        grid=(m // tm,),
        in_specs=[
            pl.BlockSpec((tm, D_MODEL), lambda i: (i, 0)),
            _layer_spec(w_all, l),
            _mod_spec(mod, l, j_scale, tm, rpb, 0),
            _mod_spec(mod, l, j_shift, tm, rpb, 0),
        ],
        out_specs=pl.BlockSpec((tm, D_MODEL), lambda i: (i, 0)),
        out_shape=jax.ShapeDtypeStruct((m, D_MODEL), BF16),
        compiler_params=_cparams(("arbitrary",)),
        name="norm_mod",
    )(x, w_all, mod, mod)


def _mm_kernel(a_ref, w_ref, o_ref):
    o_ref[...] = _dot(a_ref[...], w_ref[...])


def _matmul(a, w_all, l, tm, tn):
    m, k = a.shape
    n = w_all.shape[2]
    return pl.pallas_call(
        _mm_kernel,
        grid=(n // tn, m // tm),
        in_specs=[
            pl.BlockSpec((tm, k), lambda j, i: (i, 0)),
            pl.BlockSpec((None, k, tn), lambda j, i: (l, 0, j)),
        ],
        out_specs=pl.BlockSpec((tm, tn), lambda j, i: (i, j)),
        out_shape=jax.ShapeDtypeStruct((m, n), F32),
        compiler_params=_cparams(("arbitrary", "arbitrary")),
        name="in_proj",
    )(a, w_all)


def _ssd_kernel(zx_ref, g_ref, kvdt_ref, cw_ref, cb_ref, dtb_ref, alog_ref, dsk_ref, nw_ref, scw_ref,
                y_ref, yc_ref, st_out_ref, cst_ref, ust_ref,
                xpad_ref, upad_ref, st_ref):
    c = pl.program_id(1)
    nc = pl.num_programs(1)
    L = CHUNK

    @pl.when(c == 0)
    def _():
        xpad_ref[0:8, :] = jnp.zeros((8, D_XBC), F32)
        upad_ref[0:8, :] = jnp.zeros((8, D_CONV), F32)
        st_ref[...] = jnp.zeros_like(st_ref)

    zx = zx_ref[...]
    z = zx[:, :D_SSD]
    raw = zx[:, D_SSD:]

    xpad_ref[8:8 + L, :] = raw
    cw = cw_ref[...]
    conv = (cw[3:4] * raw + cw[2:3] * xpad_ref[7:7 + L, :] + cw[1:2] * xpad_ref[6:6 + L, :]
            + cw[0:1] * xpad_ref[5:5 + L, :])
    xpad_ref[0:8, :] = raw[L - 8:, :]
    cst_ref[...] = raw[L - 8:, :]
    xbc = _silu(conv + cb_ref[...])
    xs = xbc[:, :D_SSD]

    lane = lax.broadcasted_iota(jnp.int32, (L, 128), 1)
    row = lax.broadcasted_iota(jnp.int32, (L, 128), 0)
    lo = (lane & 64) == 0
    causal = row >= lane

    head_ok = lane[0:1] < SSM_HEADS
    a_row = jnp.where(head_ok, -jnp.exp(alog_ref[...]), 0.0)
    dtr = kvdt_ref[:, 256:384] + dtb_ref[...]
    dt = jnp.maximum(dtr, 0.0) + jnp.log1p(jnp.exp(-jnp.abs(dtr)))
    da = dt * a_row

    tril = jnp.where(causal, 1.0, 0.0).astype(BF16)
    d1, d2, d3 = _split3(da)
    cs = _dot(tril, d1) + _dot(tril, d2) + _dot(tril, d3)
    cs_t = cs.T
    ecs = jnp.exp(cs)
    te = jnp.exp(cs[L - 1:L, :] - cs)
    ecl = ecs[L - 1:L, :]

    def pair(arr, j):
        rows = arr.shape[0]
        return jnp.where(lo[:rows], arr[:, 2 * j:2 * j + 1], arr[:, 2 * j + 1:2 * j + 2])

    cbs, yoffs, updates = [], [], []
    xdts = []
    for j in range(8):
        xdts.append(xs[:, 128 * j:128 * (j + 1)] * pair(dt, j))
    for g in range(2):
        bm = xbc[:, D_SSD + 128 * g:D_SSD + 128 * (g + 1)]
        cm = xbc[:, D_SSD + 256 + 128 * g:D_SSD + 256 + 128 * (g + 1)]
        cm16 = cm.astype(BF16)
        cbs.append(_dot_nt(cm16, bm.astype(BF16)))
        yoffs.append(_dot(cm16, st_ref[:, 512 * g:512 * (g + 1)].astype(BF16)))
        xe = jnp.concatenate([xdts[4 * g + i] * pair(te, 4 * g + i) for i in range(4)], axis=1)
        s_new = _dot(bm.T.astype(BF16), xe.astype(BF16))
        cd = jnp.concatenate([pair(ecl, 4 * g + i) for i in range(4)], axis=1)
        updates.append((cd, s_new))
    for g in range(2):
        cd, s_new = updates[g]
        st_ref[:, 512 * g:512 * (g + 1)] = cd * st_ref[:, 512 * g:512 * (g + 1)] + s_new

    ys = []
    for j in range(8):
        g = j // 4
        ca, cb_ = cs[:, 2 * j:2 * j + 1], cs[:, 2 * j + 1:2 * j + 2]
        ra, rb = cs_t[2 * j:2 * j + 1, :], cs_t[2 * j + 1:2 * j + 2, :]
        dec_a = jnp.where(causal, jnp.exp(jnp.minimum(ca - ra, 0.0)), 0.0)
        dec_b = jnp.where(causal, jnp.exp(jnp.minimum(cb_ - rb, 0.0)), 0.0)
        mm = jnp.concatenate([cbs[g] * dec_a, cbs[g] * dec_b], axis=1).astype(BF16)
        xp = xdts[j]
        rhs = jnp.concatenate([jnp.where(lo, xp, 0.0), jnp.where(lo, 0.0, xp)], axis=0).astype(BF16)
        yd = _dot(mm, rhs)
        yo = yoffs[g][:, 128 * (j % 4):128 * (j % 4 + 1)] * pair(ecs, j)
        ys.append(yd + yo + dsk_ref[:, 128 * j:128 * (j + 1)] * xs[:, 128 * j:128 * (j + 1)])
    y = jnp.concatenate(ys, axis=1) * _silu(z)
    nw = nw_ref[...]
    outs = []
    for g in range(2):
        yg = y[:, 512 * g:512 * (g + 1)]
        ms = jnp.mean(yg * yg, axis=-1, keepdims=True)
        outs.append(yg * lax.rsqrt(ms + EPS) * nw[:, 512 * g:512 * (g + 1)])
    y_ref[...] = jnp.concatenate(outs, axis=1).astype(BF16)

    @pl.when(c == nc - 1)
    def _():
        st_out_ref[...] = st_ref[...].T

    gg = g_ref[...]
    u = gg[:, D_CONV:2 * D_CONV] * gg[:, 2 * D_CONV:]
    upad_ref[8:8 + L, :] = u
    sw = scw_ref[...]
    sconv = sw[2:3] * u + sw[1:2] * upad_ref[7:7 + L, :] + sw[0:1] * upad_ref[6:6 + L, :]
    upad_ref[0:8, :] = u[L - 8:, :]
    ust_ref[...] = u[L - 8:, :]
    yc_ref[...] = (gg[:, :D_CONV] * sconv).astype(BF16)


def _ssd_prompt(proj, nb, seq, l, cw, cb, dtb, alog, dsk, nw, scw):
    nc = seq // CHUNK
    m = nb * seq
    row = lambda b, c: (b * nc + c)
    return pl.pallas_call(
        _ssd_kernel,
        grid=(nb, nc),
        in_specs=[
            pl.BlockSpec((CHUNK, 2560), lambda b, c: (row(b, c), 0)),
            pl.BlockSpec((CHUNK, 1536), lambda b, c: (row(b, c), 2)),
            pl.BlockSpec((CHUNK, 512), lambda b, c: (row(b, c), 9)),
            _layer_spec(cw, l), _layer_spec(cb, l), _layer_spec(dtb, l), _layer_spec(alog, l),
            _layer_spec(dsk, l), _layer_spec(nw, l), _layer_spec(scw, l),
        ],
        out_specs=[
            pl.BlockSpec((CHUNK, D_SSD), lambda b, c: (row(b, c), 0)),
            pl.BlockSpec((CHUNK, D_CONV), lambda b, c: (row(b, c), 0)),
            pl.BlockSpec((None, D_SSD, D_STATE), lambda b, c: (b, 0, 0)),
            pl.BlockSpec((None, 8, D_XBC), lambda b, c: (b, 0, 0)),
            pl.BlockSpec((None, 8, D_CONV), lambda b, c: (b, 0, 0)),
        ],
        out_shape=[
            jax.ShapeDtypeStruct((m, D_SSD), BF16),
            jax.ShapeDtypeStruct((m, D_CONV), BF16),
            jax.ShapeDtypeStruct((nb, D_SSD, D_STATE), F32),
            jax.ShapeDtypeStruct((nb, 8, D_XBC), F32),
            jax.ShapeDtypeStruct((nb, 8, D_CONV), F32),
        ],
        scratch_shapes=[
            pltpu.VMEM((8 + CHUNK, D_XBC), F32),
            pltpu.VMEM((8 + CHUNK, D_CONV), F32),
            pltpu.VMEM((D_STATE, D_SSD), F32),
        ],
        compiler_params=_cparams(("arbitrary", "arbitrary")),
        name="ssd_prompt",
    )(proj, proj, proj, cw, cb, dtb, alog, dsk, nw, scw)


def _head_norm(x, w, bd):
    sq = x * x
    hi = sq.astype(BF16)
    lo = (sq - hi.astype(F32)).astype(BF16)
    ms = _dot(hi, bd) + _dot(lo, bd)
    return x * lax.rsqrt(ms + EPS) * w


def _rope(x, cos, sin_signed, first_half):
    partner = jnp.where(first_half, pltpu.roll(x, 96, axis=1), pltpu.roll(x, 32, axis=1))
    return x * cos + partner * sin_signed


def _seg_mean_matrix():
    r = lax.broadcasted_iota(jnp.int32, (128, 128), 0)
    c = lax.broadcasted_iota(jnp.int32, (128, 128), 1)
    return jnp.where((r & 64) == (c & 64), 1.0 / HEAD_DIM, 0.0).astype(BF16)


def _attn_kernel(sink_ref, q_ref, kv_ref, cos_ref, sin_ref, qw_ref, kw_ref,
                 o_ref, kr_ref, kprev_ref, vprev_ref, *, layer):
    c = pl.program_id(1)
    L = CHUNK

    @pl.when(c == 0)
    def _():
        kprev_ref[...] = jnp.zeros_like(kprev_ref)
        vprev_ref[...] = jnp.zeros_like(vprev_ref)

    bd = _seg_mean_matrix()
    lane = lax.broadcasted_iota(jnp.int32, (L, 128), 1)
    lo = (lane & 64) == 0
    first_half = (lane & 32) == 0
    cos = cos_ref[...]
    sin = sin_ref[...]

    k_new = _rope(_head_norm(kv_ref[:, 0:128], kw_ref[...], bd), cos, sin, first_half)
    v_new = kv_ref[:, 128:256]
    kr_ref[...] = k_new
    kband = jnp.concatenate([kprev_ref[...], k_new], axis=0)
    vband = jnp.concatenate([vprev_ref[...], v_new], axis=0)
    kprev_ref[...] = k_new
    vprev_ref[...] = v_new
    lo2 = jnp.concatenate([lo, lo], axis=0)
    kswap = pltpu.roll(kband, 64, axis=1)
    vswap = pltpu.roll(vband, 64, axis=1)
    k2 = [jnp.where(lo2, kband, kswap).astype(BF16), jnp.where(lo2, kswap, kband).astype(BF16)]
    v2 = [jnp.where(lo2, vband, vswap).astype(BF16), jnp.where(lo2, vswap, vband).astype(BF16)]

    qi = lax.broadcasted_iota(jnp.int32, (L, 2 * L), 0)
    kj = lax.broadcasted_iota(jnp.int32, (L, 2 * L), 1)
    diff = L + qi - kj
    valid = (diff >= 0) & (diff <= WINDOW) & ((kj >= L) | (c > 0))

    qt = []
    for t in range(4):
        qn = _head_norm(q_ref[:, 128 * t:128 * (t + 1)], qw_ref[...], bd)
        qt.append(_rope(qn, cos, sin, first_half) * (HEAD_DIM ** -0.5))

    otiles = []
    for g in range(KV_HEADS):
        lhs = []
        for t in (2 * g, 2 * g + 1):
            lhs.append(jnp.where(lo, qt[t], 0.0))
            lhs.append(jnp.where(lo, 0.0, qt[t]))
        s = _dot_nt(jnp.concatenate(lhs, axis=0).astype(BF16), k2[g])
        ps = []
        for i in range(4):
            sink = sink_ref[layer, 4 * g + i]
            sh = jnp.where(valid, s[L * i:L * (i + 1)], -1e30)
            mx = jnp.maximum(jnp.max(sh, axis=-1, keepdims=True), sink)
            p = jnp.exp(sh - mx)
            den = jnp.sum(p, axis=-1, keepdims=True) + jnp.exp(sink - mx)
            ps.append(p / den)
        og = _dot(jnp.concatenate(ps, axis=0).astype(BF16), v2[g])
        otiles.append(jnp.where(lo, og[0:L], og[L:2 * L]))
        otiles.append(jnp.where(lo, og[2 * L:3 * L], og[3 * L:4 * L]))
    o_ref[...] = jnp.concatenate(otiles, axis=1).astype(BF16)


def _attn_prompt(proj, nb, seq, l, sinks, cos_t, sin_t, qw, kw):
    nc = seq // CHUNK
    m = nb * seq
    row = lambda b, c: (b * nc + c)
    return pl.pallas_call(
        functools.partial(_attn_kernel, layer=l),
        grid=(nb, nc),
        in_specs=[
            pl.BlockSpec(memory_space=pltpu.SMEM),
            pl.BlockSpec((CHUNK, 512), lambda b, c: (row(b, c), 5)),
            pl.BlockSpec((CHUNK, 512), lambda b, c: (row(b, c), 9)),
            pl.BlockSpec((CHUNK, 128), lambda b, c: (c, 0)),
            pl.BlockSpec((CHUNK, 128), lambda b, c: (c, 0)),
            _layer_spec(qw, l), _layer_spec(kw, l),
        ],
        out_specs=[
            pl.BlockSpec((CHUNK, D_ATTN), lambda b, c: (row(b, c), 0)),
            pl.BlockSpec((CHUNK, 128), lambda b, c: (row(b, c), 0)),
        ],
        out_shape=[
            jax.ShapeDtypeStruct((m, D_ATTN), BF16),
            jax.ShapeDtypeStruct((m, 128), F32),
        ],
        scratch_shapes=[pltpu.VMEM((CHUNK, 128), F32), pltpu.VMEM((CHUNK, 128), F32)],
        compiler_params=_cparams(("arbitrary", "arbitrary")),
        name="attn_prompt",
    )(sinks, proj, proj, cos_t, sin_t, qw, kw)


def _outproj_kernel(ys_ref, o_ref, yc_ref, w_ref, x_ref, gate_ref, sc_ref, sh_ref, nw_ref, x1_ref, h2_ref):
    acc = _dot(ys_ref[...], w_ref[0:D_SSD, :])
    acc += _dot(o_ref[...], w_ref[D_SSD:D_SSD + D_ATTN, :])
    acc += _dot(yc_ref[...], w_ref[D_SSD + D_ATTN:, :])
    x1 = x_ref[...] + gate_ref[...] * acc
    x1_ref[...] = x1
    h2_ref[...] = _rms_mod(x1, nw_ref[...], sc_ref[...], sh_ref[...]).astype(BF16)


def _outproj(ys, o, yc, w_all, x, mod, l, nw2_all, tm, rpb):
    m = x.shape[0]
    return pl.pallas_call(
        _outproj_kernel,
        grid=(m // tm,),
        in_specs=[
            pl.BlockSpec((tm, D_SSD), lambda i: (i, 0)),
            pl.BlockSpec((tm, D_ATTN), lambda i: (i, 0)),
            pl.BlockSpec((tm, D_CONV), lambda i: (i, 0)),
            pl.BlockSpec((None, D_MODEL, D_MODEL), lambda i: (l, 0, 0), pipeline_mode=pl.Buffered(1)),
            pl.BlockSpec((tm, D_MODEL), lambda i: (i, 0)),
            _mod_spec(mod, l, 2, tm, rpb, 0),
            _mod_spec(mod, l, 4, tm, rpb, 0),
            _mod_spec(mod, l, 3, tm, rpb, 0),
            _layer_spec(nw2_all, l),
        ],
        out_specs=[
            pl.BlockSpec((tm, D_MODEL), lambda i: (i, 0)),
            pl.BlockSpec((tm, D_MODEL), lambda i: (i, 0)),
        ],
        out_shape=[
            jax.ShapeDtypeStruct((m, D_MODEL), F32),
            jax.ShapeDtypeStruct((m, D_MODEL), BF16),
        ],
        compiler_params=_cparams(("arbitrary",)),
        name="out_proj",
    )(ys, o, yc, w_all, x, mod, mod, mod, nw2_all)


def _up_seq_kernel(h_ref, wg_ref, wu_ref, cw_ref, cb_ref, a_ref, gst_ref, pad_ref, wg16_ref, wu16_ref,
                   *, tm, rpb):
    i = pl.program_id(1)

    @pl.when(i == 0)
    def _():
        wg16_ref[...] = wg_ref[...].astype(BF16)
        wu16_ref[...] = wu_ref[...].astype(BF16)

    @pl.when((i * tm) % rpb == 0)
    def _():
        pad_ref[0:8, :] = jnp.zeros((8, pad_ref.shape[1]), F32)

    h = h_ref[...]
    gt = _dot(h, wg16_ref[...])
    u = _dot(h, wu16_ref[...])
    pad_ref[8:8 + tm, :] = gt
    cw = cw_ref[...]
    conv = cw[2:3] * gt + cw[1:2] * pad_ref[7:7 + tm, :] + cw[0:1] * pad_ref[6:6 + tm, :]
    pad_ref[0:8, :] = gt[tm - 8:, :]
    gst_ref[...] = gt[tm - 8:, :]
    a_ref[...] = (_silu(conv + cb_ref[...]) * u).astype(BF16)


def _up_seq(h, w_up_all, l, cw_all, cb_all, nb, tm, tn):
    m = h.shape[0]
    rpb = m // nb
    nt = D_FF // tn
    return pl.pallas_call(
        functools.partial(_up_seq_kernel, tm=tm, rpb=rpb),
        grid=(nt, m // tm),
        in_specs=[
            pl.BlockSpec((tm, D_MODEL), lambda j, i: (i, 0)),
            pl.BlockSpec((None, D_MODEL, tn), lambda j, i: (l, 0, j)),
            pl.BlockSpec((None, D_MODEL, tn), lambda j, i: (l, 0, j + nt)),
            pl.BlockSpec((None, 3, tn), lambda j, i: (l, 0, j)),
            pl.BlockSpec((None, 1, tn), lambda j, i: (l, 0, j)),
        ],
        out_specs=[
            pl.BlockSpec((tm, tn), lambda j, i: (i, j)),
            pl.BlockSpec((None, 8, tn), lambda j, i: ((i * tm) // rpb, 0, j)),
        ],
        out_shape=[
            jax.ShapeDtypeStruct((m, D_FF), BF16),
            jax.ShapeDtypeStruct((nb, 8, D_FF), F32),
        ],
        scratch_shapes=[
            pltpu.VMEM((8 + tm, tn), F32),
            pltpu.VMEM((D_MODEL, tn), BF16),
            pltpu.VMEM((D_MODEL, tn), BF16),
        ],
        compiler_params=_cparams(("arbitrary", "arbitrary")),
        name="up_proj_seq",
    )(h, w_up_all, w_up_all, cw_all, cb_all)


def _up_step_kernel(h_ref, wg_ref, wu_ref, cw_ref, cb_ref, prev_ref, a_ref, gt_ref):
    h = h_ref[...]
    gt = _dot(h, wg_ref[...].astype(BF16))
    u = _dot(h, wu_ref[...].astype(BF16))
    cw = cw_ref[...]
    conv = cw[2:3] * gt + cw[1:2] * prev_ref[1] + cw[0:1] * prev_ref[0]
    gt_ref[...] = gt
    a_ref[...] = (_silu(conv + cb_ref[...]) * u).astype(BF16)


def _up_step(h, w_up_all, l, cw_all, cb_all, prev_all, tn):
    m = h.shape[0]
    nt = D_FF // tn
    return pl.pallas_call(
        _up_step_kernel,
        grid=(nt,),
        in_specs=[
            pl.BlockSpec((m, D_MODEL), lambda j: (0, 0)),
            pl.BlockSpec((None, D_MODEL, tn), lambda j: (l, 0, j)),
            pl.BlockSpec((None, D_MODEL, tn), lambda j: (l, 0, j + nt)),
            pl.BlockSpec((None, 3, tn), lambda j: (l, 0, j)),
            pl.BlockSpec((None, 1, tn), lambda j: (l, 0, j)),
            pl.BlockSpec((None, 2, m, tn), lambda j: (l, 0, 0, j)),
        ],
        out_specs=[
            pl.BlockSpec((m, tn), lambda j: (0, j)),
            pl.BlockSpec((m, tn), lambda j: (0, j)),
        ],
        out_shape=[
            jax.ShapeDtypeStruct((m, D_FF), BF16),
            jax.ShapeDtypeStruct((m, D_FF), F32),
        ],
        compiler_params=_cparams(("arbitrary",)),
        name="up_proj_step",
    )(h, w_up_all, w_up_all, cw_all, cb_all, prev_all)


def _down_kernel(a_ref, w_ref, x_ref, gate_ref, *rest, with_norm):
    if with_norm:
        sc_ref, sh_ref, nw_ref, x2_ref, hn_ref = rest
    else:
        (x2_ref,) = rest
    x2 = x_ref[...] + gate_ref[...] * _dot(a_ref[...], w_ref[...])
    x2_ref[...] = x2
    if with_norm:
        hn_ref[...] = _rms_mod(x2, nw_ref[...], sc_ref[...], sh_ref[...]).astype(BF16)


def _down(a, w_all, x, mod, l, nw1_all, tm, rpb):
    m = x.shape[0]
    with_norm = l + 1 < w_all.shape[0]
    in_specs = [
        pl.BlockSpec((tm, D_FF), lambda i: (i, 0)),
        pl.BlockSpec((None, D_FF, D_MODEL), lambda i: (l, 0, 0), pipeline_mode=pl.Buffered(1)),
        pl.BlockSpec((tm, D_MODEL), lambda i: (i, 0)),
        _mod_spec(mod, l, 5, tm, rpb, 0),
    ]
    args = [a, w_all, x, mod]
    out_specs = [pl.BlockSpec((tm, D_MODEL), lambda i: (i, 0))]
    out_shape = [jax.ShapeDtypeStruct((m, D_MODEL), F32)]
    if with_norm:
        in_specs += [
            _mod_spec(mod, l + 1, 1, tm, rpb, 0),
            _mod_spec(mod, l + 1, 0, tm, rpb, 0),
            _layer_spec(nw1_all, l + 1),
        ]
        args += [mod, mod, nw1_all]
        out_specs.append(pl.BlockSpec((tm, D_MODEL), lambda i: (i, 0)))
        out_shape.append(jax.ShapeDtypeStruct((m, D_MODEL), BF16))
    res = pl.pallas_call(
        functools.partial(_down_kernel, with_norm=with_norm),
        grid=(m // tm,),
        in_specs=in_specs,
        out_specs=out_specs,
        out_shape=out_shape,
        compiler_params=_cparams(("arbitrary",)),
        name="down_proj",
    )(*args)
    return (res[0], res[1]) if with_norm else (res[0], None)


def _expand_matrix():
    r = lax.broadcasted_iota(jnp.int32, (128, D_SSD), 0)
    c = lax.broadcasted_iota(jnp.int32, (128, D_SSD), 1)
    return jnp.where(r == (c >> 6), 1.0, 0.0).astype(BF16)


def _ssd_step_kernel(p_ref, st_ref, cprev_ref, sprev_ref, cw_ref, cb_ref, dtb_ref, alog_ref, dsk_ref, nw_ref,
                     scw_ref, y_ref, yc_ref, u_ref, stn_ref):
    nbb = p_ref.shape[0]
    z = p_ref[:, C_Z:C_Z + D_SSD]
    raw = p_ref[:, C_XBC:C_XBC + D_XBC]
    cw = cw_ref[...]
    conv = cw[3:4] * raw + cw[2:3] * cprev_ref[2] + cw[1:2] * cprev_ref[1] + cw[0:1] * cprev_ref[0]
    xbc = _silu(conv + cb_ref[...])
    xs = xbc[:, :D_SSD]
    bm = xbc[:, D_SSD:D_SSD + 256]
    cm = xbc[:, D_SSD + 256:]

    lane = lax.broadcasted_iota(jnp.int32, (1, 128), 1)
    a_row = jnp.where(lane < SSM_HEADS, -jnp.exp(alog_ref[...]), 0.0)
    dtr = p_ref[:, C_DT:C_DT + 128] + dtb_ref[...]
    dt = jnp.maximum(dtr, 0.0) + jnp.log1p(jnp.exp(-jnp.abs(dtr)))
    dec = jnp.exp(dt * a_row)
    em = _expand_matrix()

    def expand(v):
        v1, v2, v3 = _split3(v)
        return _dot(v1, em) + _dot(v2, em) + _dot(v3, em)

    xdt_t = (xs * expand(dt)).T
    dec_t = expand(dec).T
    rowid = lax.broadcasted_iota(jnp.int32, (nbb, 512), 0)
    ysum = [jnp.zeros((nbb, 512), F32), jnp.zeros((nbb, 512), F32)]
    for i in range(nbb):
        for g in range(2):
            rs = slice(512 * g, 512 * (g + 1))
            hn = st_ref[i, rs, :] * dec_t[rs, i:i + 1] + xdt_t[rs, i:i + 1] * bm[i:i + 1, 128 * g:128 * (g + 1)]
            stn_ref[i, rs, :] = hn
            r = _dot_nt(cm[:, 128 * g:128 * (g + 1)].astype(BF16), hn.astype(BF16))
            ysum[g] = ysum[g] + jnp.where(rowid == i, r, 0.0)
    y = jnp.concatenate(ysum, axis=1) + dsk_ref[...] * xs
    y = y * _silu(z)
    nw = nw_ref[...]
    outs = []
    for g in range(2):
        yg = y[:, 512 * g:512 * (g + 1)]
        ms = jnp.mean(yg * yg, axis=-1, keepdims=True)
        outs.append(yg * lax.rsqrt(ms + EPS) * nw[:, 512 * g:512 * (g + 1)])
    y_ref[...] = jnp.concatenate(outs, axis=1).astype(BF16)

    u = p_ref[:, C_G + D_CONV:C_G + 2 * D_CONV] * p_ref[:, C_G + 2 * D_CONV:C_G + 3 * D_CONV]
    sw = scw_ref[...]
    sconv = sw[2:3] * u + sw[1:2] * sprev_ref[1] + sw[0:1] * sprev_ref[0]
    u_ref[...] = u
    yc_ref[...] = (p_ref[:, C_G:C_G + D_CONV] * sconv).astype(BF16)


def _ssd_step(proj, state_all, cprev_all, sprev_all, l, cw, cb, dtb, alog, dsk, nw, scw, nbb=8):
    nb = proj.shape[0]
    return pl.pallas_call(
        _ssd_step_kernel,
        grid=(nb // nbb,),
        in_specs=[
            pl.BlockSpec((nbb, PROJ_P), lambda i: (i, 0)),
            pl.BlockSpec((None, nbb, D_SSD, D_STATE), lambda i: (l, i, 0, 0)),
            pl.BlockSpec((None, 3, nbb, D_XBC), lambda i: (l, 0, i, 0)),
            pl.BlockSpec((None, 2, nbb, D_CONV), lambda i: (l, 0, i, 0)),
            _layer_spec(cw, l), _layer_spec(cb, l), _layer_spec(dtb, l), _layer_spec(alog, l),
            _layer_spec(dsk, l), _layer_spec(nw, l), _layer_spec(scw, l),
        ],
        out_specs=[
            pl.BlockSpec((nbb, D_SSD), lambda i: (i, 0)),
            pl.BlockSpec((nbb, D_CONV), lambda i: (i, 0)),
            pl.BlockSpec((nbb, D_CONV), lambda i: (i, 0)),
            pl.BlockSpec((nbb, D_SSD, D_STATE), lambda i: (i, 0, 0)),
        ],
        out_shape=[
            jax.ShapeDtypeStruct((nb, D_SSD), BF16),
            jax.ShapeDtypeStruct((nb, D_CONV), BF16),
            jax.ShapeDtypeStruct((nb, D_CONV), F32),
            jax.ShapeDtypeStruct((nb, D_SSD, D_STATE), F32),
        ],
        compiler_params=_cparams(("arbitrary",)),
        name="ssd_step",
    )(proj, state_all, cprev_all, sprev_all, cw, cb, dtb, alog, dsk, nw, scw)


def _qk_step_kernel(p_ref, cos_ref, sin_ref, qw_ref, kw_ref, q_ref, k_ref):
    bd = _seg_mean_matrix()
    lane = lax.broadcasted_iota(jnp.int32, (p_ref.shape[0], 128), 1)
    first_half = (lane & 32) == 0
    cos = cos_ref[...]
    sin = sin_ref[...]
    for t in range(4):
        qn = _head_norm(p_ref[:, C_Q + 128 * t:C_Q + 128 * (t + 1)], qw_ref[...], bd)
        q_ref[:, 128 * t:128 * (t + 1)] = _rope(qn, cos, sin, first_half) * (HEAD_DIM ** -0.5)
    k_ref[...] = _rope(_head_norm(p_ref[:, C_K:C_K + 128], kw_ref[...], bd), cos, sin, first_half)


def _qk_step(proj, cos_r, sin_r, l, qw, kw):
    nb = proj.shape[0]
    full = lambda a: pl.BlockSpec(a.shape, lambda i: (0,) * a.ndim)
    return pl.pallas_call(
        _qk_step_kernel,
        grid=(1,),
        in_specs=[full(proj), full(cos_r), full(sin_r), _layer_spec(qw, l), _layer_spec(kw, l)],
        out_specs=[pl.BlockSpec((nb, D_ATTN), lambda i: (0, 0)), pl.BlockSpec((nb, 128), lambda i: (0, 0))],
        out_shape=[jax.ShapeDtypeStruct((nb, D_ATTN), F32), jax.ShapeDtypeStruct((nb, 128), F32)],
        name="qk_step",
    )(proj, cos_r, sin_r, qw, kw)


def _attn_step_kernel(q_ref, kc_ref, vc_ref, knew_ref, vnew_ref, sink_ref, o_ref):
    q = q_ref[...]
    s = lax.dot_general(q.astype(BF16), kc_ref[...].astype(BF16), (((2,), (2,)), ((0,), (0,))),
                        preferred_element_type=F32)
    s_new = jnp.sum(q * knew_ref[...], axis=-1, keepdims=True)
    sink = sink_ref[...]
    mx = jnp.maximum(jnp.maximum(jnp.max(s, axis=-1, keepdims=True), s_new), sink)
    p = jnp.exp(s - mx)
    p_new = jnp.exp(s_new - mx)
    den = jnp.sum(p, axis=-1, keepdims=True) + p_new + jnp.exp(sink - mx)
    o = lax.dot_general((p / den).astype(BF16), vc_ref[...].astype(BF16), (((2,), (1,)), ((0,), (0,))),
                        preferred_element_type=F32)
    o_ref[...] = o + (p_new / den) * vnew_ref[...]


def _attn_step(q3, kc_all, vc_all, l, knew, vnew, sink_all):
    nb = q3.shape[0]
    full = lambda a: pl.BlockSpec(a.shape, lambda i: (0,) * a.ndim)
    return pl.pallas_call(
        _attn_step_kernel,
        grid=(1,),
        in_specs=[full(q3), _layer_spec(kc_all, l), _layer_spec(vc_all, l), full(knew), full(vnew),
                  _layer_spec(sink_all, l)],
        out_specs=pl.BlockSpec((nb, ATT_HEADS, 128), lambda i: (0, 0, 0)),
        out_shape=jax.ShapeDtypeStruct((nb, ATT_HEADS, 128), F32),
        name="attn_step",
    )(q3, kc_all, vc_all, knew, vnew, sink_all)


def _rope_tables(pos):
    half = HEAD_DIM // 2
    inv = jnp.power(ROPE_THETA, -jnp.arange(half, dtype=F32) / half)
    ang = pos[:, None] * inv[None, :]
    cos = jnp.cos(ang)
    sin = jnp.sin(ang)
    cos_t = jnp.concatenate([cos, cos, cos, cos], axis=-1)
    sin_t = jnp.concatenate([-sin, sin, -sin, sin], axis=-1)
    return cos_t, sin_t


def kernel(x_prompt, x_sample, state_ssm, state_ssm_conv, cache_win_k, cache_win_v, state_short_conv,
           state_ffn_conv, c_prompt, c_sample, norm1_w, norm2_w, w_ada, b_ada, w_in, ssm_conv_w, ssm_conv_b,
           dt_bias, a_log, d_skip, ssm_norm_w, q_norm_w, k_norm_w, sinks, sconv_w, w_out, w_up, ffn_conv_w,
           ffn_conv_b, w_down):
    nbp, seq, _ = x_prompt.shape
    nbs = x_sample.shape[0]
    depth = w_in.shape[0]
    mp = nbp * seq
    nbuf = cache_win_k.shape[2]

    w_in_r = jnp.concatenate(
        [w_in[:, :, :2560], w_in[:, :, 2576:3088], w_in[:, :, 3344:4880], w_in[:, :, 3088:3344],
         w_in[:, :, 2560:2576], jnp.zeros((depth, D_MODEL, PROJ_P - PROJ_W), w_in.dtype)], axis=-1).astype(BF16)
    w_out_b = w_out.astype(BF16)
    w_down_b = w_down.astype(BF16)

    n1 = norm1_w.reshape(depth, 1, D_MODEL)
    n2 = norm2_w.reshape(depth, 1, D_MODEL)
    cb = ssm_conv_b.reshape(depth, 1, D_XBC)
    dtb = jnp.pad(dt_bias, ((0, 0), (0, 128 - SSM_HEADS))).reshape(depth, 1, 128)
    alog = jnp.pad(a_log, ((0, 0), (0, 128 - SSM_HEADS))).reshape(depth, 1, 128)
    dsk = jnp.repeat(d_skip, SSM_HEAD_DIM, axis=1).reshape(depth, 1, D_SSD)
    nw = ssm_norm_w.reshape(depth, 1, D_SSD)
    qw = jnp.tile(q_norm_w, (1, 2)).reshape(depth, 1, 128)
    kw = jnp.tile(k_norm_w, (1, 2)).reshape(depth, 1, 128)
    fcb = ffn_conv_b.reshape(depth, 1, D_FF)
    sink3 = sinks.reshape(depth, ATT_HEADS, 1)

    st_all = state_ssm.reshape(depth, nbs, D_SSD, D_STATE)
    cprev = jnp.transpose(state_ssm_conv, (0, 2, 1, 3))
    sprev = jnp.transpose(state_short_conv, (0, 2, 1, 3))
    fprev = jnp.transpose(state_ffn_conv, (0, 2, 1, 3))
    ck = cache_win_k.reshape(depth, nbs, nbuf, 128)
    cv = cache_win_v.reshape(depth, nbs, nbuf, 128)

    c_all = jnp.concatenate([c_prompt, jnp.zeros((8 - nbp, D_MODEL), F32), c_sample], axis=0)
    mod_all = _modulation(c_all, w_ada, b_ada)
    mod_p = mod_all[:, :nbp].reshape(depth, nbp, 1, 6 * D_MODEL)
    mod_s = mod_all[:, 8:].reshape(depth, 1, nbs, 6 * D_MODEL)

    cos_p, sin_p = _rope_tables(jnp.arange(seq, dtype=F32))
    cos_s, sin_s = _rope_tables(jnp.full((1,), float(PAST_LEN), F32))

    head_g = jnp.arange(ATT_HEADS) // (ATT_HEADS // KV_HEADS)
    q_place = (head_g[:, None] == jnp.arange(KV_HEADS)[None, :]).astype(F32)[None, :, :, None]

    xp = x_prompt.reshape(mp, D_MODEL)
    xs = x_sample.reshape(nbs, D_MODEL)
    TM = 512
    hp = _norm_mod(xp, n1, mod_p, 0, 1, 0, TM, seq)
    hs = _norm_mod(xs, n1, mod_s, 0, 1, 0, nbs, nbs)

    outs_p = [[] for _ in range(6)]
    outs_s = [[] for _ in range(6)]
    for l in range(depth):
        proj = _matmul(hp, w_in_r, l, 1024, 1024)
        y_ssm, y_conv, st_t, cst, ust = _ssd_prompt(proj, nbp, seq, l, ssm_conv_w, cb, dtb, alog, dsk, nw,
                                                    sconv_w)
        o, k_rot = _attn_prompt(proj, nbp, seq, l, sinks, cos_p, sin_p, qw, kw)
        xp, h2 = _outproj(y_ssm, o, y_conv, w_out_b, xp, mod_p, l, n2, TM, seq)
        act, gst = _up_seq(h2, w_up, l, ffn_conv_w, fcb, nbp, 1024, 512)
        xp, hp = _down(act, w_down_b, xp, mod_p, l, n1, 256, seq)

        proj3 = proj.reshape(nbp, seq, PROJ_P)
        outs_p[0].append(st_t.reshape(nbp, SSM_HEADS, SSM_HEAD_DIM, D_STATE))
        outs_p[1].append(cst[:, 8 - (SSM_CONV - 1):])
        outs_p[2].append(k_rot.reshape(nbp, seq, KV_HEADS, HEAD_DIM)[:, seq - WINDOW:])
        outs_p[3].append(proj3[:, seq - WINDOW:, C_V:C_V + 128].reshape(nbp, WINDOW, KV_HEADS, HEAD_DIM))
        outs_p[4].append(ust[:, 6:])
        outs_p[5].append(gst[:, 6:])

        projs = _matmul(hs, w_in_r, l, nbs, 1024)
        y_s, yc_s, u_s, st_new = _ssd_step(projs, st_all, cprev, sprev, l, ssm_conv_w, cb, dtb, alog, dsk, nw,
                                           sconv_w)
        q_r, k_r = _qk_step(projs, cos_s, sin_s, l, qw, kw)
        v_r = projs[:, C_V:C_V + 128]
        q3 = (q_r.reshape(nbs, ATT_HEADS, 1, HEAD_DIM) * q_place).reshape(nbs, ATT_HEADS, 128)
        o3 = _attn_step(q3, ck, cv, l, k_r[:, None, :], v_r[:, None, :], sink3)
        o4 = o3.reshape(nbs, ATT_HEADS, KV_HEADS, HEAD_DIM)
        o_s = jnp.concatenate([o4[:, :4, 0], o4[:, 4:, 1]], axis=1).reshape(nbs, D_ATTN).astype(BF16)
        xs, h2s = _outproj(y_s, o_s, yc_s, w_out_b, xs, mod_s, l, n2, nbs, nbs)
        act_s, gt_s = _up_step(h2s, w_up, l, ffn_conv_w, fcb, fprev, 512)
        xs, hs = _down(act_s, w_down_b, xs, mod_s, l, n1, nbs, nbs)

        outs_s[0].append(st_new.reshape(nbs, SSM_HEADS, SSM_HEAD_DIM, D_STATE))
        outs_s[1].append(projs[:, C_XBC:C_XBC + D_XBC])
        outs_s[2].append(k_r)
        outs_s[3].append(v_r)
        outs_s[4].append(u_s)
        outs_s[5].append(gt_s)

    res_p = [jnp.stack(a, axis=0) for a in outs_p]
    new_rows = [jnp.stack(a, axis=0) for a in outs_s]
    s_ssm = new_rows[0]
    s_ssm_conv = jnp.concatenate([state_ssm_conv[:, :, 1:], new_rows[1][:, :, None]], axis=2)
    s_win_k = jnp.concatenate([ck[:, :, 1:], new_rows[2][:, :, None]], axis=2).reshape(cache_win_k.shape)
    s_win_v = jnp.concatenate([cv[:, :, 1:], new_rows[3][:, :, None]], axis=2).reshape(cache_win_v.shape)
    s_short = jnp.concatenate([state_short_conv[:, :, 1:], new_rows[4][:, :, None]], axis=2)
    s_ffn = jnp.concatenate([state_ffn_conv[:, :, 1:], new_rows[5][:, :, None]], axis=2)
    return (xp.reshape(nbp, seq, D_MODEL), xs.reshape(nbs, 1, D_MODEL), *res_p,
            s_ssm, s_ssm_conv, s_win_k, s_win_v, s_short, s_ffn)
```

```python
import functools

import jax
import jax.numpy as jnp
from jax import lax
from jax.experimental import pallas as pl
from jax.experimental.pallas import tpu as pltpu

F32 = jnp.float32
BF16 = jnp.bfloat16

D_MODEL = 2048
D_SSD = 1024
SSM_HEADS = 16
SSM_HEAD_DIM = 64
D_STATE = 128
SSM_CONV = 4
CHUNK = 128
D_XBC = 1536
HEAD_DIM = 64
D_ATTN = 512
ATT_HEADS = 8
KV_HEADS = 2
WINDOW = 128
ROPE_THETA = 10000.0
D_CONV = 512
D_FF = 5632
PROJ_W = 4880
EPS = 1e-6
PAST_LEN = 16384

C_Z, C_XBC, C_Q, C_G, C_K, C_V, C_DT, PROJ_P = 0, 1024, 2560, 3072, 4608, 4736, 4864, 5120

VMEM_LIMIT = 56 * 1024 * 1024


def _cparams(sem):
    return pltpu.CompilerParams(dimension_semantics=sem, vmem_limit_bytes=VMEM_LIMIT)


def _silu(x):
    return x * jax.nn.sigmoid(x)


def _rms_mod(x, w, scale, shift):
    ms = jnp.mean(x * x, axis=-1, keepdims=True)
    return (x * lax.rsqrt(ms + EPS) * w) * (1.0 + scale) + shift


def _split3(x):
    h1 = x.astype(BF16)
    r1 = x - h1.astype(F32)
    h2 = r1.astype(BF16)
    r2 = r1 - h2.astype(F32)
    return h1, h2, r2.astype(BF16)


def _dot(a, b):
    return jnp.dot(a, b, preferred_element_type=F32)


def _dot_nt(a, b):
    return lax.dot_general(a, b, (((1,), (1,)), ((), ())), preferred_element_type=F32)


def _layer_spec(arr, l):
    shape = arr.shape[1:]
    return pl.BlockSpec((None,) + shape, lambda *_: (l,) + (0,) * len(shape))


def _mod_kernel(c_ref, w_ref, b_ref, o_ref):
    a = _silu(c_ref[...]).astype(BF16)
    o_ref[...] = _dot(a, w_ref[...].astype(BF16)) + b_ref[...]


def _modulation(c_all, w_ada, b_ada, tn=1024):
    depth, d, n = w_ada.shape
    r = c_all.shape[0]
    return pl.pallas_call(
        _mod_kernel,
        grid=(depth, n // tn),
        in_specs=[
            pl.BlockSpec((r, d), lambda l, j: (0, 0)),
            pl.BlockSpec((None, d, tn), lambda l, j: (l, 0, j)),
            pl.BlockSpec((None, 1, tn), lambda l, j: (l, 0, j)),
        ],
        out_specs=pl.BlockSpec((None, r, tn), lambda l, j: (l, 0, j)),
        out_shape=jax.ShapeDtypeStruct((depth, r, n), F32),
        compiler_params=_cparams(("arbitrary", "arbitrary")),
        name="modulation",
    )(c_all, w_ada, b_ada.reshape(depth, 1, n))


def _norm_kernel(x_ref, w_ref, sc_ref, sh_ref, o_ref):
    o_ref[...] = _rms_mod(x_ref[...], w_ref[...], sc_ref[...], sh_ref[...]).astype(BF16)


def _mod_spec(mod, l, j, tm, rpb, m_axis):
    r = mod.shape[2]

    def imap(*idx):
        return (l, (idx[m_axis] * tm) // rpb, 0, j)

    return pl.BlockSpec((None, None, r, D_MODEL), imap)


def _norm_mod(x, w_all, mod, l, j_scale, j_shift, tm, rpb):
    m = x.shape[0]
    return pl.pallas_call(
        _norm_kernel,
        grid=(m // tm,),
        in_specs=[
            pl.BlockSpec((tm, D_MODEL), lambda i: (i, 0)),
            _layer_spec(w_all, l),
            _mod_spec(mod, l, j_scale, tm, rpb, 0),
            _mod_spec(mod, l, j_shift, tm, rpb, 0),
        ],
        out_specs=pl.BlockSpec((tm, D_MODEL), lambda i: (i, 0)),
        out_shape=jax.ShapeDtypeStruct((m, D_MODEL), BF16),
        compiler_params=_cparams(("arbitrary",)),
        name="norm_mod",
    )(x, w_all, mod, mod)


def _in_proj_table():
    segs = [(C_Z, C_Q, 0), (C_Q, C_G, 2576), (C_G, C_K, 3344), (C_K, C_DT, 3088), (C_DT, PROJ_P, 2560)]
    a, b, s = [], [], []
    for st in range(PROJ_P // 256):
        c = st * 256
        lo, _, src = [g for g in segs if g[0] <= c < g[1]][0]
        o = src + c - lo
        sh = o % 128
        a.append((o - sh) // 256)
        b.append((o - sh + 256) // 128 if sh else 0)
        s.append(1 if sh else 0)
    return jnp.array([a, b, s], jnp.int32)


def _in_proj_kernel(tab_ref, h_ref, hs_ref, a0_ref, b0_ref, a1_ref, b1_ref, o_ref, os_ref, w16_ref):
    j = pl.program_id(0)
    i = pl.program_id(1)

    @pl.when(i == 0)
    def _():
        for s, (a_ref, b_ref) in enumerate(((a0_ref, b0_ref), (a1_ref, b1_ref))):
            shifted = tab_ref[2, 2 * j + s]

            @pl.when(shifted == 0)
            def _():
                w16_ref[:, 256 * s:256 * (s + 1)] = a_ref[...].astype(BF16)

            @pl.when(shifted != 0)
            def _():
                lane = lax.broadcasted_iota(jnp.int32, (D_MODEL, 128), 1)
                r = [pltpu.roll(a_ref[:, 0:128], 112, axis=1), pltpu.roll(a_ref[:, 128:256], 112, axis=1),
                     pltpu.roll(b_ref[...], 112, axis=1)]
                for c in range(2):
                    w16_ref[:, 256 * s + 128 * c:256 * s + 128 * (c + 1)] = (
                        jnp.where(lane < 112, r[c], r[c + 1]).astype(BF16))

        os_ref[...] = _dot(hs_ref[...], w16_ref[...])

    o_ref[...] = _dot(h_ref[...], w16_ref[...])


def _in_proj(h, hs, w_in, l, tab, tm):
    m, k = h.shape
    ms = hs.shape[0]
    tn = 512
    a_spec = lambda s: pl.BlockSpec((None, k, 256), lambda j, i, t: (l, 0, t[0, 2 * j + s]))
    b_spec = lambda s: pl.BlockSpec((None, k, 128), lambda j, i, t: (l, 0, t[1, 2 * j + s]))
    return pl.pallas_call(
        _in_proj_kernel,
        grid_spec=pltpu.PrefetchScalarGridSpec(
            num_scalar_prefetch=1,
            grid=(PROJ_P // tn, m // tm),
            in_specs=[
                pl.BlockSpec((tm, k), lambda j, i, t: (i, 0)),
                pl.BlockSpec((ms, k), lambda j, i, t: (0, 0)),
                a_spec(0), b_spec(0), a_spec(1), b_spec(1),
            ],
            out_specs=[
                pl.BlockSpec((tm, tn), lambda j, i, t: (i, j)),
                pl.BlockSpec((ms, tn), lambda j, i, t: (0, j)),
            ],
            scratch_shapes=[pltpu.VMEM((k, tn), BF16)],
        ),
        out_shape=[jax.ShapeDtypeStruct((m, PROJ_P), F32), jax.ShapeDtypeStruct((ms, PROJ_P), F32)],
        compiler_params=_cparams(("arbitrary", "arbitrary")),
        name="in_proj",
    )(tab, h, hs, w_in, w_in, w_in, w_in)


def _ssd_kernel(zx_ref, g_ref, kvdt_ref, cw_ref, cb_ref, dtb_ref, alog_ref, dsk_ref, nw_ref, scw_ref,
                y_ref, yc_ref, st_out_ref, cst_ref, ust_ref,
                xpad_ref, upad_ref, st_ref):
    c = pl.program_id(1)
    nc = pl.num_programs(1)
    L = CHUNK

    @pl.when(c == 0)
    def _():
        xpad_ref[0:8, :] = jnp.zeros((8, D_XBC), F32)
        upad_ref[0:8, :] = jnp.zeros((8, D_CONV), F32)
        st_ref[...] = jnp.zeros_like(st_ref)

    zx = zx_ref[...]
    z = zx[:, :D_SSD]
    raw = zx[:, D_SSD:]

    xpad_ref[8:8 + L, :] = raw
    cw = cw_ref[...]
    conv = (cw[3:4] * raw + cw[2:3] * xpad_ref[7:7 + L, :] + cw[1:2] * xpad_ref[6:6 + L, :]
            + cw[0:1] * xpad_ref[5:5 + L, :])
    xpad_ref[0:8, :] = raw[L - 8:, :]
    cst_ref[...] = raw[L - 8:, :]
    xbc = _silu(conv + cb_ref[...])
    xs = xbc[:, :D_SSD]

    lane = lax.broadcasted_iota(jnp.int32, (L, 128), 1)
    row = lax.broadcasted_iota(jnp.int32, (L, 128), 0)
    lo = (lane & 64) == 0
    causal = row >= lane

    head_ok = lane[0:1] < SSM_HEADS
    a_row = jnp.where(head_ok, -jnp.exp(alog_ref[...]), 0.0)
    dtr = kvdt_ref[:, 256:384] + dtb_ref[...]
    dt = jnp.maximum(dtr, 0.0) + jnp.log1p(jnp.exp(-jnp.abs(dtr)))
    da = dt * a_row

    tril = jnp.where(causal, 1.0, 0.0).astype(BF16)
    d1, d2, d3 = _split3(da)
    cs = _dot(tril, d1) + _dot(tril, d2) + _dot(tril, d3)
    cs_t = cs.T
    ecs = jnp.exp(cs)
    te = jnp.exp(cs[L - 1:L, :] - cs)
    ecl = ecs[L - 1:L, :]

    def pair(arr, j):
        rows = arr.shape[0]
        return jnp.where(lo[:rows], arr[:, 2 * j:2 * j + 1], arr[:, 2 * j + 1:2 * j + 2])

    cbs, yoffs, updates = [], [], []
    xdts = []
    for j in range(8):
        xdts.append(xs[:, 128 * j:128 * (j + 1)] * pair(dt, j))
    for g in range(2):
        bm = xbc[:, D_SSD + 128 * g:D_SSD + 128 * (g + 1)]
        cm = xbc[:, D_SSD + 256 + 128 * g:D_SSD + 256 + 128 * (g + 1)]
        cm16 = cm.astype(BF16)
        cbs.append(_dot_nt(cm16, bm.astype(BF16)))
        yoffs.append(_dot(cm16, st_ref[:, 512 * g:512 * (g + 1)].astype(BF16)))
        xe = jnp.concatenate([xdts[4 * g + i] * pair(te, 4 * g + i) for i in range(4)], axis=1)
        s_new = _dot(bm.T.astype(BF16), xe.astype(BF16))
        cd = jnp.concatenate([pair(ecl, 4 * g + i) for i in range(4)], axis=1)
        updates.append((cd, s_new))
    for g in range(2):
        cd, s_new = updates[g]
        st_ref[:, 512 * g:512 * (g + 1)] = cd * st_ref[:, 512 * g:512 * (g + 1)] + s_new

    ys = []
    for j in range(8):
        g = j // 4
        ca, cb_ = cs[:, 2 * j:2 * j + 1], cs[:, 2 * j + 1:2 * j + 2]
        ra, rb = cs_t[2 * j:2 * j + 1, :], cs_t[2 * j + 1:2 * j + 2, :]
        dec_a = jnp.where(causal, jnp.exp(jnp.minimum(ca - ra, 0.0)), 0.0)
        dec_b = jnp.where(causal, jnp.exp(jnp.minimum(cb_ - rb, 0.0)), 0.0)
        mm = jnp.concatenate([cbs[g] * dec_a, cbs[g] * dec_b], axis=1).astype(BF16)
        xp = xdts[j]
        rhs = jnp.concatenate([jnp.where(lo, xp, 0.0), jnp.where(lo, 0.0, xp)], axis=0).astype(BF16)
        yd = _dot(mm, rhs)
        yo = yoffs[g][:, 128 * (j % 4):128 * (j % 4 + 1)] * pair(ecs, j)
        ys.append(yd + yo + dsk_ref[:, 128 * j:128 * (j + 1)] * xs[:, 128 * j:128 * (j + 1)])
    y = jnp.concatenate(ys, axis=1) * _silu(z)
    nw = nw_ref[...]
    outs = []
    for g in range(2):
        yg = y[:, 512 * g:512 * (g + 1)]
        ms = jnp.mean(yg * yg, axis=-1, keepdims=True)
        outs.append(yg * lax.rsqrt(ms + EPS) * nw[:, 512 * g:512 * (g + 1)])
    y_ref[...] = jnp.concatenate(outs, axis=1).astype(BF16)

    @pl.when(c == nc - 1)
    def _():
        st_out_ref[...] = st_ref[...].T

    gg = g_ref[...]
    u = gg[:, D_CONV:2 * D_CONV] * gg[:, 2 * D_CONV:]
    upad_ref[8:8 + L, :] = u
    sw = scw_ref[...]
    sconv = sw[2:3] * u + sw[1:2] * upad_ref[7:7 + L, :] + sw[0:1] * upad_ref[6:6 + L, :]
    upad_ref[0:8, :] = u[L - 8:, :]
    ust_ref[...] = u[L - 8:, :]
    yc_ref[...] = (gg[:, :D_CONV] * sconv).astype(BF16)


def _ssd_prompt(proj, nb, seq, l, cw, cb, dtb, alog, dsk, nw, scw):
    nc = seq // CHUNK
    m = nb * seq
    row = lambda b, c: (b * nc + c)
    return pl.pallas_call(
        _ssd_kernel,
        grid=(nb, nc),
        in_specs=[
            pl.BlockSpec((CHUNK, 2560), lambda b, c: (row(b, c), 0)),
            pl.BlockSpec((CHUNK, 1536), lambda b, c: (row(b, c), 2)),
            pl.BlockSpec((CHUNK, 512), lambda b, c: (row(b, c), 9)),
            _layer_spec(cw, l), _layer_spec(cb, l), _layer_spec(dtb, l), _layer_spec(alog, l),
            _layer_spec(dsk, l), _layer_spec(nw, l), _layer_spec(scw, l),
        ],
        out_specs=[
            pl.BlockSpec((CHUNK, D_SSD), lambda b, c: (row(b, c), 0)),
            pl.BlockSpec((CHUNK, D_CONV), lambda b, c: (row(b, c), 0)),
            pl.BlockSpec((None, D_SSD, D_STATE), lambda b, c: (b, 0, 0)),
            pl.BlockSpec((None, 8, D_XBC), lambda b, c: (b, 0, 0)),
            pl.BlockSpec((None, 8, D_CONV), lambda b, c: (b, 0, 0)),
        ],
        out_shape=[
            jax.ShapeDtypeStruct((m, D_SSD), BF16),
            jax.ShapeDtypeStruct((m, D_CONV), BF16),
            jax.ShapeDtypeStruct((nb, D_SSD, D_STATE), F32),
            jax.ShapeDtypeStruct((nb, 8, D_XBC), F32),
            jax.ShapeDtypeStruct((nb, 8, D_CONV), F32),
        ],
        scratch_shapes=[
            pltpu.VMEM((8 + CHUNK, D_XBC), F32),
            pltpu.VMEM((8 + CHUNK, D_CONV), F32),
            pltpu.VMEM((D_STATE, D_SSD), F32),
        ],
        compiler_params=_cparams(("arbitrary", "arbitrary")),
        name="ssd_prompt",
    )(proj, proj, proj, cw, cb, dtb, alog, dsk, nw, scw)


def _head_norm(x, w, bd):
    sq = x * x
    hi = sq.astype(BF16)
    lo = (sq - hi.astype(F32)).astype(BF16)
    ms = _dot(hi, bd) + _dot(lo, bd)
    return x * lax.rsqrt(ms + EPS) * w


def _rope(x, cos, sin_signed, first_half):
    partner = jnp.where(first_half, pltpu.roll(x, 96, axis=1), pltpu.roll(x, 32, axis=1))
    return x * cos + partner * sin_signed


def _seg_mean_matrix():
    r = lax.broadcasted_iota(jnp.int32, (128, 128), 0)
    c = lax.broadcasted_iota(jnp.int32, (128, 128), 1)
    return jnp.where((r & 64) == (c & 64), 1.0 / HEAD_DIM, 0.0).astype(BF16)


def _attn_kernel(sink_ref, q_ref, kv_ref, cos_ref, sin_ref, qw_ref, kw_ref,
                 o_ref, kr_ref, kprev_ref, vprev_ref, *, layer):
    c = pl.program_id(1)
    L = CHUNK

    @pl.when(c == 0)
    def _():
        kprev_ref[...] = jnp.zeros_like(kprev_ref)
        vprev_ref[...] = jnp.zeros_like(vprev_ref)

    bd = _seg_mean_matrix()
    lane = lax.broadcasted_iota(jnp.int32, (L, 128), 1)
    lo = (lane & 64) == 0
    first_half = (lane & 32) == 0
    cos = cos_ref[...]
    sin = sin_ref[...]

    k_new = _rope(_head_norm(kv_ref[:, 0:128], kw_ref[...], bd), cos, sin, first_half)
    v_new = kv_ref[:, 128:256]
    kr_ref[...] = k_new
    kband = jnp.concatenate([kprev_ref[...], k_new], axis=0)
    vband = jnp.concatenate([vprev_ref[...], v_new], axis=0)
    kprev_ref[...] = k_new
    vprev_ref[...] = v_new
    lo2 = jnp.concatenate([lo, lo], axis=0)
    kswap = pltpu.roll(kband, 64, axis=1)
    vswap = pltpu.roll(vband, 64, axis=1)
    k2 = [jnp.where(lo2, kband, kswap).astype(BF16), jnp.where(lo2, kswap, kband).astype(BF16)]
    v2 = [jnp.where(lo2, vband, vswap).astype(BF16), jnp.where(lo2, vswap, vband).astype(BF16)]

    qi = lax.broadcasted_iota(jnp.int32, (L, 2 * L), 0)
    kj = lax.broadcasted_iota(jnp.int32, (L, 2 * L), 1)
    diff = L + qi - kj
    valid = (diff >= 0) & (diff <= WINDOW) & ((kj >= L) | (c > 0))

    qt = []
    for t in range(4):
        qn = _head_norm(q_ref[:, 128 * t:128 * (t + 1)], qw_ref[...], bd)
        qt.append(_rope(qn, cos, sin, first_half) * (HEAD_DIM ** -0.5))

    otiles = []
    for g in range(KV_HEADS):
        lhs = []
        for t in (2 * g, 2 * g + 1):
            lhs.append(jnp.where(lo, qt[t], 0.0))
            lhs.append(jnp.where(lo, 0.0, qt[t]))
        s = _dot_nt(jnp.concatenate(lhs, axis=0).astype(BF16), k2[g])
        ps = []
        for i in range(4):
            sink = sink_ref[layer, 4 * g + i]
            sh = jnp.where(valid, s[L * i:L * (i + 1)], -1e30)
            mx = jnp.maximum(jnp.max(sh, axis=-1, keepdims=True), sink)
            p = jnp.exp(sh - mx)
            den = jnp.sum(p, axis=-1, keepdims=True) + jnp.exp(sink - mx)
            ps.append(p / den)
        og = _dot(jnp.concatenate(ps, axis=0).astype(BF16), v2[g])
        otiles.append(jnp.where(lo, og[0:L], og[L:2 * L]))
        otiles.append(jnp.where(lo, og[2 * L:3 * L], og[3 * L:4 * L]))
    o_ref[...] = jnp.concatenate(otiles, axis=1).astype(BF16)


def _attn_prompt(proj, nb, seq, l, sinks, cos_t, sin_t, qw, kw):
    nc = seq // CHUNK
    m = nb * seq
    row = lambda b, c: (b * nc + c)
    return pl.pallas_call(
        functools.partial(_attn_kernel, layer=l),
        grid=(nb, nc),
        in_specs=[
            pl.BlockSpec(memory_space=pltpu.SMEM),
            pl.BlockSpec((CHUNK, 512), lambda b, c: (row(b, c), 5)),
            pl.BlockSpec((CHUNK, 512), lambda b, c: (row(b, c), 9)),
            pl.BlockSpec((CHUNK, 128), lambda b, c: (c, 0)),
            pl.BlockSpec((CHUNK, 128), lambda b, c: (c, 0)),
            _layer_spec(qw, l), _layer_spec(kw, l),
        ],
        out_specs=[
            pl.BlockSpec((CHUNK, D_ATTN), lambda b, c: (row(b, c), 0)),
            pl.BlockSpec((CHUNK, 128), lambda b, c: (row(b, c), 0)),
        ],
        out_shape=[
            jax.ShapeDtypeStruct((m, D_ATTN), BF16),
            jax.ShapeDtypeStruct((m, 128), F32),
        ],
        scratch_shapes=[pltpu.VMEM((CHUNK, 128), F32), pltpu.VMEM((CHUNK, 128), F32)],
        compiler_params=_cparams(("arbitrary", "arbitrary")),
        name="attn_prompt",
    )(sinks, proj, proj, cos_t, sin_t, qw, kw)


def _outproj_rows(ys_ref, o_ref, yc_ref, x_ref, gate_ref, sc_ref, sh_ref, w_ref, nw_ref, x1_ref, h2_ref):
    acc = _dot(ys_ref[...], w_ref[0:D_SSD, :])
    acc += _dot(o_ref[...], w_ref[D_SSD:D_SSD + D_ATTN, :])
    acc += _dot(yc_ref[...], w_ref[D_SSD + D_ATTN:, :])
    x1 = x_ref[...] + gate_ref[...] * acc
    x1_ref[...] = x1
    h2_ref[...] = _rms_mod(x1, nw_ref[...], sc_ref[...], sh_ref[...]).astype(BF16)


def _outproj_kernel(*refs):
    p_in, s_in, (w_ref, nw_ref), p_out, s_out = refs[0:7], refs[7:14], refs[14:16], refs[16:18], refs[18:20]

    @pl.when(pl.program_id(0) == 0)
    def _():
        _outproj_rows(*s_in, w_ref, nw_ref, *s_out)

    _outproj_rows(*p_in, w_ref, nw_ref, *p_out)


def _row_specs(widths, tm):
    return [pl.BlockSpec((tm, w), lambda i: (i, 0)) for w in widths]


def _const_specs(widths, rows):
    return [pl.BlockSpec((rows, w), lambda i: (0, 0)) for w in widths]


def _outproj(mix_p, x_p, mod_p, mix_s, x_s, mod_s, w_all, l, nw2_all, tm, rpb):
    m, ms = x_p.shape[0], x_s.shape[0]
    widths = (D_SSD, D_ATTN, D_CONV, D_MODEL)
    mods = lambda mod, t, r: [_mod_spec(mod, l, 2, t, r, 0), _mod_spec(mod, l, 4, t, r, 0),
                              _mod_spec(mod, l, 3, t, r, 0)]
    return pl.pallas_call(
        _outproj_kernel,
        grid=(m // tm,),
        in_specs=(_row_specs(widths, tm) + mods(mod_p, tm, rpb)
                  + _const_specs(widths, ms) + mods(mod_s, 0, 1)
                  + [pl.BlockSpec((None, D_MODEL, D_MODEL), lambda i: (l, 0, 0), pipeline_mode=pl.Buffered(1)),
                     _layer_spec(nw2_all, l)]),
        out_specs=_row_specs((D_MODEL, D_MODEL), tm) + _const_specs((D_MODEL, D_MODEL), ms),
        out_shape=[
            jax.ShapeDtypeStruct((m, D_MODEL), F32), jax.ShapeDtypeStruct((m, D_MODEL), BF16),
            jax.ShapeDtypeStruct((ms, D_MODEL), F32), jax.ShapeDtypeStruct((ms, D_MODEL), BF16),
        ],
        compiler_params=_cparams(("arbitrary",)),
        name="out_proj",
    )(*mix_p, x_p, mod_p, mod_p, mod_p, *mix_s, x_s, mod_s, mod_s, mod_s, w_all, nw2_all)


def _up_seq_kernel(h_ref, hs_ref, wg_ref, wu_ref, cw_ref, cb_ref, prev_ref,
                   a_ref, gst_ref, as_ref, gts_ref, pad_ref, wg16_ref, wu16_ref, *, tm, rpb):
    i = pl.program_id(1)
    cw = cw_ref[...]
    cb = cb_ref[...]

    @pl.when(i == 0)
    def _():
        wg16_ref[...] = wg_ref[...].astype(BF16)
        wu16_ref[...] = wu_ref[...].astype(BF16)
        hs = hs_ref[...]
        gt = _dot(hs, wg16_ref[...])
        u = _dot(hs, wu16_ref[...])
        conv = cw[2:3] * gt + cw[1:2] * prev_ref[1] + cw[0:1] * prev_ref[0]
        gts_ref[...] = gt
        as_ref[...] = (_silu(conv + cb) * u).astype(BF16)

    @pl.when((i * tm) % rpb == 0)
    def _():
        pad_ref[0:8, :] = jnp.zeros((8, pad_ref.shape[1]), F32)

    sub = 512
    for r in range(tm // sub):
        h = h_ref[r * sub:(r + 1) * sub, :]
        gt = _dot(h, wg16_ref[...])
        u = _dot(h, wu16_ref[...])
        base = 8 + r * sub
        pad_ref[base:base + sub, :] = gt
        conv = (cw[2:3] * gt + cw[1:2] * pad_ref[base - 1:base - 1 + sub, :]
                + cw[0:1] * pad_ref[base - 2:base - 2 + sub, :])
        a_ref[r * sub:(r + 1) * sub, :] = (_silu(conv + cb) * u).astype(BF16)
    tail = pad_ref[tm:tm + 8, :]
    pad_ref[0:8, :] = tail
    gst_ref[...] = tail


def _up_proj(h, hs, w_up_all, l, cw_all, cb_all, prev_all, nb, tm, tn):
    m = h.shape[0]
    ms = hs.shape[0]
    rpb = m // nb
    nt = D_FF // tn
    return pl.pallas_call(
        functools.partial(_up_seq_kernel, tm=tm, rpb=rpb),
        grid=(nt, m // tm),
        in_specs=[
            pl.BlockSpec((tm, D_MODEL), lambda j, i: (i, 0)),
            pl.BlockSpec((ms, D_MODEL), lambda j, i: (0, 0)),
            pl.BlockSpec((None, D_MODEL, tn), lambda j, i: (l, 0, j)),
            pl.BlockSpec((None, D_MODEL, tn), lambda j, i: (l, 0, j + nt)),
            pl.BlockSpec((None, 3, tn), lambda j, i: (l, 0, j)),
            pl.BlockSpec((None, 1, tn), lambda j, i: (l, 0, j)),
            pl.BlockSpec((None, 2, ms, tn), lambda j, i: (l, 0, 0, j)),
        ],
        out_specs=[
            pl.BlockSpec((tm, tn), lambda j, i: (i, j)),
            pl.BlockSpec((None, 8, tn), lambda j, i: ((i * tm) // rpb, 0, j)),
            pl.BlockSpec((ms, tn), lambda j, i: (0, j)),
            pl.BlockSpec((ms, tn), lambda j, i: (0, j)),
        ],
        out_shape=[
            jax.ShapeDtypeStruct((m, D_FF), BF16),
            jax.ShapeDtypeStruct((nb, 8, D_FF), F32),
            jax.ShapeDtypeStruct((ms, D_FF), BF16),
            jax.ShapeDtypeStruct((ms, D_FF), F32),
        ],
        scratch_shapes=[
            pltpu.VMEM((8 + tm, tn), F32),
            pltpu.VMEM((D_MODEL, tn), BF16),
            pltpu.VMEM((D_MODEL, tn), BF16),
        ],
        compiler_params=_cparams(("arbitrary", "arbitrary")),
        name="up_proj",
    )(h, hs, w_up_all, w_up_all, cw_all, cb_all, prev_all)


def _down_rows(a_ref, x_ref, gate_ref, mods, w_ref, nw_ref, x2_ref, hn_ref):
    x2 = x_ref[...] + gate_ref[...] * _dot(a_ref[...], w_ref[...])
    x2_ref[...] = x2
    if hn_ref is not None:
        hn_ref[...] = _rms_mod(x2, nw_ref[...], mods[0][...], mods[1][...]).astype(BF16)


def _down_kernel(*refs, with_norm):
    refs = list(refs)
    n_in = 5 if with_norm else 3
    p_in, s_in = refs[0:n_in], refs[n_in:2 * n_in]
    w_ref = refs[2 * n_in]
    nw_ref = refs[2 * n_in + 1] if with_norm else None
    outs = refs[2 * n_in + (2 if with_norm else 1):]
    p_out = (outs[0], outs[1]) if with_norm else (outs[0], None)
    s_out = (outs[2], outs[3]) if with_norm else (outs[1], None)

    @pl.when(pl.program_id(0) == 0)
    def _():
        _down_rows(s_in[0], s_in[1], s_in[2], s_in[3:], w_ref, nw_ref, *s_out)

    _down_rows(p_in[0], p_in[1], p_in[2], p_in[3:], w_ref, nw_ref, *p_out)


def _down(a_p, x_p, mod_p, a_s, x_s, mod_s, w_all, l, nw1_all, tm, rpb):
    m, ms = x_p.shape[0], x_s.shape[0]
    with_norm = l + 1 < w_all.shape[0]

    def mods(mod, t, r):
        sp = [_mod_spec(mod, l, 5, t, r, 0)]
        if with_norm:
            sp += [_mod_spec(mod, l + 1, 1, t, r, 0), _mod_spec(mod, l + 1, 0, t, r, 0)]
        return sp

    n_mod = 3 if with_norm else 1
    in_specs = (_row_specs((D_FF, D_MODEL), tm) + mods(mod_p, tm, rpb)
                + _const_specs((D_FF, D_MODEL), ms) + mods(mod_s, 0, 1)
                + [pl.BlockSpec((None, D_FF, D_MODEL), lambda i: (l, 0, 0), pipeline_mode=pl.Buffered(1))])
    args = [a_p, x_p] + [mod_p] * n_mod + [a_s, x_s] + [mod_s] * n_mod + [w_all]
    out_w = (D_MODEL, D_MODEL) if with_norm else (D_MODEL,)
    out_dt = (F32, BF16) if with_norm else (F32,)
    if with_norm:
        in_specs.append(_layer_spec(nw1_all, l + 1))
        args.append(nw1_all)
    res = pl.pallas_call(
        functools.partial(_down_kernel, with_norm=with_norm),
        grid=(m // tm,),
        in_specs=in_specs,
        out_specs=_row_specs(out_w, tm) + _const_specs(out_w, ms),
        out_shape=([jax.ShapeDtypeStruct((m, D_MODEL), d) for d in out_dt]
                   + [jax.ShapeDtypeStruct((ms, D_MODEL), d) for d in out_dt]),
        compiler_params=_cparams(("arbitrary",)),
        name="down_proj",
    )(*args)
    if with_norm:
        return res[0], res[1], res[2], res[3]
    return res[0], None, res[1], None


def _expand_matrix():
    r = lax.broadcasted_iota(jnp.int32, (128, D_SSD), 0)
    c = lax.broadcasted_iota(jnp.int32, (128, D_SSD), 1)
    return jnp.where(r == (c >> 6), 1.0, 0.0).astype(BF16)


def _ssd_step_kernel(p_ref, st_ref, cprev_ref, sprev_ref, cw_ref, cb_ref, dtb_ref, alog_ref, dsk_ref, nw_ref,
                     scw_ref, y_ref, yc_ref, u_ref, stn_ref):
    nbb = p_ref.shape[0]
    z = p_ref[:, C_Z:C_Z + D_SSD]
    raw = p_ref[:, C_XBC:C_XBC + D_XBC]
    cw = cw_ref[...]
    conv = cw[3:4] * raw + cw[2:3] * cprev_ref[2] + cw[1:2] * cprev_ref[1] + cw[0:1] * cprev_ref[0]
    xbc = _silu(conv + cb_ref[...])
    xs = xbc[:, :D_SSD]
    bm = xbc[:, D_SSD:D_SSD + 256]
    cm = xbc[:, D_SSD + 256:]

    lane = lax.broadcasted_iota(jnp.int32, (1, 128), 1)
    a_row = jnp.where(lane < SSM_HEADS, -jnp.exp(alog_ref[...]), 0.0)
    dtr = p_ref[:, C_DT:C_DT + 128] + dtb_ref[...]
    dt = jnp.maximum(dtr, 0.0) + jnp.log1p(jnp.exp(-jnp.abs(dtr)))
    dec = jnp.exp(dt * a_row)
    em = _expand_matrix()

    def expand(v):
        v1, v2, v3 = _split3(v)
        return _dot(v1, em) + _dot(v2, em) + _dot(v3, em)

    xdt_t = (xs * expand(dt)).T
    dec_t = expand(dec).T
    rowid = lax.broadcasted_iota(jnp.int32, (nbb, 512), 0)
    ysum = [jnp.zeros((nbb, 512), F32), jnp.zeros((nbb, 512), F32)]
    for i in range(nbb):
        for g in range(2):
            rs = slice(512 * g, 512 * (g + 1))
            hn = st_ref[i, rs, :] * dec_t[rs, i:i + 1] + xdt_t[rs, i:i + 1] * bm[i:i + 1, 128 * g:128 * (g + 1)]
            stn_ref[i, rs, :] = hn
            r = _dot_nt(cm[:, 128 * g:128 * (g + 1)].astype(BF16), hn.astype(BF16))
            ysum[g] = ysum[g] + jnp.where(rowid == i, r, 0.0)
    y = jnp.concatenate(ysum, axis=1) + dsk_ref[...] * xs
    y = y * _silu(z)
    nw = nw_ref[...]
    outs = []
    for g in range(2):
        yg = y[:, 512 * g:512 * (g + 1)]
        ms = jnp.mean(yg * yg, axis=-1, keepdims=True)
        outs.append(yg * lax.rsqrt(ms + EPS) * nw[:, 512 * g:512 * (g + 1)])
    y_ref[...] = jnp.concatenate(outs, axis=1).astype(BF16)

    u = p_ref[:, C_G + D_CONV:C_G + 2 * D_CONV] * p_ref[:, C_G + 2 * D_CONV:C_G + 3 * D_CONV]
    sw = scw_ref[...]
    sconv = sw[2:3] * u + sw[1:2] * sprev_ref[1] + sw[0:1] * sprev_ref[0]
    u_ref[...] = u
    yc_ref[...] = (p_ref[:, C_G:C_G + D_CONV] * sconv).astype(BF16)


def _ssd_step(proj, state_all, cprev_all, sprev_all, l, cw, cb, dtb, alog, dsk, nw, scw, nbb=8):
    nb = proj.shape[0]
    return pl.pallas_call(
        _ssd_step_kernel,
        grid=(nb // nbb,),
        in_specs=[
            pl.BlockSpec((nbb, PROJ_P), lambda i: (i, 0)),
            pl.BlockSpec((None, nbb, D_SSD, D_STATE), lambda i: (l, i, 0, 0)),
            pl.BlockSpec((None, 3, nbb, D_XBC), lambda i: (l, 0, i, 0)),
            pl.BlockSpec((None, 2, nbb, D_CONV), lambda i: (l, 0, i, 0)),
            _layer_spec(cw, l), _layer_spec(cb, l), _layer_spec(dtb, l), _layer_spec(alog, l),
            _layer_spec(dsk, l), _layer_spec(nw, l), _layer_spec(scw, l),
        ],
        out_specs=[
            pl.BlockSpec((nbb, D_SSD), lambda i: (i, 0)),
            pl.BlockSpec((nbb, D_CONV), lambda i: (i, 0)),
            pl.BlockSpec((nbb, D_CONV), lambda i: (i, 0)),
            pl.BlockSpec((nbb, D_SSD, D_STATE), lambda i: (i, 0, 0)),
        ],
        out_shape=[
            jax.ShapeDtypeStruct((nb, D_SSD), BF16),
            jax.ShapeDtypeStruct((nb, D_CONV), BF16),
            jax.ShapeDtypeStruct((nb, D_CONV), F32),
            jax.ShapeDtypeStruct((nb, D_SSD, D_STATE), F32),
        ],
        compiler_params=_cparams(("arbitrary",)),
        name="ssd_step",
    )(proj, state_all, cprev_all, sprev_all, cw, cb, dtb, alog, dsk, nw, scw)


def _qk_step_kernel(p_ref, cos_ref, sin_ref, qw_ref, kw_ref, q_ref, k_ref):
    bd = _seg_mean_matrix()
    lane = lax.broadcasted_iota(jnp.int32, (p_ref.shape[0], 128), 1)
    first_half = (lane & 32) == 0
    cos = cos_ref[...]
    sin = sin_ref[...]
    for t in range(4):
        qn = _head_norm(p_ref[:, C_Q + 128 * t:C_Q + 128 * (t + 1)], qw_ref[...], bd)
        q_ref[:, 128 * t:128 * (t + 1)] = _rope(qn, cos, sin, first_half) * (HEAD_DIM ** -0.5)
    k_ref[...] = _rope(_head_norm(p_ref[:, C_K:C_K + 128], kw_ref[...], bd), cos, sin, first_half)


def _qk_step(proj, cos_r, sin_r, l, qw, kw):
    nb = proj.shape[0]
    full = lambda a: pl.BlockSpec(a.shape, lambda i: (0,) * a.ndim)
    return pl.pallas_call(
        _qk_step_kernel,
        grid=(1,),
        in_specs=[full(proj), full(cos_r), full(sin_r), _layer_spec(qw, l), _layer_spec(kw, l)],
        out_specs=[pl.BlockSpec((nb, D_ATTN), lambda i: (0, 0)), pl.BlockSpec((nb, 128), lambda i: (0, 0))],
        out_shape=[jax.ShapeDtypeStruct((nb, D_ATTN), F32), jax.ShapeDtypeStruct((nb, 128), F32)],
        name="qk_step",
    )(proj, cos_r, sin_r, qw, kw)


def _attn_step_kernel(q_ref, kc_ref, vc_ref, knew_ref, vnew_ref, sink_ref, o_ref):
    q = q_ref[...]
    s = lax.dot_general(q.astype(BF16), kc_ref[...].astype(BF16), (((2,), (2,)), ((0,), (0,))),
                        preferred_element_type=F32)
    s_new = jnp.sum(q * knew_ref[...], axis=-1, keepdims=True)
    sink = sink_ref[...]
    mx = jnp.maximum(jnp.maximum(jnp.max(s, axis=-1, keepdims=True), s_new), sink)
    p = jnp.exp(s - mx)
    p_new = jnp.exp(s_new - mx)
    den = jnp.sum(p, axis=-1, keepdims=True) + p_new + jnp.exp(sink - mx)
    o = lax.dot_general((p / den).astype(BF16), vc_ref[...].astype(BF16), (((2,), (1,)), ((0,), (0,))),
                        preferred_element_type=F32)
    o_ref[...] = o + (p_new / den) * vnew_ref[...]


def _attn_step(q3, kc_all, vc_all, l, knew, vnew, sink_all):
    nb = q3.shape[0]
    full = lambda a: pl.BlockSpec(a.shape, lambda i: (0,) * a.ndim)
    return pl.pallas_call(
        _attn_step_kernel,
        grid=(1,),
        in_specs=[full(q3), _layer_spec(kc_all, l), _layer_spec(vc_all, l), full(knew), full(vnew),
                  _layer_spec(sink_all, l)],
        out_specs=pl.BlockSpec((nb, ATT_HEADS, 128), lambda i: (0, 0, 0)),
        out_shape=jax.ShapeDtypeStruct((nb, ATT_HEADS, 128), F32),
        name="attn_step",
    )(q3, kc_all, vc_all, knew, vnew, sink_all)


def _rope_tables(pos):
    half = HEAD_DIM // 2
    inv = jnp.power(ROPE_THETA, -jnp.arange(half, dtype=F32) / half)
    ang = pos[:, None] * inv[None, :]
    cos = jnp.cos(ang)
    sin = jnp.sin(ang)
    cos_t = jnp.concatenate([cos, cos, cos, cos], axis=-1)
    sin_t = jnp.concatenate([-sin, sin, -sin, sin], axis=-1)
    return cos_t, sin_t


def kernel(x_prompt, x_sample, state_ssm, state_ssm_conv, cache_win_k, cache_win_v, state_short_conv,
           state_ffn_conv, c_prompt, c_sample, norm1_w, norm2_w, w_ada, b_ada, w_in, ssm_conv_w, ssm_conv_b,
           dt_bias, a_log, d_skip, ssm_norm_w, q_norm_w, k_norm_w, sinks, sconv_w, w_out, w_up, ffn_conv_w,
           ffn_conv_b, w_down):
    nbp, seq, _ = x_prompt.shape
    nbs = x_sample.shape[0]
    depth = w_in.shape[0]
    mp = nbp * seq
    nbuf = cache_win_k.shape[2]

    in_tab = _in_proj_table()
    w_out_b = w_out.astype(BF16)
    w_down_b = w_down.astype(BF16)

    n1 = norm1_w.reshape(depth, 1, D_MODEL)
    n2 = norm2_w.reshape(depth, 1, D_MODEL)
    cb = ssm_conv_b.reshape(depth, 1, D_XBC)
    dtb = jnp.pad(dt_bias, ((0, 0), (0, 128 - SSM_HEADS))).reshape(depth, 1, 128)
    alog = jnp.pad(a_log, ((0, 0), (0, 128 - SSM_HEADS))).reshape(depth, 1, 128)
    dsk = jnp.repeat(d_skip, SSM_HEAD_DIM, axis=1).reshape(depth, 1, D_SSD)
    nw = ssm_norm_w.reshape(depth, 1, D_SSD)
    qw = jnp.tile(q_norm_w, (1, 2)).reshape(depth, 1, 128)
    kw = jnp.tile(k_norm_w, (1, 2)).reshape(depth, 1, 128)
    fcb = ffn_conv_b.reshape(depth, 1, D_FF)
    sink3 = sinks.reshape(depth, ATT_HEADS, 1)

    st_all = state_ssm.reshape(depth, nbs, D_SSD, D_STATE)
    cprev = jnp.transpose(state_ssm_conv, (0, 2, 1, 3))
    sprev = jnp.transpose(state_short_conv, (0, 2, 1, 3))
    fprev = jnp.transpose(state_ffn_conv, (0, 2, 1, 3))
    ck = cache_win_k.reshape(depth, nbs, nbuf, 128)
    cv = cache_win_v.reshape(depth, nbs, nbuf, 128)

    c_all = jnp.concatenate([c_prompt, jnp.zeros((8 - nbp, D_MODEL), F32), c_sample], axis=0)
    mod_all = _modulation(c_all, w_ada, b_ada)
    mod_p = mod_all[:, :nbp].reshape(depth, nbp, 1, 6 * D_MODEL)
    mod_s = mod_all[:, 8:].reshape(depth, 1, nbs, 6 * D_MODEL)

    cos_p, sin_p = _rope_tables(jnp.arange(seq, dtype=F32))
    cos_s, sin_s = _rope_tables(jnp.full((1,), float(PAST_LEN), F32))

    head_g = jnp.arange(ATT_HEADS) // (ATT_HEADS // KV_HEADS)
    q_place = (head_g[:, None] == jnp.arange(KV_HEADS)[None, :]).astype(F32)[None, :, :, None]

    xp = x_prompt.reshape(mp, D_MODEL)
    xs = x_sample.reshape(nbs, D_MODEL)
    TM = 512
    hp = _norm_mod(xp, n1, mod_p, 0, 1, 0, TM, seq)
    hs = _norm_mod(xs, n1, mod_s, 0, 1, 0, nbs, nbs)

    outs_p = [[] for _ in range(6)]
    outs_s = [[] for _ in range(6)]
    for l in range(depth):
        proj, projs = _in_proj(hp, hs, w_in, l, in_tab, min(2048, seq))

        y_ssm, y_conv, st_t, cst, ust = _ssd_prompt(proj, nbp, seq, l, ssm_conv_w, cb, dtb, alog, dsk, nw,
                                                    sconv_w)
        o, k_rot = _attn_prompt(proj, nbp, seq, l, sinks, cos_p, sin_p, qw, kw)

        y_s, yc_s, u_s, st_new = _ssd_step(projs, st_all, cprev, sprev, l, ssm_conv_w, cb, dtb, alog, dsk, nw,
                                           sconv_w)
        q_r, k_r = _qk_step(projs, cos_s, sin_s, l, qw, kw)
        v_r = projs[:, C_V:C_V + 128]
        q3 = (q_r.reshape(nbs, ATT_HEADS, 1, HEAD_DIM) * q_place).reshape(nbs, ATT_HEADS, 128)
        o3 = _attn_step(q3, ck, cv, l, k_r[:, None, :], v_r[:, None, :], sink3)
        o4 = o3.reshape(nbs, ATT_HEADS, KV_HEADS, HEAD_DIM)
        o_s = jnp.concatenate([o4[:, :4, 0], o4[:, 4:, 1]], axis=1).reshape(nbs, D_ATTN).astype(BF16)

        xp, h2, xs, h2s = _outproj((y_ssm, o, y_conv), xp, mod_p, (y_s, o_s, yc_s), xs, mod_s, w_out_b, l, n2,
                                   TM, seq)
        act, gst, act_s, gt_s = _up_proj(h2, h2s, w_up, l, ffn_conv_w, fcb, fprev, nbp, min(2048, seq), 512)
        xp, hp, xs, hs = _down(act, xp, mod_p, act_s, xs, mod_s, w_down_b, l, n1, 256, seq)

        proj3 = proj.reshape(nbp, seq, PROJ_P)
        outs_p[0].append(st_t.reshape(nbp, SSM_HEADS, SSM_HEAD_DIM, D_STATE))
        outs_p[1].append(cst[:, 8 - (SSM_CONV - 1):])
        outs_p[2].append(k_rot.reshape(nbp, seq, KV_HEADS, HEAD_DIM)[:, seq - WINDOW:])
        outs_p[3].append(proj3[:, seq - WINDOW:, C_V:C_V + 128].reshape(nbp, WINDOW, KV_HEADS, HEAD_DIM))
        outs_p[4].append(ust[:, 6:])
        outs_p[5].append(gst[:, 6:])

        outs_s[0].append(st_new.reshape(nbs, SSM_HEADS, SSM_HEAD_DIM, D_STATE))
        outs_s[1].append(projs[:, C_XBC:C_XBC + D_XBC])
        outs_s[2].append(k_r)
        outs_s[3].append(v_r)
        outs_s[4].append(u_s)
        outs_s[5].append(gt_s)

    res_p = [jnp.stack(a, axis=0) for a in outs_p]
    new_rows = [jnp.stack(a, axis=0) for a in outs_s]
    s_ssm = new_rows[0]
    s_ssm_conv = jnp.concatenate([state_ssm_conv[:, :, 1:], new_rows[1][:, :, None]], axis=2)
    s_win_k = jnp.concatenate([ck[:, :, 1:], new_rows[2][:, :, None]], axis=2).reshape(cache_win_k.shape)
    s_win_v = jnp.concatenate([cv[:, :, 1:], new_rows[3][:, :, None]], axis=2).reshape(cache_win_v.shape)
    s_short = jnp.concatenate([state_short_conv[:, :, 1:], new_rows[4][:, :, None]], axis=2)
    s_ffn = jnp.concatenate([state_ffn_conv[:, :, 1:], new_rows[5][:, :, None]], axis=2)
    return (xp.reshape(nbp, seq, D_MODEL), xs.reshape(nbs, 1, D_MODEL), *res_p,
            s_ssm, s_ssm_conv, s_win_k, s_win_v, s_short, s_ffn)
```

```python
import functools

import jax
import jax.numpy as jnp
from jax import lax
from jax.experimental import pallas as pl
from jax.experimental.pallas import tpu as pltpu

F32 = jnp.float32
BF16 = jnp.bfloat16

D_MODEL = 2048
D_SSD = 1024
SSM_HEADS = 16
SSM_HEAD_DIM = 64
D_STATE = 128
SSM_CONV = 4
CHUNK = 128
D_XBC = 1536
HEAD_DIM = 64
D_ATTN = 512
ATT_HEADS = 8
KV_HEADS = 2
WINDOW = 128
ROPE_THETA = 10000.0
D_CONV = 512
D_FF = 5632
PROJ_W = 4880
EPS = 1e-6
PAST_LEN = 16384

C_Z, C_XBC, C_Q, C_G, C_K, C_V, C_DT, PROJ_P = 0, 1024, 2560, 3072, 4608, 4736, 4864, 5120

VMEM_LIMIT = 56 * 1024 * 1024


def _cparams(sem):
    return pltpu.CompilerParams(dimension_semantics=sem, vmem_limit_bytes=VMEM_LIMIT)


def _silu(x):
    return x * jax.nn.sigmoid(x)


def _rms_mod(x, w, scale, shift):
    ms = jnp.mean(x * x, axis=-1, keepdims=True)
    return (x * lax.rsqrt(ms + EPS) * w) * (1.0 + scale) + shift


def _split3(x):
    h1 = x.astype(BF16)
    r1 = x - h1.astype(F32)
    h2 = r1.astype(BF16)
    r2 = r1 - h2.astype(F32)
    return h1, h2, r2.astype(BF16)


def _dot(a, b):
    return jnp.dot(a, b, preferred_element_type=F32)


def _dot_nt(a, b):
    return lax.dot_general(a, b, (((1,), (1,)), ((), ())), preferred_element_type=F32)


def _layer_spec(arr, l):
    shape = arr.shape[1:]
    return pl.BlockSpec((None,) + shape, lambda *_: (l,) + (0,) * len(shape))


def _mod_kernel(c_ref, w_ref, b_ref, o_ref):
    a = _silu(c_ref[...]).astype(BF16)
    o_ref[...] = _dot(a, w_ref[...].astype(BF16)) + b_ref[...]


def _modulation(c_all, w_ada, b_ada, tn=1024):
    depth, d, n = w_ada.shape
    r = c_all.shape[0]
    return pl.pallas_call(
        _mod_kernel,
        grid=(depth, n // tn),
        in_specs=[
            pl.BlockSpec((r, d), lambda l, j: (0, 0)),
            pl.BlockSpec((None, d, tn), lambda l, j: (l, 0, j)),
            pl.BlockSpec((None, 1, tn), lambda l, j: (l, 0, j)),
        ],
        out_specs=pl.BlockSpec((None, r, tn), lambda l, j: (l, 0, j)),
        out_shape=jax.ShapeDtypeStruct((depth, r, n), F32),
        compiler_params=_cparams(("arbitrary", "arbitrary")),
        name="modulation",
    )(c_all, w_ada, b_ada.reshape(depth, 1, n))


def _norm_kernel(x_ref, w_ref, sc_ref, sh_ref, o_ref):
    o_ref[...] = _rms_mod(x_ref[...], w_ref[...], sc_ref[...], sh_ref[...]).astype(BF16)


def _mod_spec(mod, l, j, tm, rpb, m_axis):
    r = mod.shape[2]

    def imap(*idx):
        return (l, (idx[m_axis] * tm) // rpb, 0, j)

    return pl.BlockSpec((None, None, r, D_MODEL), imap)


def _norm_mod(x, w_all, mod, l, j_scale, j_shift, tm, rpb):
    m = x.shape[0]
    return pl.pallas_call(
        _norm_kernel,
        grid=(m // tm,),
        in_specs=[
            pl.BlockSpec((tm, D_MODEL), lambda i: (i, 0)),
            _layer_spec(w_all, l),
            _mod_spec(mod, l, j_scale, tm, rpb, 0),
            _mod_spec(mod, l, j_shift, tm, rpb, 0),
        ],
        out_specs=pl.BlockSpec((tm, D_MODEL), lambda i: (i, 0)),
        out_shape=jax.ShapeDtypeStruct((m, D_MODEL), BF16),
        compiler_params=_cparams(("arbitrary",)),
        name="norm_mod",
    )(x, w_all, mod, mod)


IN_TN = 1024
IN_SRC = 256
IN_SUB = 256
IN_ROW_UNIT = 16


def _in_proj_table():
    segs = [(C_Z, C_Q, 0), (C_Q, C_G, 2576), (C_G, C_K, 3344), (C_K, C_DT, 3088), (C_DT, PROJ_P, 2560)]
    rows = []
    for st in range(PROJ_P // IN_SRC):
        c = st * IN_SRC
        lo, _, src = [g for g in segs if g[0] <= c < g[1]][0]
        rows.append((src + c - lo) // IN_ROW_UNIT)
    return jnp.array(rows, jnp.int32)


def _in_proj_kernel(tab_ref, h_ref, hs_ref, a0_ref, a1_ref, a2_ref, a3_ref, wo_ref,
                    o_ref, os_ref, wo16_ref, w16_ref, *, tm):
    del tab_ref
    j = pl.program_id(0)
    i = pl.program_id(1)
    nsub = tm // IN_SUB
    wo16_ref[...] = wo_ref[...].astype(BF16)

    @pl.when(i == 0)
    def _():
        for s, a_ref in enumerate((a0_ref, a1_ref, a2_ref, a3_ref)):
            w16_ref[:, IN_SRC * s:IN_SRC * (s + 1)] = a_ref[0].T.astype(BF16)
        os_ref[...] = _dot(hs_ref[...], w16_ref[...])

    def rows(r):
        return slice(r * IN_SUB, (r + 1) * IN_SUB)

    def tile(r):
        return _dot(h_ref[rows(r), :], w16_ref[...])

    @pl.when(j == 0)
    def _():
        for r in range(nsub):
            o_ref[rows(r), :] = _silu(tile(r))

    @pl.when(j > 0)
    def _():
        for r in range(nsub):
            o_ref[rows(r), :] = tile(r)


def _in_proj(h, hs, w_in_t, l, tab, w_out_all, tm):
    m, k = h.shape
    ms = hs.shape[0]
    nsrc = IN_TN // IN_SRC
    ni = m // tm
    nsteps = (PROJ_P // IN_TN) * ni
    nslab = 1 << (nsteps.bit_length() - 1)
    slab = D_MODEL // nslab
    a_spec = lambda s: pl.BlockSpec((pl.Element(1), pl.Element(IN_SRC), pl.Element(k)),
                                    lambda j, i, t: (l, t[nsrc * j + s] * IN_ROW_UNIT, 0))
    slab_idx = lambda j, i: jnp.minimum(j * ni + i, nslab - 1)
    return pl.pallas_call(
        functools.partial(_in_proj_kernel, tm=tm),
        grid_spec=pltpu.PrefetchScalarGridSpec(
            num_scalar_prefetch=1,
            grid=(PROJ_P // IN_TN, ni),
            in_specs=[
                pl.BlockSpec((tm, k), lambda j, i, t: (i, 0)),
                pl.BlockSpec((ms, k), lambda j, i, t: (0, 0)),
                a_spec(0), a_spec(1), a_spec(2), a_spec(3),
                pl.BlockSpec((None, slab, D_MODEL), lambda j, i, t: (l, slab_idx(j, i), 0)),
            ],
            out_specs=[
                pl.BlockSpec((tm, IN_TN), lambda j, i, t: (i, j)),
                pl.BlockSpec((ms, IN_TN), lambda j, i, t: (0, j)),
                pl.BlockSpec((slab, D_MODEL), lambda j, i, t: (slab_idx(j, i), 0)),
            ],
            scratch_shapes=[pltpu.VMEM((k, IN_TN), BF16)],
        ),
        out_shape=[jax.ShapeDtypeStruct((m, PROJ_P), F32), jax.ShapeDtypeStruct((ms, PROJ_P), F32),
                   jax.ShapeDtypeStruct((D_MODEL, D_MODEL), BF16)],
        compiler_params=_cparams(("arbitrary", "arbitrary")),
        name="in_proj",
    )(tab, h, hs, w_in_t, w_in_t, w_in_t, w_in_t, w_out_all)


def _ssd_kernel(zx_ref, g_ref, kvdt_ref, cw_ref, cb_ref, dtb_ref, alog_ref, dsk_ref, nw_ref, scw_ref,
                y_ref, yc_ref, st_out_ref, cst_ref, ust_ref,
                xpad_ref, upad_ref, st_ref):
    c = pl.program_id(1)
    nc = pl.num_programs(1)
    L = CHUNK

    @pl.when(c == 0)
    def _():
        xpad_ref[0:8, :] = jnp.zeros((8, D_XBC), F32)
        upad_ref[0:8, :] = jnp.zeros((8, D_CONV), F32)
        st_ref[...] = jnp.zeros_like(st_ref)

    zg = zx_ref[:, :D_SSD]
    raw = zx_ref[:, D_SSD:]

    xpad_ref[8:8 + L, :] = raw
    cw = cw_ref[...]
    conv = (cw[3:4] * raw + cw[2:3] * xpad_ref[7:7 + L, :] + cw[1:2] * xpad_ref[6:6 + L, :]
            + cw[0:1] * xpad_ref[5:5 + L, :])
    xpad_ref[0:8, :] = raw[L - 8:, :]
    cst_ref[...] = raw[L - 8:, :]
    xbc = _silu(conv + cb_ref[...])
    xs = xbc[:, :D_SSD]

    lane = lax.broadcasted_iota(jnp.int32, (L, 128), 1)
    row = lax.broadcasted_iota(jnp.int32, (L, 128), 0)
    lo = (lane & 64) == 0
    causal = row >= lane

    head_ok = lane[0:1] < SSM_HEADS
    a_row = jnp.where(head_ok, -jnp.exp(alog_ref[...]), 0.0)
    dtr = kvdt_ref[:, 256:384] + dtb_ref[...]
    dt = jnp.maximum(dtr, 0.0) + jnp.log1p(jnp.exp(-jnp.abs(dtr)))
    da = dt * a_row

    tril = jnp.where(causal, 1.0, 0.0).astype(BF16)
    d1, d2, d3 = _split3(da)
    cs = _dot(tril, d1) + _dot(tril, d2) + _dot(tril, d3)
    cs_t = cs.T
    ecs = jnp.exp(cs)
    te = jnp.exp(cs[L - 1:L, :] - cs)
    ecl = ecs[L - 1:L, :]

    def pair(arr, j):
        rows = arr.shape[0]
        return jnp.where(lo[:rows], arr[:, 2 * j:2 * j + 1], arr[:, 2 * j + 1:2 * j + 2])

    cbs, yoffs, updates = [], [], []
    xdts = []
    for j in range(8):
        xdts.append(xs[:, 128 * j:128 * (j + 1)] * pair(dt, j))
    for g in range(2):
        bm = xbc[:, D_SSD + 128 * g:D_SSD + 128 * (g + 1)]
        cm = xbc[:, D_SSD + 256 + 128 * g:D_SSD + 256 + 128 * (g + 1)]
        cm16 = cm.astype(BF16)
        cbs.append(_dot_nt(cm16, bm.astype(BF16)))
        yoffs.append(_dot(cm16, st_ref[:, 512 * g:512 * (g + 1)].astype(BF16)))
        xe = jnp.concatenate([xdts[4 * g + i] * pair(te, 4 * g + i) for i in range(4)], axis=1)
        s_new = _dot(bm.T.astype(BF16), xe.astype(BF16))
        cd = jnp.concatenate([pair(ecl, 4 * g + i) for i in range(4)], axis=1)
        updates.append((cd, s_new))
    for g in range(2):
        cd, s_new = updates[g]
        st_ref[:, 512 * g:512 * (g + 1)] = cd * st_ref[:, 512 * g:512 * (g + 1)] + s_new

    ys = []
    for j in range(8):
        g = j // 4
        ca, cb_ = cs[:, 2 * j:2 * j + 1], cs[:, 2 * j + 1:2 * j + 2]
        ra, rb = cs_t[2 * j:2 * j + 1, :], cs_t[2 * j + 1:2 * j + 2, :]
        dec_a = jnp.where(causal, jnp.exp(jnp.minimum(ca - ra, 0.0)), 0.0)
        dec_b = jnp.where(causal, jnp.exp(jnp.minimum(cb_ - rb, 0.0)), 0.0)
        mm = jnp.concatenate([cbs[g] * dec_a, cbs[g] * dec_b], axis=1).astype(BF16)
        xp = xdts[j]
        rhs = jnp.concatenate([jnp.where(lo, xp, 0.0), jnp.where(lo, 0.0, xp)], axis=0).astype(BF16)
        yd = _dot(mm, rhs)
        yo = yoffs[g][:, 128 * (j % 4):128 * (j % 4 + 1)] * pair(ecs, j)
        ys.append(yd + yo + dsk_ref[:, 128 * j:128 * (j + 1)] * xs[:, 128 * j:128 * (j + 1)])
    y = jnp.concatenate(ys, axis=1) * zg
    nw = nw_ref[...]
    outs = []
    for g in range(2):
        yg = y[:, 512 * g:512 * (g + 1)]
        ms = jnp.mean(yg * yg, axis=-1, keepdims=True)
        outs.append(yg * lax.rsqrt(ms + EPS) * nw[:, 512 * g:512 * (g + 1)])
    y_ref[...] = jnp.concatenate(outs, axis=1).astype(BF16)

    @pl.when(c == nc - 1)
    def _():
        st_out_ref[...] = st_ref[...].T

    gg = g_ref[...]
    u = gg[:, D_CONV:2 * D_CONV] * gg[:, 2 * D_CONV:]
    upad_ref[8:8 + L, :] = u
    sw = scw_ref[...]
    sconv = sw[2:3] * u + sw[1:2] * upad_ref[7:7 + L, :] + sw[0:1] * upad_ref[6:6 + L, :]
    upad_ref[0:8, :] = u[L - 8:, :]
    ust_ref[...] = u[L - 8:, :]
    yc_ref[...] = (gg[:, :D_CONV] * sconv).astype(BF16)


def _ssd_prompt(proj, nb, seq, l, cw, cb, dtb, alog, dsk, nw, scw):
    nc = seq // CHUNK
    m = nb * seq
    row = lambda b, c: (b * nc + c)
    return pl.pallas_call(
        _ssd_kernel,
        grid=(nb, nc),
        in_specs=[
            pl.BlockSpec((CHUNK, 2560), lambda b, c: (row(b, c), 0)),
            pl.BlockSpec((CHUNK, 1536), lambda b, c: (row(b, c), 2)),
            pl.BlockSpec((CHUNK, 512), lambda b, c: (row(b, c), 9)),
            _layer_spec(cw, l), _layer_spec(cb, l), _layer_spec(dtb, l), _layer_spec(alog, l),
            _layer_spec(dsk, l), _layer_spec(nw, l), _layer_spec(scw, l),
        ],
        out_specs=[
            pl.BlockSpec((CHUNK, D_SSD), lambda b, c: (row(b, c), 0)),
            pl.BlockSpec((CHUNK, D_CONV), lambda b, c: (row(b, c), 0)),
            pl.BlockSpec((None, D_SSD, D_STATE), lambda b, c: (b, 0, 0)),
            pl.BlockSpec((None, 8, D_XBC), lambda b, c: (b, 0, 0)),
            pl.BlockSpec((None, 8, D_CONV), lambda b, c: (b, 0, 0)),
        ],
        out_shape=[
            jax.ShapeDtypeStruct((m, D_SSD), BF16),
            jax.ShapeDtypeStruct((m, D_CONV), BF16),
            jax.ShapeDtypeStruct((nb, D_SSD, D_STATE), F32),
            jax.ShapeDtypeStruct((nb, 8, D_XBC), F32),
            jax.ShapeDtypeStruct((nb, 8, D_CONV), F32),
        ],
        scratch_shapes=[
            pltpu.VMEM((8 + CHUNK, D_XBC), F32),
            pltpu.VMEM((8 + CHUNK, D_CONV), F32),
            pltpu.VMEM((D_STATE, D_SSD), F32),
        ],
        compiler_params=_cparams(("arbitrary", "arbitrary")),
        name="ssd_prompt",
    )(proj, proj, proj, cw, cb, dtb, alog, dsk, nw, scw)


def _head_norm(x, w, bd):
    sq = x * x
    hi = sq.astype(BF16)
    lo = (sq - hi.astype(F32)).astype(BF16)
    ms = _dot(hi, bd) + _dot(lo, bd)
    return x * lax.rsqrt(ms + EPS) * w


def _rope(x, cos, sin_signed, first_half):
    partner = jnp.where(first_half, pltpu.roll(x, 96, axis=1), pltpu.roll(x, 32, axis=1))
    return x * cos + partner * sin_signed


def _seg_mean_matrix():
    r = lax.broadcasted_iota(jnp.int32, (128, 128), 0)
    c = lax.broadcasted_iota(jnp.int32, (128, 128), 1)
    return jnp.where((r & 64) == (c & 64), 1.0 / HEAD_DIM, 0.0).astype(BF16)


def _attn_kernel(sink_ref, q_ref, kv_ref, cos_ref, sin_ref, qw_ref, kw_ref,
                 o_ref, kr_ref, kprev_ref, vprev_ref, *, layer):
    c = pl.program_id(1)
    L = CHUNK

    @pl.when(c == 0)
    def _():
        kprev_ref[...] = jnp.zeros_like(kprev_ref)
        vprev_ref[...] = jnp.zeros_like(vprev_ref)

    bd = _seg_mean_matrix()
    lane = lax.broadcasted_iota(jnp.int32, (L, 128), 1)
    lo = (lane & 64) == 0
    first_half = (lane & 32) == 0
    cos = cos_ref[...]
    sin = sin_ref[...]

    k_new = _rope(_head_norm(kv_ref[:, 0:128], kw_ref[...], bd), cos, sin, first_half)
    v_new = kv_ref[:, 128:256]
    kr_ref[...] = k_new
    kband = jnp.concatenate([kprev_ref[...], k_new], axis=0)
    vband = jnp.concatenate([vprev_ref[...], v_new], axis=0)
    kprev_ref[...] = k_new
    vprev_ref[...] = v_new
    lo2 = jnp.concatenate([lo, lo], axis=0)
    kswap = pltpu.roll(kband, 64, axis=1)
    vswap = pltpu.roll(vband, 64, axis=1)
    k2 = [jnp.where(lo2, kband, kswap).astype(BF16), jnp.where(lo2, kswap, kband).astype(BF16)]
    v2 = [jnp.where(lo2, vband, vswap).astype(BF16), jnp.where(lo2, vswap, vband).astype(BF16)]

    qi = lax.broadcasted_iota(jnp.int32, (L, 2 * L), 0)
    kj = lax.broadcasted_iota(jnp.int32, (L, 2 * L), 1)
    diff = L + qi - kj
    valid = (diff >= 0) & (diff <= WINDOW) & ((kj >= L) | (c > 0))

    qt = []
    for t in range(4):
        qn = _head_norm(q_ref[:, 128 * t:128 * (t + 1)], qw_ref[...], bd)
        qt.append(_rope(qn, cos, sin, first_half) * (HEAD_DIM ** -0.5))

    otiles = []
    for g in range(KV_HEADS):
        lhs = []
        for t in (2 * g, 2 * g + 1):
            lhs.append(jnp.where(lo, qt[t], 0.0))
            lhs.append(jnp.where(lo, 0.0, qt[t]))
        s = _dot_nt(jnp.concatenate(lhs, axis=0).astype(BF16), k2[g])
        ps = []
        for i in range(4):
            sink = sink_ref[layer, 4 * g + i]
            sh = jnp.where(valid, s[L * i:L * (i + 1)], -1e30)
            mx = jnp.maximum(jnp.max(sh, axis=-1, keepdims=True), sink)
            p = jnp.exp(sh - mx)
            den = jnp.sum(p, axis=-1, keepdims=True) + jnp.exp(sink - mx)
            ps.append(p / den)
        og = _dot(jnp.concatenate(ps, axis=0).astype(BF16), v2[g])
        otiles.append(jnp.where(lo, og[0:L], og[L:2 * L]))
        otiles.append(jnp.where(lo, og[2 * L:3 * L], og[3 * L:4 * L]))
    o_ref[...] = jnp.concatenate(otiles, axis=1).astype(BF16)


def _attn_prompt(proj, nb, seq, l, sinks, cos_t, sin_t, qw, kw):
    nc = seq // CHUNK
    m = nb * seq
    row = lambda b, c: (b * nc + c)
    return pl.pallas_call(
        functools.partial(_attn_kernel, layer=l),
        grid=(nb, nc),
        in_specs=[
            pl.BlockSpec(memory_space=pltpu.SMEM),
            pl.BlockSpec((CHUNK, 512), lambda b, c: (row(b, c), 5)),
            pl.BlockSpec((CHUNK, 512), lambda b, c: (row(b, c), 9)),
            pl.BlockSpec((CHUNK, 128), lambda b, c: (c, 0)),
            pl.BlockSpec((CHUNK, 128), lambda b, c: (c, 0)),
            _layer_spec(qw, l), _layer_spec(kw, l),
        ],
        out_specs=[
            pl.BlockSpec((CHUNK, D_ATTN), lambda b, c: (row(b, c), 0)),
            pl.BlockSpec((CHUNK, 128), lambda b, c: (row(b, c), 0)),
        ],
        out_shape=[
            jax.ShapeDtypeStruct((m, D_ATTN), BF16),
            jax.ShapeDtypeStruct((m, 128), F32),
        ],
        scratch_shapes=[pltpu.VMEM((CHUNK, 128), F32), pltpu.VMEM((CHUNK, 128), F32)],
        compiler_params=_cparams(("arbitrary", "arbitrary")),
        name="attn_prompt",
    )(sinks, proj, proj, cos_t, sin_t, qw, kw)


def _outproj_rows(ys_ref, o_ref, yc_ref, x_ref, gate_ref, sc_ref, sh_ref, w_ref, nw_ref, x1_ref, h2_ref):
    acc = _dot(ys_ref[...], w_ref[0:D_SSD, :])
    acc += _dot(o_ref[...], w_ref[D_SSD:D_SSD + D_ATTN, :])
    acc += _dot(yc_ref[...], w_ref[D_SSD + D_ATTN:, :])
    x1 = x_ref[...] + gate_ref[...] * acc
    x1_ref[...] = x1
    h2_ref[...] = _rms_mod(x1, nw_ref[...], sc_ref[...], sh_ref[...]).astype(BF16)


def _outproj_kernel(*refs):
    p_in, s_in, (w_ref, nw_ref), p_out, s_out = refs[0:7], refs[7:14], refs[14:16], refs[16:18], refs[18:20]

    @pl.when(pl.program_id(0) == 0)
    def _():
        _outproj_rows(*s_in, w_ref, nw_ref, *s_out)

    _outproj_rows(*p_in, w_ref, nw_ref, *p_out)


def _row_specs(widths, tm):
    return [pl.BlockSpec((tm, w), lambda i: (i, 0)) for w in widths]


def _const_specs(widths, rows):
    return [pl.BlockSpec((rows, w), lambda i: (0, 0)) for w in widths]


def _outproj(mix_p, x_p, mod_p, mix_s, x_s, mod_s, w_all, l, nw2_all, tm, rpb):
    m, ms = x_p.shape[0], x_s.shape[0]
    widths = (D_SSD, D_ATTN, D_CONV, D_MODEL)
    mods = lambda mod, t, r: [_mod_spec(mod, l, 2, t, r, 0), _mod_spec(mod, l, 4, t, r, 0),
                              _mod_spec(mod, l, 3, t, r, 0)]
    return pl.pallas_call(
        _outproj_kernel,
        grid=(m // tm,),
        in_specs=(_row_specs(widths, tm) + mods(mod_p, tm, rpb)
                  + _const_specs(widths, ms) + mods(mod_s, 0, 1)
                  + [pl.BlockSpec((D_MODEL, D_MODEL), lambda i: (0, 0), pipeline_mode=pl.Buffered(1)),
                     _layer_spec(nw2_all, l)]),
        out_specs=_row_specs((D_MODEL, D_MODEL), tm) + _const_specs((D_MODEL, D_MODEL), ms),
        out_shape=[
            jax.ShapeDtypeStruct((m, D_MODEL), F32), jax.ShapeDtypeStruct((m, D_MODEL), BF16),
            jax.ShapeDtypeStruct((ms, D_MODEL), F32), jax.ShapeDtypeStruct((ms, D_MODEL), BF16),
        ],
        compiler_params=_cparams(("arbitrary",)),
        name="out_proj",
    )(*mix_p, x_p, mod_p, mod_p, mod_p, *mix_s, x_s, mod_s, mod_s, mod_s, w_all, nw2_all)


def _up_seq_kernel(h_ref, hs_ref, wg_ref, wu_ref, cw_ref, cb_ref, prev_ref, wd_ref,
                   a_ref, gst_ref, as_ref, gts_ref, wd16_ref, pad_ref, wg16_ref, wu16_ref, *, tm, rpb):
    i = pl.program_id(1)
    cw = cw_ref[...]
    cb = cb_ref[...]
    wd16_ref[...] = wd_ref[...].astype(BF16)

    @pl.when(i == 0)
    def _():
        wg16_ref[...] = wg_ref[...].astype(BF16)
        wu16_ref[...] = wu_ref[...].astype(BF16)
        hs = hs_ref[...]
        gt = _dot(hs, wg16_ref[...])
        u = _dot(hs, wu16_ref[...])
        conv = cw[2:3] * gt + cw[1:2] * prev_ref[1] + cw[0:1] * prev_ref[0]
        gts_ref[...] = gt
        as_ref[...] = (_silu(conv + cb) * u).astype(BF16)

    @pl.when((i * tm) % rpb == 0)
    def _():
        pad_ref[0:8, :] = jnp.zeros((8, pad_ref.shape[1]), F32)

    sub = 512
    for r in range(tm // sub):
        h = h_ref[r * sub:(r + 1) * sub, :]
        gt = _dot(h, wg16_ref[...])
        u = _dot(h, wu16_ref[...])
        base = 8 + r * sub
        pad_ref[base:base + sub, :] = gt
        conv = (cw[2:3] * gt + cw[1:2] * pad_ref[base - 1:base - 1 + sub, :]
                + cw[0:1] * pad_ref[base - 2:base - 2 + sub, :])
        a_ref[r * sub:(r + 1) * sub, :] = (_silu(conv + cb) * u).astype(BF16)
    tail = pad_ref[tm:tm + 8, :]
    pad_ref[0:8, :] = tail
    gst_ref[...] = tail


def _up_proj(h, hs, w_up_all, l, cw_all, cb_all, prev_all, w_down_all, nb, tm, tn):
    m = h.shape[0]
    ms = hs.shape[0]
    rpb = m // nb
    nt = D_FF // tn
    ni = m // tm
    slab = D_FF // (nt * ni)
    assert slab * nt * ni == D_FF and slab % 16 == 0
    return pl.pallas_call(
        functools.partial(_up_seq_kernel, tm=tm, rpb=rpb),
        grid=(nt, ni),
        in_specs=[
            pl.BlockSpec((tm, D_MODEL), lambda j, i: (i, 0)),
            pl.BlockSpec((ms, D_MODEL), lambda j, i: (0, 0)),
            pl.BlockSpec((None, D_MODEL, tn), lambda j, i: (l, 0, j)),
            pl.BlockSpec((None, D_MODEL, tn), lambda j, i: (l, 0, j + nt)),
            pl.BlockSpec((None, 3, tn), lambda j, i: (l, 0, j)),
            pl.BlockSpec((None, 1, tn), lambda j, i: (l, 0, j)),
            pl.BlockSpec((None, 2, ms, tn), lambda j, i: (l, 0, 0, j)),
            pl.BlockSpec((None, slab, D_MODEL), lambda j, i: (l, j * ni + i, 0)),
        ],
        out_specs=[
            pl.BlockSpec((tm, tn), lambda j, i: (i, j)),
            pl.BlockSpec((None, 8, tn), lambda j, i: ((i * tm) // rpb, 0, j)),
            pl.BlockSpec((ms, tn), lambda j, i: (0, j)),
            pl.BlockSpec((ms, tn), lambda j, i: (0, j)),
            pl.BlockSpec((slab, D_MODEL), lambda j, i: (j * ni + i, 0)),
        ],
        out_shape=[
            jax.ShapeDtypeStruct((m, D_FF), BF16),
            jax.ShapeDtypeStruct((nb, 8, D_FF), F32),
            jax.ShapeDtypeStruct((ms, D_FF), BF16),
            jax.ShapeDtypeStruct((ms, D_FF), F32),
            jax.ShapeDtypeStruct((D_FF, D_MODEL), BF16),
        ],
        scratch_shapes=[
            pltpu.VMEM((8 + tm, tn), F32),
            pltpu.VMEM((D_MODEL, tn), BF16),
            pltpu.VMEM((D_MODEL, tn), BF16),
        ],
        compiler_params=_cparams(("arbitrary", "arbitrary")),
        name="up_proj",
    )(h, hs, w_up_all, w_up_all, cw_all, cb_all, prev_all, w_down_all)


def _down_rows(a_ref, x_ref, gate_ref, mods, w_ref, nw_ref, x2_ref, hn_ref):
    x2 = x_ref[...] + gate_ref[...] * _dot(a_ref[...], w_ref[...])
    x2_ref[...] = x2
    if hn_ref is not None:
        hn_ref[...] = _rms_mod(x2, nw_ref[...], mods[0][...], mods[1][...]).astype(BF16)


def _down_kernel(*refs, with_norm):
    refs = list(refs)
    n_in = 5 if with_norm else 3
    p_in, s_in = refs[0:n_in], refs[n_in:2 * n_in]
    w_ref = refs[2 * n_in]
    nw_ref = refs[2 * n_in + 1] if with_norm else None
    outs = refs[2 * n_in + (2 if with_norm else 1):]
    p_out = (outs[0], outs[1]) if with_norm else (outs[0], None)
    s_out = (outs[2], outs[3]) if with_norm else (outs[1], None)

    @pl.when(pl.program_id(0) == 0)
    def _():
        _down_rows(s_in[0], s_in[1], s_in[2], s_in[3:], w_ref, nw_ref, *s_out)

    _down_rows(p_in[0], p_in[1], p_in[2], p_in[3:], w_ref, nw_ref, *p_out)


def _down(a_p, x_p, mod_p, a_s, x_s, mod_s, w_all, l, nw1_all, tm, rpb):
    m, ms = x_p.shape[0], x_s.shape[0]
    with_norm = l + 1 < nw1_all.shape[0]

    def mods(mod, t, r):
        sp = [_mod_spec(mod, l, 5, t, r, 0)]
        if with_norm:
            sp += [_mod_spec(mod, l + 1, 1, t, r, 0), _mod_spec(mod, l + 1, 0, t, r, 0)]
        return sp

    n_mod = 3 if with_norm else 1
    in_specs = (_row_specs((D_FF, D_MODEL), tm) + mods(mod_p, tm, rpb)
                + _const_specs((D_FF, D_MODEL), ms) + mods(mod_s, 0, 1)
                + [pl.BlockSpec((D_FF, D_MODEL), lambda i: (0, 0), pipeline_mode=pl.Buffered(1))])
    args = [a_p, x_p] + [mod_p] * n_mod + [a_s, x_s] + [mod_s] * n_mod + [w_all]
    out_w = (D_MODEL, D_MODEL) if with_norm else (D_MODEL,)
    out_dt = (F32, BF16) if with_norm else (F32,)
    if with_norm:
        in_specs.append(_layer_spec(nw1_all, l + 1))
        args.append(nw1_all)
    res = pl.pallas_call(
        functools.partial(_down_kernel, with_norm=with_norm),
        grid=(m // tm,),
        in_specs=in_specs,
        out_specs=_row_specs(out_w, tm) + _const_specs(out_w, ms),
        out_shape=([jax.ShapeDtypeStruct((m, D_MODEL), d) for d in out_dt]
                   + [jax.ShapeDtypeStruct((ms, D_MODEL), d) for d in out_dt]),
        compiler_params=_cparams(("arbitrary",)),
        name="down_proj",
    )(*args)
    if with_norm:
        return res[0], res[1], res[2], res[3]
    return res[0], None, res[1], None


def _expand_matrix():
    r = lax.broadcasted_iota(jnp.int32, (128, D_SSD), 0)
    c = lax.broadcasted_iota(jnp.int32, (128, D_SSD), 1)
    return jnp.where(r == (c >> 6), 1.0, 0.0).astype(BF16)


def _ssd_step_kernel(p_ref, st_ref, cprev_ref, sprev_ref, cw_ref, cb_ref, dtb_ref, alog_ref, dsk_ref, nw_ref,
                     scw_ref, y_ref, yc_ref, u_ref, stn_ref):
    nbb = p_ref.shape[0]
    z = p_ref[:, C_Z:C_Z + D_SSD]
    raw = p_ref[:, C_XBC:C_XBC + D_XBC]
    cw = cw_ref[...]
    conv = cw[3:4] * raw + cw[2:3] * cprev_ref[2] + cw[1:2] * cprev_ref[1] + cw[0:1] * cprev_ref[0]
    xbc = _silu(conv + cb_ref[...])
    xs = xbc[:, :D_SSD]
    bm = xbc[:, D_SSD:D_SSD + 256]
    cm = xbc[:, D_SSD + 256:]

    lane = lax.broadcasted_iota(jnp.int32, (1, 128), 1)
    a_row = jnp.where(lane < SSM_HEADS, -jnp.exp(alog_ref[...]), 0.0)
    dtr = p_ref[:, C_DT:C_DT + 128] + dtb_ref[...]
    dt = jnp.maximum(dtr, 0.0) + jnp.log1p(jnp.exp(-jnp.abs(dtr)))
    dec = jnp.exp(dt * a_row)
    em = _expand_matrix()

    def expand(v):
        v1, v2, v3 = _split3(v)
        return _dot(v1, em) + _dot(v2, em) + _dot(v3, em)

    xdt_t = (xs * expand(dt)).T
    dec_t = expand(dec).T
    rowid = lax.broadcasted_iota(jnp.int32, (nbb, 512), 0)
    ysum = [jnp.zeros((nbb, 512), F32), jnp.zeros((nbb, 512), F32)]
    for i in range(nbb):
        for g in range(2):
            rs = slice(512 * g, 512 * (g + 1))
            hn = st_ref[i, rs, :] * dec_t[rs, i:i + 1] + xdt_t[rs, i:i + 1] * bm[i:i + 1, 128 * g:128 * (g + 1)]
            stn_ref[i, rs, :] = hn
            r = _dot_nt(cm[:, 128 * g:128 * (g + 1)].astype(BF16), hn.astype(BF16))
            ysum[g] = ysum[g] + jnp.where(rowid == i, r, 0.0)
    y = jnp.concatenate(ysum, axis=1) + dsk_ref[...] * xs
    y = y * _silu(z)
    nw = nw_ref[...]
    outs = []
    for g in range(2):
        yg = y[:, 512 * g:512 * (g + 1)]
        ms = jnp.mean(yg * yg, axis=-1, keepdims=True)
        outs.append(yg * lax.rsqrt(ms + EPS) * nw[:, 512 * g:512 * (g + 1)])
    y_ref[...] = jnp.concatenate(outs, axis=1).astype(BF16)

    u = p_ref[:, C_G + D_CONV:C_G + 2 * D_CONV] * p_ref[:, C_G + 2 * D_CONV:C_G + 3 * D_CONV]
    sw = scw_ref[...]
    sconv = sw[2:3] * u + sw[1:2] * sprev_ref[1] + sw[0:1] * sprev_ref[0]
    u_ref[...] = u
    yc_ref[...] = (p_ref[:, C_G:C_G + D_CONV] * sconv).astype(BF16)


def _ssd_step(proj, state_all, cprev_all, sprev_all, l, cw, cb, dtb, alog, dsk, nw, scw, nbb=8):
    nb = proj.shape[0]
    return pl.pallas_call(
        _ssd_step_kernel,
        grid=(nb // nbb,),
        in_specs=[
            pl.BlockSpec((nbb, PROJ_P), lambda i: (i, 0)),
            pl.BlockSpec((None, nbb, D_SSD, D_STATE), lambda i: (l, i, 0, 0)),
            pl.BlockSpec((None, 3, nbb, D_XBC), lambda i: (l, 0, i, 0)),
            pl.BlockSpec((None, 2, nbb, D_CONV), lambda i: (l, 0, i, 0)),
            _layer_spec(cw, l), _layer_spec(cb, l), _layer_spec(dtb, l), _layer_spec(alog, l),
            _layer_spec(dsk, l), _layer_spec(nw, l), _layer_spec(scw, l),
        ],
        out_specs=[
            pl.BlockSpec((nbb, D_SSD), lambda i: (i, 0)),
            pl.BlockSpec((nbb, D_CONV), lambda i: (i, 0)),
            pl.BlockSpec((nbb, D_CONV), lambda i: (i, 0)),
            pl.BlockSpec((nbb, D_SSD, D_STATE), lambda i: (i, 0, 0)),
        ],
        out_shape=[
            jax.ShapeDtypeStruct((nb, D_SSD), BF16),
            jax.ShapeDtypeStruct((nb, D_CONV), BF16),
            jax.ShapeDtypeStruct((nb, D_CONV), F32),
            jax.ShapeDtypeStruct((nb, D_SSD, D_STATE), F32),
        ],
        compiler_params=_cparams(("arbitrary",)),
        name="ssd_step",
    )(proj, state_all, cprev_all, sprev_all, cw, cb, dtb, alog, dsk, nw, scw)


def _qk_step_kernel(p_ref, cos_ref, sin_ref, qw_ref, kw_ref, q_ref, k_ref):
    bd = _seg_mean_matrix()
    lane = lax.broadcasted_iota(jnp.int32, (p_ref.shape[0], 128), 1)
    first_half = (lane & 32) == 0
    cos = cos_ref[...]
    sin = sin_ref[...]
    for t in range(4):
        qn = _head_norm(p_ref[:, C_Q + 128 * t:C_Q + 128 * (t + 1)], qw_ref[...], bd)
        q_ref[:, 128 * t:128 * (t + 1)] = _rope(qn, cos, sin, first_half) * (HEAD_DIM ** -0.5)
    k_ref[...] = _rope(_head_norm(p_ref[:, C_K:C_K + 128], kw_ref[...], bd), cos, sin, first_half)


def _qk_step(proj, cos_r, sin_r, l, qw, kw):
    nb = proj.shape[0]
    full = lambda a: pl.BlockSpec(a.shape, lambda i: (0,) * a.ndim)
    return pl.pallas_call(
        _qk_step_kernel,
        grid=(1,),
        in_specs=[full(proj), full(cos_r), full(sin_r), _layer_spec(qw, l), _layer_spec(kw, l)],
        out_specs=[pl.BlockSpec((nb, D_ATTN), lambda i: (0, 0)), pl.BlockSpec((nb, 128), lambda i: (0, 0))],
        out_shape=[jax.ShapeDtypeStruct((nb, D_ATTN), F32), jax.ShapeDtypeStruct((nb, 128), F32)],
        name="qk_step",
    )(proj, cos_r, sin_r, qw, kw)


def _attn_step_kernel(q_ref, kc_ref, vc_ref, knew_ref, vnew_ref, sink_ref, o_ref):
    q = q_ref[...]
    s = lax.dot_general(q.astype(BF16), kc_ref[...].astype(BF16), (((2,), (2,)), ((0,), (0,))),
                        preferred_element_type=F32)
    s_new = jnp.sum(q * knew_ref[...], axis=-1, keepdims=True)
    sink = sink_ref[...]
    mx = jnp.maximum(jnp.maximum(jnp.max(s, axis=-1, keepdims=True), s_new), sink)
    p = jnp.exp(s - mx)
    p_new = jnp.exp(s_new - mx)
    den = jnp.sum(p, axis=-1, keepdims=True) + p_new + jnp.exp(sink - mx)
    o = lax.dot_general((p / den).astype(BF16), vc_ref[...].astype(BF16), (((2,), (1,)), ((0,), (0,))),
                        preferred_element_type=F32)
    o_ref[...] = o + (p_new / den) * vnew_ref[...]


def _attn_step(q3, kc_all, vc_all, l, knew, vnew, sink_all):
    nb = q3.shape[0]
    full = lambda a: pl.BlockSpec(a.shape, lambda i: (0,) * a.ndim)
    return pl.pallas_call(
        _attn_step_kernel,
        grid=(1,),
        in_specs=[full(q3), _layer_spec(kc_all, l), _layer_spec(vc_all, l), full(knew), full(vnew),
                  _layer_spec(sink_all, l)],
        out_specs=pl.BlockSpec((nb, ATT_HEADS, 128), lambda i: (0, 0, 0)),
        out_shape=jax.ShapeDtypeStruct((nb, ATT_HEADS, 128), F32),
        name="attn_step",
    )(q3, kc_all, vc_all, knew, vnew, sink_all)


def _rope_tables(pos):
    half = HEAD_DIM // 2
    inv = jnp.power(ROPE_THETA, -jnp.arange(half, dtype=F32) / half)
    ang = pos[:, None] * inv[None, :]
    cos = jnp.cos(ang)
    sin = jnp.sin(ang)
    cos_t = jnp.concatenate([cos, cos, cos, cos], axis=-1)
    sin_t = jnp.concatenate([-sin, sin, -sin, sin], axis=-1)
    return cos_t, sin_t


def kernel(x_prompt, x_sample, state_ssm, state_ssm_conv, cache_win_k, cache_win_v, state_short_conv,
           state_ffn_conv, c_prompt, c_sample, norm1_w, norm2_w, w_ada, b_ada, w_in, ssm_conv_w, ssm_conv_b,
           dt_bias, a_log, d_skip, ssm_norm_w, q_norm_w, k_norm_w, sinks, sconv_w, w_out, w_up, ffn_conv_w,
           ffn_conv_b, w_down):
    nbp, seq, _ = x_prompt.shape
    nbs = x_sample.shape[0]
    depth = w_in.shape[0]
    mp = nbp * seq
    nbuf = cache_win_k.shape[2]

    in_tab = _in_proj_table()
    w_in_t = jnp.swapaxes(w_in, 1, 2)

    n1 = norm1_w.reshape(depth, 1, D_MODEL)
    n2 = norm2_w.reshape(depth, 1, D_MODEL)
    cb = ssm_conv_b.reshape(depth, 1, D_XBC)
    dtb = jnp.pad(dt_bias, ((0, 0), (0, 128 - SSM_HEADS))).reshape(depth, 1, 128)
    alog = jnp.pad(a_log, ((0, 0), (0, 128 - SSM_HEADS))).reshape(depth, 1, 128)
    dsk = jnp.repeat(d_skip, SSM_HEAD_DIM, axis=1).reshape(depth, 1, D_SSD)
    nw = ssm_norm_w.reshape(depth, 1, D_SSD)
    qw = jnp.tile(q_norm_w, (1, 2)).reshape(depth, 1, 128)
    kw = jnp.tile(k_norm_w, (1, 2)).reshape(depth, 1, 128)
    fcb = ffn_conv_b.reshape(depth, 1, D_FF)
    sink3 = sinks.reshape(depth, ATT_HEADS, 1)

    st_all = state_ssm.reshape(depth, nbs, D_SSD, D_STATE)
    cprev = jnp.transpose(state_ssm_conv, (0, 2, 1, 3))
    sprev = jnp.transpose(state_short_conv, (0, 2, 1, 3))
    fprev = jnp.transpose(state_ffn_conv, (0, 2, 1, 3))
    ck = cache_win_k.reshape(depth, nbs, nbuf, 128)
    cv = cache_win_v.reshape(depth, nbs, nbuf, 128)

    c_all = jnp.concatenate([c_prompt, jnp.zeros((8 - nbp, D_MODEL), F32), c_sample], axis=0)
    mod_all = _modulation(c_all, w_ada, b_ada)
    mod_p = mod_all[:, :nbp].reshape(depth, nbp, 1, 6 * D_MODEL)
    mod_s = mod_all[:, 8:].reshape(depth, 1, nbs, 6 * D_MODEL)

    cos_p, sin_p = _rope_tables(jnp.arange(seq, dtype=F32))
    cos_s, sin_s = _rope_tables(jnp.full((1,), float(PAST_LEN), F32))

    head_g = jnp.arange(ATT_HEADS) // (ATT_HEADS // KV_HEADS)
    q_place = (head_g[:, None] == jnp.arange(KV_HEADS)[None, :]).astype(F32)[None, :, :, None]

    xp = x_prompt.reshape(mp, D_MODEL)
    xs = x_sample.reshape(nbs, D_MODEL)
    TM = 512
    hp = _norm_mod(xp, n1, mod_p, 0, 1, 0, TM, seq)
    hs = _norm_mod(xs, n1, mod_s, 0, 1, 0, nbs, nbs)

    outs_p = [[] for _ in range(6)]
    outs_s = [[] for _ in range(6)]
    for l in range(depth):
        proj, projs, w_out_l = _in_proj(hp, hs, w_in_t, l, in_tab, w_out, min(1024, seq))

        y_ssm, y_conv, st_t, cst, ust = _ssd_prompt(proj, nbp, seq, l, ssm_conv_w, cb, dtb, alog, dsk, nw,
                                                    sconv_w)
        o, k_rot = _attn_prompt(proj, nbp, seq, l, sinks, cos_p, sin_p, qw, kw)

        y_s, yc_s, u_s, st_new = _ssd_step(projs, st_all, cprev, sprev, l, ssm_conv_w, cb, dtb, alog, dsk, nw,
                                           sconv_w)
        q_r, k_r = _qk_step(projs, cos_s, sin_s, l, qw, kw)
        v_r = projs[:, C_V:C_V + 128]
        q3 = (q_r.reshape(nbs, ATT_HEADS, 1, HEAD_DIM) * q_place).reshape(nbs, ATT_HEADS, 128)
        o3 = _attn_step(q3, ck, cv, l, k_r[:, None, :], v_r[:, None, :], sink3)
        o4 = o3.reshape(nbs, ATT_HEADS, KV_HEADS, HEAD_DIM)
        o_s = jnp.concatenate([o4[:, :4, 0], o4[:, 4:, 1]], axis=1).reshape(nbs, D_ATTN).astype(BF16)

        xp, h2, xs, h2s = _outproj((y_ssm, o, y_conv), xp, mod_p, (y_s, o_s, yc_s), xs, mod_s, w_out_l, l, n2,
                                   TM, seq)
        act, gst, act_s, gt_s, w_down_l = _up_proj(h2, h2s, w_up, l, ffn_conv_w, fcb, fprev, w_down, nbp,
                                                   min(2048, seq), 512)
        xp, hp, xs, hs = _down(act, xp, mod_p, act_s, xs, mod_s, w_down_l, l, n1, 256, seq)

        proj3 = proj.reshape(nbp, seq, PROJ_P)
        outs_p[0].append(st_t.reshape(nbp, SSM_HEADS, SSM_HEAD_DIM, D_STATE))
        outs_p[1].append(cst[:, 8 - (SSM_CONV - 1):])
        outs_p[2].append(k_rot.reshape(nbp, seq, KV_HEADS, HEAD_DIM)[:, seq - WINDOW:])
        outs_p[3].append(proj3[:, seq - WINDOW:, C_V:C_V + 128].reshape(nbp, WINDOW, KV_HEADS, HEAD_DIM))
        outs_p[4].append(ust[:, 6:])
        outs_p[5].append(gst[:, 6:])

        outs_s[0].append(st_new.reshape(nbs, SSM_HEADS, SSM_HEAD_DIM, D_STATE))
        outs_s[1].append(projs[:, C_XBC:C_XBC + D_XBC])
        outs_s[2].append(k_r)
        outs_s[3].append(v_r)
        outs_s[4].append(u_s)
        outs_s[5].append(gt_s)

    res_p = [jnp.stack(a, axis=0) for a in outs_p]
    new_rows = [jnp.stack(a, axis=0) for a in outs_s]
    s_ssm = new_rows[0]
    s_ssm_conv = jnp.concatenate([state_ssm_conv[:, :, 1:], new_rows[1][:, :, None]], axis=2)
    s_win_k = jnp.concatenate([ck[:, :, 1:], new_rows[2][:, :, None]], axis=2).reshape(cache_win_k.shape)
    s_win_v = jnp.concatenate([cv[:, :, 1:], new_rows[3][:, :, None]], axis=2).reshape(cache_win_v.shape)
    s_short = jnp.concatenate([state_short_conv[:, :, 1:], new_rows[4][:, :, None]], axis=2)
    s_ffn = jnp.concatenate([state_ffn_conv[:, :, 1:], new_rows[5][:, :, None]], axis=2)
    return (xp.reshape(nbp, seq, D_MODEL), xs.reshape(nbs, 1, D_MODEL), *res_p,
            s_ssm, s_ssm_conv, s_win_k, s_win_v, s_short, s_ffn)
```

```python
import functools

import jax
import jax.numpy as jnp
from jax import lax
from jax.experimental import pallas as pl
from jax.experimental.pallas import tpu as pltpu

F32 = jnp.float32
BF16 = jnp.bfloat16

D_MODEL = 2048
D_SSD = 1024
SSM_HEADS = 16
SSM_HEAD_DIM = 64
D_STATE = 128
SSM_CONV = 4
CHUNK = 128
D_XBC = 1536
HEAD_DIM = 64
D_ATTN = 512
ATT_HEADS = 8
KV_HEADS = 2
WINDOW = 128
ROPE_THETA = 10000.0
D_CONV = 512
D_FF = 5632
PROJ_W = 4880
EPS = 1e-6
PAST_LEN = 16384

C_Z, C_XBC, C_Q, C_G, C_K, C_V, C_DT, PROJ_P = 0, 1024, 2560, 3072, 4608, 4736, 4864, 5120

VMEM_LIMIT = 56 * 1024 * 1024


def _cparams(sem):
    return pltpu.CompilerParams(dimension_semantics=sem, vmem_limit_bytes=VMEM_LIMIT)


def _silu(x):
    return x * jax.nn.sigmoid(x)


def _rms_mod(x, w, scale, shift):
    ms = jnp.mean(x * x, axis=-1, keepdims=True)
    return (x * lax.rsqrt(ms + EPS) * w) * (1.0 + scale) + shift


def _split3(x):
    h1 = x.astype(BF16)
    r1 = x - h1.astype(F32)
    h2 = r1.astype(BF16)
    r2 = r1 - h2.astype(F32)
    return h1, h2, r2.astype(BF16)


def _dot(a, b):
    return jnp.dot(a, b, preferred_element_type=F32)


def _dot_nt(a, b):
    return lax.dot_general(a, b, (((1,), (1,)), ((), ())), preferred_element_type=F32)


def _layer_spec(arr, l):
    shape = arr.shape[1:]
    return pl.BlockSpec((None,) + shape, lambda *_: (l,) + (0,) * len(shape))


def _mod_kernel(c_ref, w_ref, b_ref, o_ref):
    a = _silu(c_ref[...]).astype(BF16)
    o_ref[...] = _dot(a, w_ref[...].astype(BF16)) + b_ref[...]


def _modulation(c_all, w_ada, b_ada, tn=1024):
    depth, d, n = w_ada.shape
    r = c_all.shape[0]
    return pl.pallas_call(
        _mod_kernel,
        grid=(depth, n // tn),
        in_specs=[
            pl.BlockSpec((r, d), lambda l, j: (0, 0)),
            pl.BlockSpec((None, d, tn), lambda l, j: (l, 0, j)),
            pl.BlockSpec((None, 1, tn), lambda l, j: (l, 0, j)),
        ],
        out_specs=pl.BlockSpec((None, r, tn), lambda l, j: (l, 0, j)),
        out_shape=jax.ShapeDtypeStruct((depth, r, n), F32),
        compiler_params=_cparams(("arbitrary", "arbitrary")),
        name="modulation",
    )(c_all, w_ada, b_ada.reshape(depth, 1, n))


def _norm_kernel(x_ref, w_ref, sc_ref, sh_ref, o_ref):
    o_ref[...] = _rms_mod(x_ref[...], w_ref[...], sc_ref[...], sh_ref[...]).astype(BF16)


def _mod_spec(mod, l, j, tm, rpb, m_axis):
    r = mod.shape[2]

    def imap(*idx):
        return (l, (idx[m_axis] * tm) // rpb, 0, j)

    return pl.BlockSpec((None, None, r, D_MODEL), imap)


def _norm_mod(x, w_all, mod, l, j_scale, j_shift, tm, rpb):
    m = x.shape[0]
    return pl.pallas_call(
        _norm_kernel,
        grid=(m // tm,),
        in_specs=[
            pl.BlockSpec((tm, D_MODEL), lambda i: (i, 0)),
            _layer_spec(w_all, l),
            _mod_spec(mod, l, j_scale, tm, rpb, 0),
            _mod_spec(mod, l, j_shift, tm, rpb, 0),
        ],
        out_specs=pl.BlockSpec((tm, D_MODEL), lambda i: (i, 0)),
        out_shape=jax.ShapeDtypeStruct((m, D_MODEL), BF16),
        compiler_params=_cparams(("arbitrary",)),
        name="norm_mod",
    )(x, w_all, mod, mod)


MIX_CHUNKS = 2
IN_TN = 1024
IN_SRC = 256
IN_SUB = 256
IN_ROW_UNIT = 16


def _in_proj_table():
    segs = [(C_Z, C_Q, 0), (C_Q, C_G, 2576), (C_G, C_K, 3344), (C_K, C_DT, 3088), (C_DT, PROJ_P, 2560)]
    rows = []
    for st in range(PROJ_P // IN_SRC):
        c = st * IN_SRC
        lo, _, src = [g for g in segs if g[0] <= c < g[1]][0]
        rows.append((src + c - lo) // IN_ROW_UNIT)
    return jnp.array(rows, jnp.int32)


def _in_proj_kernel(tab_ref, h_ref, hs_ref, a0_ref, a1_ref, a2_ref, a3_ref, wo_ref,
                    o_ref, os_ref, wo16_ref, w16_ref, *, tm):
    del tab_ref
    j = pl.program_id(0)
    i = pl.program_id(1)
    nsub = tm // IN_SUB
    wo16_ref[...] = wo_ref[...].astype(BF16)

    @pl.when(i == 0)
    def _():
        for s, a_ref in enumerate((a0_ref, a1_ref, a2_ref, a3_ref)):
            w16_ref[:, IN_SRC * s:IN_SRC * (s + 1)] = a_ref[0].T.astype(BF16)
        os_ref[...] = _dot(hs_ref[...], w16_ref[...])

    def rows(r):
        return slice(r * IN_SUB, (r + 1) * IN_SUB)

    def tile(r):
        return _dot(h_ref[rows(r), :], w16_ref[...])

    @pl.when(j == 0)
    def _():
        for r in range(nsub):
            o_ref[rows(r), :] = _silu(tile(r))

    @pl.when(j > 0)
    def _():
        for r in range(nsub):
            o_ref[rows(r), :] = tile(r)


def _in_proj(h, hs, w_in_t, l, tab, w_out_all, tm):
    m, k = h.shape
    ms = hs.shape[0]
    nsrc = IN_TN // IN_SRC
    ni = m // tm
    nsteps = (PROJ_P // IN_TN) * ni
    nslab = 1 << (nsteps.bit_length() - 1)
    slab = D_MODEL // nslab
    a_spec = lambda s: pl.BlockSpec((pl.Element(1), pl.Element(IN_SRC), pl.Element(k)),
                                    lambda j, i, t: (l, t[nsrc * j + s] * IN_ROW_UNIT, 0))
    slab_idx = lambda j, i: jnp.minimum(j * ni + i, nslab - 1)
    return pl.pallas_call(
        functools.partial(_in_proj_kernel, tm=tm),
        grid_spec=pltpu.PrefetchScalarGridSpec(
            num_scalar_prefetch=1,
            grid=(PROJ_P // IN_TN, ni),
            in_specs=[
                pl.BlockSpec((tm, k), lambda j, i, t: (i, 0)),
                pl.BlockSpec((ms, k), lambda j, i, t: (0, 0)),
                a_spec(0), a_spec(1), a_spec(2), a_spec(3),
                pl.BlockSpec((None, slab, D_MODEL), lambda j, i, t: (l, slab_idx(j, i), 0)),
            ],
            out_specs=[
                pl.BlockSpec((tm, IN_TN), lambda j, i, t: (i, j)),
                pl.BlockSpec((ms, IN_TN), lambda j, i, t: (0, j)),
                pl.BlockSpec((slab, D_MODEL), lambda j, i, t: (slab_idx(j, i), 0)),
            ],
            scratch_shapes=[pltpu.VMEM((k, IN_TN), BF16)],
        ),
        out_shape=[jax.ShapeDtypeStruct((m, PROJ_P), F32), jax.ShapeDtypeStruct((ms, PROJ_P), F32),
                   jax.ShapeDtypeStruct((D_MODEL, D_MODEL), BF16)],
        compiler_params=_cparams(("arbitrary", "arbitrary")),
        name="in_proj",
    )(tab, h, hs, w_in_t, w_in_t, w_in_t, w_in_t, w_out_all)


def _ssd_body(zx_ref, g_ref, kvdt_ref, cw_ref, cb_ref, dtb_ref, alog_ref, dsk_ref, nw_ref, scw_ref,
              y_ref, yc_ref, cst_ref, ust_ref, xpad_ref, upad_ref, st_ref):
    L = CHUNK
    zg = zx_ref[:, :D_SSD]
    raw = zx_ref[:, D_SSD:]

    xpad_ref[8:8 + L, :] = raw
    cw = cw_ref[...]
    conv = (cw[3:4] * raw + cw[2:3] * xpad_ref[7:7 + L, :] + cw[1:2] * xpad_ref[6:6 + L, :]
            + cw[0:1] * xpad_ref[5:5 + L, :])
    xpad_ref[0:8, :] = raw[L - 8:, :]
    cst_ref[...] = raw[L - 8:, :]
    xbc = _silu(conv + cb_ref[...])
    xs = xbc[:, :D_SSD]

    lane = lax.broadcasted_iota(jnp.int32, (L, 128), 1)
    row = lax.broadcasted_iota(jnp.int32, (L, 128), 0)
    lo = (lane & 64) == 0
    causal = row >= lane

    head_ok = lane[0:1] < SSM_HEADS
    a_row = jnp.where(head_ok, -jnp.exp(alog_ref[...]), 0.0)
    dtr = kvdt_ref[:, 256:384] + dtb_ref[...]
    dt = jnp.maximum(dtr, 0.0) + jnp.log1p(jnp.exp(-jnp.abs(dtr)))
    da = dt * a_row

    tril = jnp.where(causal, 1.0, 0.0).astype(BF16)
    d1, d2, d3 = _split3(da)
    cs = _dot(tril, d1) + _dot(tril, d2) + _dot(tril, d3)
    cs_t = cs.T
    ecs = jnp.exp(cs)
    te = jnp.exp(cs[L - 1:L, :] - cs)
    ecl = ecs[L - 1:L, :]

    def pair(arr, j):
        rows = arr.shape[0]
        return jnp.where(lo[:rows], arr[:, 2 * j:2 * j + 1], arr[:, 2 * j + 1:2 * j + 2])

    cbs, yoffs, updates = [], [], []
    xdts = []
    for j in range(8):
        xdts.append(xs[:, 128 * j:128 * (j + 1)] * pair(dt, j))
    for g in range(2):
        bm = xbc[:, D_SSD + 128 * g:D_SSD + 128 * (g + 1)]
        cm = xbc[:, D_SSD + 256 + 128 * g:D_SSD + 256 + 128 * (g + 1)]
        cm16 = cm.astype(BF16)
        cbs.append(jnp.where(causal, _dot_nt(cm16, bm.astype(BF16)), 0.0))
        yoffs.append(_dot(cm16, st_ref[:, 512 * g:512 * (g + 1)].astype(BF16)))
        xe = jnp.concatenate([xdts[4 * g + i] * pair(te, 4 * g + i) for i in range(4)], axis=1)
        s_new = _dot(bm.T.astype(BF16), xe.astype(BF16))
        cd = jnp.concatenate([pair(ecl, 4 * g + i) for i in range(4)], axis=1)
        updates.append((cd, s_new))
    for g in range(2):
        cd, s_new = updates[g]
        st_ref[:, 512 * g:512 * (g + 1)] = cd * st_ref[:, 512 * g:512 * (g + 1)] + s_new

    ys = []
    for j in range(8):
        g = j // 4
        ca, cb_ = cs[:, 2 * j:2 * j + 1], cs[:, 2 * j + 1:2 * j + 2]
        ra, rb = cs_t[2 * j:2 * j + 1, :], cs_t[2 * j + 1:2 * j + 2, :]
        dec_a = jnp.exp(jnp.minimum(ca - ra, 0.0))
        dec_b = jnp.exp(jnp.minimum(cb_ - rb, 0.0))
        mm = jnp.concatenate([cbs[g] * dec_a, cbs[g] * dec_b], axis=1).astype(BF16)
        xp = xdts[j]
        rhs = jnp.concatenate([jnp.where(lo, xp, 0.0), jnp.where(lo, 0.0, xp)], axis=0).astype(BF16)
        yd = _dot(mm, rhs)
        yo = yoffs[g][:, 128 * (j % 4):128 * (j % 4 + 1)] * pair(ecs, j)
        ys.append(yd + yo + dsk_ref[:, 128 * j:128 * (j + 1)] * xs[:, 128 * j:128 * (j + 1)])
    y = jnp.concatenate(ys, axis=1) * zg
    nw = nw_ref[...]
    outs = []
    for g in range(2):
        yg = y[:, 512 * g:512 * (g + 1)]
        ms = jnp.mean(yg * yg, axis=-1, keepdims=True)
        outs.append(yg * lax.rsqrt(ms + EPS) * nw[:, 512 * g:512 * (g + 1)])
    y_ref[...] = jnp.concatenate(outs, axis=1).astype(BF16)

    gg = g_ref[...]
    u = gg[:, D_CONV:2 * D_CONV] * gg[:, 2 * D_CONV:]
    upad_ref[8:8 + L, :] = u
    sw = scw_ref[...]
    sconv = sw[2:3] * u + sw[1:2] * upad_ref[7:7 + L, :] + sw[0:1] * upad_ref[6:6 + L, :]
    upad_ref[0:8, :] = u[L - 8:, :]
    ust_ref[...] = u[L - 8:, :]
    yc_ref[...] = (gg[:, :D_CONV] * sconv).astype(BF16)


def _head_norm(x, w, bd):
    sq = x * x
    hi = sq.astype(BF16)
    lo = (sq - hi.astype(F32)).astype(BF16)
    ms = _dot(hi, bd) + _dot(lo, bd)
    return x * lax.rsqrt(ms + EPS) * w


def _rope(x, cos, sin_signed, first_half):
    partner = jnp.where(first_half, pltpu.roll(x, 96, axis=1), pltpu.roll(x, 32, axis=1))
    return x * cos + partner * sin_signed


def _seg_mean_matrix():
    r = lax.broadcasted_iota(jnp.int32, (128, 128), 0)
    c = lax.broadcasted_iota(jnp.int32, (128, 128), 1)
    return jnp.where((r & 64) == (c & 64), 1.0 / HEAD_DIM, 0.0).astype(BF16)


def _attn_body(sink_ref, q_ref, kv_ref, cos_ref, sin_ref, qw_ref, kw_ref,
               o_ref, kr_ref, kprev_ref, vprev_ref, *, layer, has_prev):
    L = CHUNK
    bd = _seg_mean_matrix()
    lane = lax.broadcasted_iota(jnp.int32, (L, 128), 1)
    lo = (lane & 64) == 0
    first_half = (lane & 32) == 0
    cos = cos_ref[...]
    sin = sin_ref[...]

    k_new = _rope(_head_norm(kv_ref[:, 0:128], kw_ref[...], bd), cos, sin, first_half)
    v_new = kv_ref[:, 128:256]
    kr_ref[...] = k_new
    kband = jnp.concatenate([kprev_ref[...], k_new], axis=0)
    vband = jnp.concatenate([vprev_ref[...], v_new], axis=0)
    kprev_ref[...] = k_new
    vprev_ref[...] = v_new
    lo2 = jnp.concatenate([lo, lo], axis=0)
    kswap = pltpu.roll(kband, 64, axis=1)
    vswap = pltpu.roll(vband, 64, axis=1)
    k2 = [jnp.where(lo2, kband, kswap).astype(BF16), jnp.where(lo2, kswap, kband).astype(BF16)]
    v2 = [jnp.where(lo2, vband, vswap).astype(BF16), jnp.where(lo2, vswap, vband).astype(BF16)]

    qi = lax.broadcasted_iota(jnp.int32, (L, 2 * L), 0)
    kj = lax.broadcasted_iota(jnp.int32, (L, 2 * L), 1)
    diff = L + qi - kj
    valid = (diff >= 0) & (diff <= WINDOW) & ((kj >= L) | has_prev)

    qt = []
    for t in range(4):
        qn = _head_norm(q_ref[:, 128 * t:128 * (t + 1)], qw_ref[...], bd)
        qt.append(_rope(qn, cos, sin, first_half) * (HEAD_DIM ** -0.5))

    otiles = []
    for g in range(KV_HEADS):
        lhs = []
        for t in (2 * g, 2 * g + 1):
            lhs.append(jnp.where(lo, qt[t], 0.0))
            lhs.append(jnp.where(lo, 0.0, qt[t]))
        s = _dot_nt(jnp.concatenate(lhs, axis=0).astype(BF16), k2[g])
        ps = []
        for i in range(4):
            sink = sink_ref[layer, 4 * g + i]
            sh = jnp.where(valid, s[L * i:L * (i + 1)], -1e30)
            mx = jnp.maximum(jnp.max(sh, axis=-1, keepdims=True), sink)
            p = jnp.exp(sh - mx)
            den = jnp.sum(p, axis=-1, keepdims=True) + jnp.exp(sink - mx)
            ps.append(p * (1.0 / den))
        og = _dot(jnp.concatenate(ps, axis=0).astype(BF16), v2[g])
        otiles.append(jnp.where(lo, og[0:L], og[L:2 * L]))
        otiles.append(jnp.where(lo, og[2 * L:3 * L], og[3 * L:4 * L]))
    o_ref[...] = jnp.concatenate(otiles, axis=1).astype(BF16)


def _mixer_kernel(sink_ref, zx_ref, g_ref, kvdt_ref, q_ref, cos_ref, sin_ref,
                  cw_ref, cb_ref, dtb_ref, alog_ref, dsk_ref, nw_ref, scw_ref, qw_ref, kw_ref,
                  y_ref, yc_ref, st_out_ref, cst_ref, ust_ref, o_ref, kr_ref,
                  xpad_ref, upad_ref, st_ref, kprev_ref, vprev_ref, *, layer):
    c = pl.program_id(1)

    @pl.when(c == 0)
    def _():
        xpad_ref[0:8, :] = jnp.zeros((8, D_XBC), F32)
        upad_ref[0:8, :] = jnp.zeros((8, D_CONV), F32)
        st_ref[...] = jnp.zeros_like(st_ref)
        kprev_ref[...] = jnp.zeros_like(kprev_ref)
        vprev_ref[...] = jnp.zeros_like(vprev_ref)

    for r in range(MIX_CHUNKS):
        rows = lambda ref: ref.at[r * CHUNK:(r + 1) * CHUNK]
        _ssd_body(rows(zx_ref), rows(g_ref), rows(kvdt_ref), cw_ref, cb_ref, dtb_ref, alog_ref, dsk_ref,
                  nw_ref, scw_ref, rows(y_ref), rows(yc_ref), cst_ref, ust_ref, xpad_ref, upad_ref, st_ref)
        _attn_body(sink_ref, rows(q_ref), rows(kvdt_ref), rows(cos_ref), rows(sin_ref), qw_ref, kw_ref,
                   rows(o_ref), rows(kr_ref), kprev_ref, vprev_ref, layer=layer,
                   has_prev=(c > 0) if r == 0 else True)

    @pl.when(c == pl.num_programs(1) - 1)
    def _():
        st_out_ref[...] = st_ref[...].T


def _mixer_prompt(proj, nb, seq, l, sinks, cos_t, sin_t, cw, cb, dtb, alog, dsk, nw, scw, qw, kw):
    tr = MIX_CHUNKS * CHUNK
    nc = seq // tr
    m = nb * seq
    row = lambda b, c: (b * nc + c)
    blk = lambda width, col: pl.BlockSpec((tr, width), lambda b, c: (row(b, c), col))
    per_batch = lambda rows, width: pl.BlockSpec((None, rows, width), lambda b, c: (b, 0, 0))
    return pl.pallas_call(
        functools.partial(_mixer_kernel, layer=l),
        grid=(nb, nc),
        in_specs=[
            pl.BlockSpec(memory_space=pltpu.SMEM),
            blk(2560, 0), blk(1536, 2), blk(512, 9), blk(512, 5),
            pl.BlockSpec((tr, 128), lambda b, c: (c, 0)),
            pl.BlockSpec((tr, 128), lambda b, c: (c, 0)),
            _layer_spec(cw, l), _layer_spec(cb, l), _layer_spec(dtb, l), _layer_spec(alog, l),
            _layer_spec(dsk, l), _layer_spec(nw, l), _layer_spec(scw, l), _layer_spec(qw, l), _layer_spec(kw, l),
        ],
        out_specs=[
            blk(D_SSD, 0), blk(D_CONV, 0),
            per_batch(D_SSD, D_STATE), per_batch(8, D_XBC), per_batch(8, D_CONV),
            blk(D_ATTN, 0), blk(128, 0),
        ],
        out_shape=[
            jax.ShapeDtypeStruct((m, D_SSD), BF16),
            jax.ShapeDtypeStruct((m, D_CONV), BF16),
            jax.ShapeDtypeStruct((nb, D_SSD, D_STATE), F32),
            jax.ShapeDtypeStruct((nb, 8, D_XBC), F32),
            jax.ShapeDtypeStruct((nb, 8, D_CONV), F32),
            jax.ShapeDtypeStruct((m, D_ATTN), BF16),
            jax.ShapeDtypeStruct((m, 128), F32),
        ],
        scratch_shapes=[
            pltpu.VMEM((8 + CHUNK, D_XBC), F32),
            pltpu.VMEM((8 + CHUNK, D_CONV), F32),
            pltpu.VMEM((D_STATE, D_SSD), F32),
            pltpu.VMEM((CHUNK, 128), F32),
            pltpu.VMEM((CHUNK, 128), F32),
        ],
        compiler_params=_cparams(("arbitrary", "arbitrary")),
        name="mixer_prompt",
    )(sinks, proj, proj, proj, proj, cos_t, sin_t, cw, cb, dtb, alog, dsk, nw, scw, qw, kw)


def _outproj_rows(ys_ref, o_ref, yc_ref, x_ref, gate_ref, sc_ref, sh_ref, w_ref, nw_ref, x1_ref, h2_ref):
    acc = _dot(ys_ref[...], w_ref[0:D_SSD, :])
    acc += _dot(o_ref[...], w_ref[D_SSD:D_SSD + D_ATTN, :])
    acc += _dot(yc_ref[...], w_ref[D_SSD + D_ATTN:, :])
    x1 = x_ref[...] + gate_ref[...] * acc
    x1_ref[...] = x1
    h2_ref[...] = _rms_mod(x1, nw_ref[...], sc_ref[...], sh_ref[...]).astype(BF16)


def _outproj_kernel(*refs):
    p_in, s_in, (w_ref, nw_ref), p_out, s_out = refs[0:7], refs[7:14], refs[14:16], refs[16:18], refs[18:20]

    @pl.when(pl.program_id(0) == 0)
    def _():
        _outproj_rows(*s_in, w_ref, nw_ref, *s_out)

    _outproj_rows(*p_in, w_ref, nw_ref, *p_out)


def _row_specs(widths, tm):
    return [pl.BlockSpec((tm, w), lambda i: (i, 0)) for w in widths]


def _const_specs(widths, rows):
    return [pl.BlockSpec((rows, w), lambda i: (0, 0)) for w in widths]


def _outproj(mix_p, x_p, mod_p, mix_s, x_s, mod_s, w_all, l, nw2_all, tm, rpb):
    m, ms = x_p.shape[0], x_s.shape[0]
    widths = (D_SSD, D_ATTN, D_CONV, D_MODEL)
    mods = lambda mod, t, r: [_mod_spec(mod, l, 2, t, r, 0), _mod_spec(mod, l, 4, t, r, 0),
                              _mod_spec(mod, l, 3, t, r, 0)]
    return pl.pallas_call(
        _outproj_kernel,
        grid=(m // tm,),
        in_specs=(_row_specs(widths, tm) + mods(mod_p, tm, rpb)
                  + _const_specs(widths, ms) + mods(mod_s, 0, 1)
                  + [pl.BlockSpec((D_MODEL, D_MODEL), lambda i: (0, 0), pipeline_mode=pl.Buffered(1)),
                     _layer_spec(nw2_all, l)]),
        out_specs=_row_specs((D_MODEL, D_MODEL), tm) + _const_specs((D_MODEL, D_MODEL), ms),
        out_shape=[
            jax.ShapeDtypeStruct((m, D_MODEL), F32), jax.ShapeDtypeStruct((m, D_MODEL), BF16),
            jax.ShapeDtypeStruct((ms, D_MODEL), F32), jax.ShapeDtypeStruct((ms, D_MODEL), BF16),
        ],
        compiler_params=_cparams(("arbitrary",)),
        name="out_proj",
    )(*mix_p, x_p, mod_p, mod_p, mod_p, *mix_s, x_s, mod_s, mod_s, mod_s, w_all, nw2_all)


def _up_seq_kernel(h_ref, hs_ref, wg_ref, wu_ref, cw_ref, cb_ref, prev_ref, wd_ref,
                   a_ref, gst_ref, as_ref, gts_ref, wd16_ref, pad_ref, wg16_ref, wu16_ref, *, tm, rpb):
    i = pl.program_id(1)
    cw = cw_ref[...]
    cb = cb_ref[...]
    wd16_ref[...] = wd_ref[...].astype(BF16)

    @pl.when(i == 0)
    def _():
        wg16_ref[...] = wg_ref[...].astype(BF16)
        wu16_ref[...] = wu_ref[...].astype(BF16)
        hs = hs_ref[...]
        gt = _dot(hs, wg16_ref[...])
        u = _dot(hs, wu16_ref[...])
        conv = cw[2:3] * gt + cw[1:2] * prev_ref[1] + cw[0:1] * prev_ref[0]
        gts_ref[...] = gt
        as_ref[...] = (_silu(conv + cb) * u).astype(BF16)

    @pl.when((i * tm) % rpb == 0)
    def _():
        pad_ref[0:8, :] = jnp.zeros((8, pad_ref.shape[1]), F32)

    sub = 512
    for r in range(tm // sub):
        h = h_ref[r * sub:(r + 1) * sub, :]
        gt = _dot(h, wg16_ref[...])
        u = _dot(h, wu16_ref[...])
        base = 8 + r * sub
        pad_ref[base:base + sub, :] = gt
        conv = (cw[2:3] * gt + cw[1:2] * pad_ref[base - 1:base - 1 + sub, :]
                + cw[0:1] * pad_ref[base - 2:base - 2 + sub, :])
        a_ref[r * sub:(r + 1) * sub, :] = (_silu(conv + cb) * u).astype(BF16)
    tail = pad_ref[tm:tm + 8, :]
    pad_ref[0:8, :] = tail
    gst_ref[...] = tail


def _up_proj(h, hs, w_up_all, l, cw_all, cb_all, prev_all, w_down_all, nb, tm, tn):
    m = h.shape[0]
    ms = hs.shape[0]
    rpb = m // nb
    nt = D_FF // tn
    ni = m // tm
    slab = D_FF // (nt * ni)
    assert slab * nt * ni == D_FF and slab % 16 == 0
    return pl.pallas_call(
        functools.partial(_up_seq_kernel, tm=tm, rpb=rpb),
        grid=(nt, ni),
        in_specs=[
            pl.BlockSpec((tm, D_MODEL), lambda j, i: (i, 0)),
            pl.BlockSpec((ms, D_MODEL), lambda j, i: (0, 0)),
            pl.BlockSpec((None, D_MODEL, tn), lambda j, i: (l, 0, j)),
            pl.BlockSpec((None, D_MODEL, tn), lambda j, i: (l, 0, j + nt)),
            pl.BlockSpec((None, 3, tn), lambda j, i: (l, 0, j)),
            pl.BlockSpec((None, 1, tn), lambda j, i: (l, 0, j)),
            pl.BlockSpec((None, 2, ms, tn), lambda j, i: (l, 0, 0, j)),
            pl.BlockSpec((None, slab, D_MODEL), lambda j, i: (l, j * ni + i, 0)),
        ],
        out_specs=[
            pl.BlockSpec((tm, tn), lambda j, i: (i, j)),
            pl.BlockSpec((None, 8, tn), lambda j, i: ((i * tm) // rpb, 0, j)),
            pl.BlockSpec((ms, tn), lambda j, i: (0, j)),
            pl.BlockSpec((ms, tn), lambda j, i: (0, j)),
            pl.BlockSpec((slab, D_MODEL), lambda j, i: (j * ni + i, 0)),
        ],
        out_shape=[
            jax.ShapeDtypeStruct((m, D_FF), BF16),
            jax.ShapeDtypeStruct((nb, 8, D_FF), F32),
            jax.ShapeDtypeStruct((ms, D_FF), BF16),
            jax.ShapeDtypeStruct((ms, D_FF), F32),
            jax.ShapeDtypeStruct((D_FF, D_MODEL), BF16),
        ],
        scratch_shapes=[
            pltpu.VMEM((8 + tm, tn), F32),
            pltpu.VMEM((D_MODEL, tn), BF16),
            pltpu.VMEM((D_MODEL, tn), BF16),
        ],
        compiler_params=_cparams(("arbitrary", "arbitrary")),
        name="up_proj",
    )(h, hs, w_up_all, w_up_all, cw_all, cb_all, prev_all, w_down_all)


def _down_rows(a_ref, x_ref, gate_ref, mods, w_ref, nw_ref, x2_ref, hn_ref):
    x2 = x_ref[...] + gate_ref[...] * _dot(a_ref[...], w_ref[...])
    x2_ref[...] = x2
    if hn_ref is not None:
        hn_ref[...] = _rms_mod(x2, nw_ref[...], mods[0][...], mods[1][...]).astype(BF16)


def _down_kernel(*refs, with_norm):
    refs = list(refs)
    n_in = 5 if with_norm else 3
    p_in, s_in = refs[0:n_in], refs[n_in:2 * n_in]
    w_ref = refs[2 * n_in]
    nw_ref = refs[2 * n_in + 1] if with_norm else None
    outs = refs[2 * n_in + (2 if with_norm else 1):]
    p_out = (outs[0], outs[1]) if with_norm else (outs[0], None)
    s_out = (outs[2], outs[3]) if with_norm else (outs[1], None)

    @pl.when(pl.program_id(0) == 0)
    def _():
        _down_rows(s_in[0], s_in[1], s_in[2], s_in[3:], w_ref, nw_ref, *s_out)

    _down_rows(p_in[0], p_in[1], p_in[2], p_in[3:], w_ref, nw_ref, *p_out)


def _down(a_p, x_p, mod_p, a_s, x_s, mod_s, w_all, l, nw1_all, tm, rpb):
    m, ms = x_p.shape[0], x_s.shape[0]
    with_norm = l + 1 < nw1_all.shape[0]

    def mods(mod, t, r):
        sp = [_mod_spec(mod, l, 5, t, r, 0)]
        if with_norm:
            sp += [_mod_spec(mod, l + 1, 1, t, r, 0), _mod_spec(mod, l + 1, 0, t, r, 0)]
        return sp

    n_mod = 3 if with_norm else 1
    in_specs = (_row_specs((D_FF, D_MODEL), tm) + mods(mod_p, tm, rpb)
                + _const_specs((D_FF, D_MODEL), ms) + mods(mod_s, 0, 1)
                + [pl.BlockSpec((D_FF, D_MODEL), lambda i: (0, 0), pipeline_mode=pl.Buffered(1))])
    args = [a_p, x_p] + [mod_p] * n_mod + [a_s, x_s] + [mod_s] * n_mod + [w_all]
    out_w = (D_MODEL, D_MODEL) if with_norm else (D_MODEL,)
    out_dt = (F32, BF16) if with_norm else (F32,)
    if with_norm:
        in_specs.append(_layer_spec(nw1_all, l + 1))
        args.append(nw1_all)
    res = pl.pallas_call(
        functools.partial(_down_kernel, with_norm=with_norm),
        grid=(m // tm,),
        in_specs=in_specs,
        out_specs=_row_specs(out_w, tm) + _const_specs(out_w, ms),
        out_shape=([jax.ShapeDtypeStruct((m, D_MODEL), d) for d in out_dt]
                   + [jax.ShapeDtypeStruct((ms, D_MODEL), d) for d in out_dt]),
        compiler_params=_cparams(("arbitrary",)),
        name="down_proj",
    )(*args)
    if with_norm:
        return res[0], res[1], res[2], res[3]
    return res[0], None, res[1], None


def _expand_matrix():
    r = lax.broadcasted_iota(jnp.int32, (128, D_SSD), 0)
    c = lax.broadcasted_iota(jnp.int32, (128, D_SSD), 1)
    return jnp.where(r == (c >> 6), 1.0, 0.0).astype(BF16)


def _ssd_step_kernel(p_ref, st_ref, cprev_ref, sprev_ref, cw_ref, cb_ref, dtb_ref, alog_ref, dsk_ref, nw_ref,
                     scw_ref, y_ref, yc_ref, u_ref, stn_ref):
    nbb = p_ref.shape[0]
    z = p_ref[:, C_Z:C_Z + D_SSD]
    raw = p_ref[:, C_XBC:C_XBC + D_XBC]
    cw = cw_ref[...]
    conv = cw[3:4] * raw + cw[2:3] * cprev_ref[2] + cw[1:2] * cprev_ref[1] + cw[0:1] * cprev_ref[0]
    xbc = _silu(conv + cb_ref[...])
    xs = xbc[:, :D_SSD]
    bm = xbc[:, D_SSD:D_SSD + 256]
    cm = xbc[:, D_SSD + 256:]

    lane = lax.broadcasted_iota(jnp.int32, (1, 128), 1)
    a_row = jnp.where(lane < SSM_HEADS, -jnp.exp(alog_ref[...]), 0.0)
    dtr = p_ref[:, C_DT:C_DT + 128] + dtb_ref[...]
    dt = jnp.maximum(dtr, 0.0) + jnp.log1p(jnp.exp(-jnp.abs(dtr)))
    dec = jnp.exp(dt * a_row)
    em = _expand_matrix()

    def expand(v):
        v1, v2, v3 = _split3(v)
        return _dot(v1, em) + _dot(v2, em) + _dot(v3, em)

    xdt_t = (xs * expand(dt)).T
    dec_t = expand(dec).T
    rowid = lax.broadcasted_iota(jnp.int32, (nbb, 512), 0)
    ysum = [jnp.zeros((nbb, 512), F32), jnp.zeros((nbb, 512), F32)]
    for i in range(nbb):
        for g in range(2):
            rs = slice(512 * g, 512 * (g + 1))
            hn = st_ref[i, rs, :] * dec_t[rs, i:i + 1] + xdt_t[rs, i:i + 1] * bm[i:i + 1, 128 * g:128 * (g + 1)]
            stn_ref[i, rs, :] = hn
            r = _dot_nt(cm[:, 128 * g:128 * (g + 1)].astype(BF16), hn.astype(BF16))
            ysum[g] = ysum[g] + jnp.where(rowid == i, r, 0.0)
    y = jnp.concatenate(ysum, axis=1) + dsk_ref[...] * xs
    y = y * _silu(z)
    nw = nw_ref[...]
    outs = []
    for g in range(2):
        yg = y[:, 512 * g:512 * (g + 1)]
        ms = jnp.mean(yg * yg, axis=-1, keepdims=True)
        outs.append(yg * lax.rsqrt(ms + EPS) * nw[:, 512 * g:512 * (g + 1)])
    y_ref[...] = jnp.concatenate(outs, axis=1).astype(BF16)

    u = p_ref[:, C_G + D_CONV:C_G + 2 * D_CONV] * p_ref[:, C_G + 2 * D_CONV:C_G + 3 * D_CONV]
    sw = scw_ref[...]
    sconv = sw[2:3] * u + sw[1:2] * sprev_ref[1] + sw[0:1] * sprev_ref[0]
    u_ref[...] = u
    yc_ref[...] = (p_ref[:, C_G:C_G + D_CONV] * sconv).astype(BF16)


def _ssd_step(proj, state_all, cprev_all, sprev_all, l, cw, cb, dtb, alog, dsk, nw, scw, nbb=8):
    nb = proj.shape[0]
    return pl.pallas_call(
        _ssd_step_kernel,
        grid=(nb // nbb,),
        in_specs=[
            pl.BlockSpec((nbb, PROJ_P), lambda i: (i, 0)),
            pl.BlockSpec((None, nbb, D_SSD, D_STATE), lambda i: (l, i, 0, 0)),
            pl.BlockSpec((None, 3, nbb, D_XBC), lambda i: (l, 0, i, 0)),
            pl.BlockSpec((None, 2, nbb, D_CONV), lambda i: (l, 0, i, 0)),
            _layer_spec(cw, l), _layer_spec(cb, l), _layer_spec(dtb, l), _layer_spec(alog, l),
            _layer_spec(dsk, l), _layer_spec(nw, l), _layer_spec(scw, l),
        ],
        out_specs=[
            pl.BlockSpec((nbb, D_SSD), lambda i: (i, 0)),
            pl.BlockSpec((nbb, D_CONV), lambda i: (i, 0)),
            pl.BlockSpec((nbb, D_CONV), lambda i: (i, 0)),
            pl.BlockSpec((nbb, D_SSD, D_STATE), lambda i: (i, 0, 0)),
        ],
        out_shape=[
            jax.ShapeDtypeStruct((nb, D_SSD), BF16),
            jax.ShapeDtypeStruct((nb, D_CONV), BF16),
            jax.ShapeDtypeStruct((nb, D_CONV), F32),
            jax.ShapeDtypeStruct((nb, D_SSD, D_STATE), F32),
        ],
        compiler_params=_cparams(("arbitrary",)),
        name="ssd_step",
    )(proj, state_all, cprev_all, sprev_all, cw, cb, dtb, alog, dsk, nw, scw)


def _qk_step_kernel(p_ref, cos_ref, sin_ref, qw_ref, kw_ref, q_ref, k_ref):
    bd = _seg_mean_matrix()
    lane = lax.broadcasted_iota(jnp.int32, (p_ref.shape[0], 128), 1)
    first_half = (lane & 32) == 0
    cos = cos_ref[...]
    sin = sin_ref[...]
    for t in range(4):
        qn = _head_norm(p_ref[:, C_Q + 128 * t:C_Q + 128 * (t + 1)], qw_ref[...], bd)
        q_ref[:, 128 * t:128 * (t + 1)] = _rope(qn, cos, sin, first_half) * (HEAD_DIM ** -0.5)
    k_ref[...] = _rope(_head_norm(p_ref[:, C_K:C_K + 128], kw_ref[...], bd), cos, sin, first_half)


def _qk_step(proj, cos_r, sin_r, l, qw, kw):
    nb = proj.shape[0]
    full = lambda a: pl.BlockSpec(a.shape, lambda i: (0,) * a.ndim)
    return pl.pallas_call(
        _qk_step_kernel,
        grid=(1,),
        in_specs=[full(proj), full(cos_r), full(sin_r), _layer_spec(qw, l), _layer_spec(kw, l)],
        out_specs=[pl.BlockSpec((nb, D_ATTN), lambda i: (0, 0)), pl.BlockSpec((nb, 128), lambda i: (0, 0))],
        out_shape=[jax.ShapeDtypeStruct((nb, D_ATTN), F32), jax.ShapeDtypeStruct((nb, 128), F32)],
        name="qk_step",
    )(proj, cos_r, sin_r, qw, kw)


def _attn_step_kernel(q_ref, kc_ref, vc_ref, knew_ref, vnew_ref, sink_ref, o_ref):
    q = q_ref[...]
    s = lax.dot_general(q.astype(BF16), kc_ref[...].astype(BF16), (((2,), (2,)), ((0,), (0,))),
                        preferred_element_type=F32)
    s_new = jnp.sum(q * knew_ref[...], axis=-1, keepdims=True)
    sink = sink_ref[...]
    mx = jnp.maximum(jnp.maximum(jnp.max(s, axis=-1, keepdims=True), s_new), sink)
    p = jnp.exp(s - mx)
    p_new = jnp.exp(s_new - mx)
    den = jnp.sum(p, axis=-1, keepdims=True) + p_new + jnp.exp(sink - mx)
    o = lax.dot_general((p / den).astype(BF16), vc_ref[...].astype(BF16), (((2,), (1,)), ((0,), (0,))),
                        preferred_element_type=F32)
    o_ref[...] = o + (p_new / den) * vnew_ref[...]


def _attn_step(q3, kc_all, vc_all, l, knew, vnew, sink_all):
    nb = q3.shape[0]
    full = lambda a: pl.BlockSpec(a.shape, lambda i: (0,) * a.ndim)
    return pl.pallas_call(
        _attn_step_kernel,
        grid=(1,),
        in_specs=[full(q3), _layer_spec(kc_all, l), _layer_spec(vc_all, l), full(knew), full(vnew),
                  _layer_spec(sink_all, l)],
        out_specs=pl.BlockSpec((nb, ATT_HEADS, 128), lambda i: (0, 0, 0)),
        out_shape=jax.ShapeDtypeStruct((nb, ATT_HEADS, 128), F32),
        name="attn_step",
    )(q3, kc_all, vc_all, knew, vnew, sink_all)


def _rope_tables(pos):
    half = HEAD_DIM // 2
    inv = jnp.power(ROPE_THETA, -jnp.arange(half, dtype=F32) / half)
    ang = pos[:, None] * inv[None, :]
    cos = jnp.cos(ang)
    sin = jnp.sin(ang)
    cos_t = jnp.concatenate([cos, cos, cos, cos], axis=-1)
    sin_t = jnp.concatenate([-sin, sin, -sin, sin], axis=-1)
    return cos_t, sin_t


def kernel(x_prompt, x_sample, state_ssm, state_ssm_conv, cache_win_k, cache_win_v, state_short_conv,
           state_ffn_conv, c_prompt, c_sample, norm1_w, norm2_w, w_ada, b_ada, w_in, ssm_conv_w, ssm_conv_b,
           dt_bias, a_log, d_skip, ssm_norm_w, q_norm_w, k_norm_w, sinks, sconv_w, w_out, w_up, ffn_conv_w,
           ffn_conv_b, w_down):
    nbp, seq, _ = x_prompt.shape
    nbs = x_sample.shape[0]
    depth = w_in.shape[0]
    mp = nbp * seq
    nbuf = cache_win_k.shape[2]

    in_tab = _in_proj_table()
    w_in_t = jnp.swapaxes(w_in, 1, 2)

    n1 = norm1_w.reshape(depth, 1, D_MODEL)
    n2 = norm2_w.reshape(depth, 1, D_MODEL)
    cb = ssm_conv_b.reshape(depth, 1, D_XBC)
    dtb = jnp.pad(dt_bias, ((0, 0), (0, 128 - SSM_HEADS))).reshape(depth, 1, 128)
    alog = jnp.pad(a_log, ((0, 0), (0, 128 - SSM_HEADS))).reshape(depth, 1, 128)
    dsk = jnp.repeat(d_skip, SSM_HEAD_DIM, axis=1).reshape(depth, 1, D_SSD)
    nw = ssm_norm_w.reshape(depth, 1, D_SSD)
    qw = jnp.tile(q_norm_w, (1, 2)).reshape(depth, 1, 128)
    kw = jnp.tile(k_norm_w, (1, 2)).reshape(depth, 1, 128)
    fcb = ffn_conv_b.reshape(depth, 1, D_FF)
    sink3 = sinks.reshape(depth, ATT_HEADS, 1)

    st_all = state_ssm.reshape(depth, nbs, D_SSD, D_STATE)
    cprev = jnp.transpose(state_ssm_conv, (0, 2, 1, 3))
    sprev = jnp.transpose(state_short_conv, (0, 2, 1, 3))
    fprev = jnp.transpose(state_ffn_conv, (0, 2, 1, 3))
    ck = cache_win_k.reshape(depth, nbs, nbuf, 128)
    cv = cache_win_v.reshape(depth, nbs, nbuf, 128)

    c_all = jnp.concatenate([c_prompt, jnp.zeros((8 - nbp, D_MODEL), F32), c_sample], axis=0)
    mod_all = _modulation(c_all, w_ada, b_ada)
    mod_p = mod_all[:, :nbp].reshape(depth, nbp, 1, 6 * D_MODEL)
    mod_s = mod_all[:, 8:].reshape(depth, 1, nbs, 6 * D_MODEL)

    cos_p, sin_p = _rope_tables(jnp.arange(seq, dtype=F32))
    cos_s, sin_s = _rope_tables(jnp.full((1,), float(PAST_LEN), F32))

    head_g = jnp.arange(ATT_HEADS) // (ATT_HEADS // KV_HEADS)
    q_place = (head_g[:, None] == jnp.arange(KV_HEADS)[None, :]).astype(F32)[None, :, :, None]

    xp = x_prompt.reshape(mp, D_MODEL)
    xs = x_sample.reshape(nbs, D_MODEL)
    TM = 512
    hp = _norm_mod(xp, n1, mod_p, 0, 1, 0, TM, seq)
    hs = _norm_mod(xs, n1, mod_s, 0, 1, 0, nbs, nbs)

    outs_p = [[] for _ in range(6)]
    outs_s = [[] for _ in range(6)]
    for l in range(depth):
        proj, projs, w_out_l = _in_proj(hp, hs, w_in_t, l, in_tab, w_out, min(1024, seq))

        y_ssm, y_conv, st_t, cst, ust, o, k_rot = _mixer_prompt(
            proj, nbp, seq, l, sinks, cos_p, sin_p, ssm_conv_w, cb, dtb, alog, dsk, nw, sconv_w, qw, kw)

        y_s, yc_s, u_s, st_new = _ssd_step(projs, st_all, cprev, sprev, l, ssm_conv_w, cb, dtb, alog, dsk, nw,
                                           sconv_w)
        q_r, k_r = _qk_step(projs, cos_s, sin_s, l, qw, kw)
        v_r = projs[:, C_V:C_V + 128]
        q3 = (q_r.reshape(nbs, ATT_HEADS, 1, HEAD_DIM) * q_place).reshape(nbs, ATT_HEADS, 128)
        o3 = _attn_step(q3, ck, cv, l, k_r[:, None, :], v_r[:, None, :], sink3)
        o4 = o3.reshape(nbs, ATT_HEADS, KV_HEADS, HEAD_DIM)
        o_s = jnp.concatenate([o4[:, :4, 0], o4[:, 4:, 1]], axis=1).reshape(nbs, D_ATTN).astype(BF16)

        xp, h2, xs, h2s = _outproj((y_ssm, o, y_conv), xp, mod_p, (y_s, o_s, yc_s), xs, mod_s, w_out_l, l, n2,
                                   TM, seq)
        act, gst, act_s, gt_s, w_down_l = _up_proj(h2, h2s, w_up, l, ffn_conv_w, fcb, fprev, w_down, nbp,
                                                   min(2048, seq), 512)
        xp, hp, xs, hs = _down(act, xp, mod_p, act_s, xs, mod_s, w_down_l, l, n1, 256, seq)

        proj3 = proj.reshape(nbp, seq, PROJ_P)
        outs_p[0].append(st_t.reshape(nbp, SSM_HEADS, SSM_HEAD_DIM, D_STATE))
        outs_p[1].append(cst[:, 8 - (SSM_CONV - 1):])
        outs_p[2].append(k_rot.reshape(nbp, seq, KV_HEADS, HEAD_DIM)[:, seq - WINDOW:])
        outs_p[3].append(proj3[:, seq - WINDOW:, C_V:C_V + 128].reshape(nbp, WINDOW, KV_HEADS, HEAD_DIM))
        outs_p[4].append(ust[:, 6:])
        outs_p[5].append(gst[:, 6:])

        outs_s[0].append(st_new.reshape(nbs, SSM_HEADS, SSM_HEAD_DIM, D_STATE))
        outs_s[1].append(projs[:, C_XBC:C_XBC + D_XBC])
        outs_s[2].append(k_r)
        outs_s[3].append(v_r)
        outs_s[4].append(u_s)
        outs_s[5].append(gt_s)

    res_p = [jnp.stack(a, axis=0) for a in outs_p]
    new_rows = [jnp.stack(a, axis=0) for a in outs_s]
    s_ssm = new_rows[0]
    s_ssm_conv = jnp.concatenate([state_ssm_conv[:, :, 1:], new_rows[1][:, :, None]], axis=2)
    s_win_k = jnp.concatenate([ck[:, :, 1:], new_rows[2][:, :, None]], axis=2).reshape(cache_win_k.shape)
    s_win_v = jnp.concatenate([cv[:, :, 1:], new_rows[3][:, :, None]], axis=2).reshape(cache_win_v.shape)
    s_short = jnp.concatenate([state_short_conv[:, :, 1:], new_rows[4][:, :, None]], axis=2)
    s_ffn = jnp.concatenate([state_ffn_conv[:, :, 1:], new_rows[5][:, :, None]], axis=2)
    return (xp.reshape(nbp, seq, D_MODEL), xs.reshape(nbs, 1, D_MODEL), *res_p,
            s_ssm, s_ssm_conv, s_win_k, s_win_v, s_short, s_ffn)
```

```python
import functools

import jax
import jax.numpy as jnp
import numpy as np
from jax import lax
from jax.experimental import pallas as pl
from jax.experimental.pallas import tpu as pltpu

F32 = jnp.float32
BF16 = jnp.bfloat16

D_MODEL = 2048
D_SSD = 1024
SSM_HEADS = 16
SSM_HEAD_DIM = 64
D_STATE = 128
SSM_CONV = 4
CHUNK = 128
D_XBC = 1536
HEAD_DIM = 64
D_ATTN = 512
ATT_HEADS = 8
KV_HEADS = 2
WINDOW = 128
ROPE_THETA = 10000.0
D_CONV = 512
D_FF = 5632
PROJ_W = 4880
EPS = 1e-6
PAST_LEN = 16384

C_Z, C_XBC, C_Q, C_G, C_K, C_V, C_DT, PROJ_P = 0, 1024, 2560, 3072, 4608, 4736, 4864, 5120

VMEM_LIMIT = 56 * 1024 * 1024


def _cparams(sem):
    return pltpu.CompilerParams(dimension_semantics=sem, vmem_limit_bytes=VMEM_LIMIT)


def _silu(x):
    return x * jax.nn.sigmoid(x)


def _rms_mod(x, w, scale, shift):
    ms = jnp.mean(x * x, axis=-1, keepdims=True)
    return (x * lax.rsqrt(ms + EPS) * w) * (1.0 + scale) + shift


def _split3(x):
    h1 = x.astype(BF16)
    r1 = x - h1.astype(F32)
    h2 = r1.astype(BF16)
    r2 = r1 - h2.astype(F32)
    return h1, h2, r2.astype(BF16)


def _dot(a, b):
    return jnp.dot(a, b, preferred_element_type=F32)


def _dot_nt(a, b):
    return lax.dot_general(a, b, (((1,), (1,)), ((), ())), preferred_element_type=F32)


def _layer_spec(arr, l):
    shape = arr.shape[1:]
    return pl.BlockSpec((None,) + shape, lambda *_: (l,) + (0,) * len(shape))


def _mod_kernel(c_ref, w_ref, b_ref, o_ref, a_ref):
    a = _silu(c_ref[...]).astype(BF16)
    a_ref[...] = a
    o_ref[...] = _dot(a, w_ref[...].astype(BF16)) + b_ref[...]


def _modulation_first(c_all, w_ada, b_ada3, tn=1024):
    _, d, n = w_ada.shape
    r = c_all.shape[0]
    return pl.pallas_call(
        _mod_kernel,
        grid=(n // tn,),
        in_specs=[
            pl.BlockSpec((r, d), lambda j: (0, 0)),
            pl.BlockSpec((None, d, tn), lambda j: (0, 0, j)),
            pl.BlockSpec((None, 1, tn), lambda j: (0, 0, j)),
        ],
        out_specs=[pl.BlockSpec((r, tn), lambda j: (0, j)), pl.BlockSpec((r, d), lambda j: (0, 0))],
        out_shape=[jax.ShapeDtypeStruct((r, n), F32), jax.ShapeDtypeStruct((r, d), BF16)],
        compiler_params=_cparams(("arbitrary",)),
        name="modulation",
    )(c_all, w_ada, b_ada3)


def _norm_kernel(x_ref, w_ref, sc_ref, sh_ref, o_ref):
    o_ref[...] = _rms_mod(x_ref[...], w_ref[...], sc_ref[...], sh_ref[...]).astype(BF16)


def _mod_spec(mod, j, tm, rpb):
    r = mod.shape[1]
    return pl.BlockSpec((None, r, D_MODEL), lambda i: ((i * tm) // rpb, 0, j))


def _norm_mod(x, w_all, mod, l, j_scale, j_shift, tm, rpb):
    m = x.shape[0]
    return pl.pallas_call(
        _norm_kernel,
        grid=(m // tm,),
        in_specs=[
            pl.BlockSpec((tm, D_MODEL), lambda i: (i, 0)),
            _layer_spec(w_all, l),
            _mod_spec(mod, j_scale, tm, rpb),
            _mod_spec(mod, j_shift, tm, rpb),
        ],
        out_specs=pl.BlockSpec((tm, D_MODEL), lambda i: (i, 0)),
        out_shape=jax.ShapeDtypeStruct((m, D_MODEL), BF16),
        compiler_params=_cparams(("arbitrary",)),
        name="norm_mod",
    )(x, w_all, mod, mod)


MIX_CHUNKS = 2
IN_TN = 1024
IN_SRC = 256
IN_SUB = 256
IN_ROW_UNIT = 16


def _in_proj_table():
    segs = [(C_Z, C_Q, 0), (C_Q, C_G, 2576), (C_G, C_K, 3344), (C_K, C_DT, 3088), (C_DT, PROJ_P, 2560)]
    rows = []
    for st in range(PROJ_P // IN_SRC):
        c = st * IN_SRC
        lo, _, src = [g for g in segs if g[0] <= c < g[1]][0]
        rows.append((src + c - lo) // IN_ROW_UNIT)
    return jnp.array(rows, jnp.int32)


def _in_proj_kernel(tab_ref, h_ref, hs_ref, a0_ref, a1_ref, a2_ref, a3_ref, wo_ref,
                    o_ref, os_ref, wo16_ref, w16_ref, *, tm):
    del tab_ref
    j = pl.program_id(0)
    i = pl.program_id(1)
    nsub = tm // IN_SUB
    wo16_ref[...] = wo_ref[...].astype(BF16)

    @pl.when(i == 0)
    def _():
        for s, a_ref in enumerate((a0_ref, a1_ref, a2_ref, a3_ref)):
            w16_ref[:, IN_SRC * s:IN_SRC * (s + 1)] = a_ref[0].T.astype(BF16)
        os_ref[...] = _dot(hs_ref[...], w16_ref[...])

    def rows(r):
        return slice(r * IN_SUB, (r + 1) * IN_SUB)

    def tile(r):
        return _dot(h_ref[rows(r), :], w16_ref[...])

    @pl.when(j == 0)
    def _():
        for r in range(nsub):
            o_ref[rows(r), :] = _silu(tile(r))

    @pl.when(j > 0)
    def _():
        for r in range(nsub):
            o_ref[rows(r), :] = tile(r)


def _in_proj(h, hs, w_in_t, l, tab, w_out_all, tm):
    m, k = h.shape
    ms = hs.shape[0]
    nsrc = IN_TN // IN_SRC
    ni = m // tm
    nsteps = (PROJ_P // IN_TN) * ni
    nslab = 1 << (nsteps.bit_length() - 1)
    slab = D_MODEL // nslab
    a_spec = lambda s: pl.BlockSpec((pl.Element(1), pl.Element(IN_SRC), pl.Element(k)),
                                    lambda j, i, t: (l, t[nsrc * j + s] * IN_ROW_UNIT, 0))
    slab_idx = lambda j, i: jnp.minimum(j * ni + i, nslab - 1)
    return pl.pallas_call(
        functools.partial(_in_proj_kernel, tm=tm),
        grid_spec=pltpu.PrefetchScalarGridSpec(
            num_scalar_prefetch=1,
            grid=(PROJ_P // IN_TN, ni),
            in_specs=[
                pl.BlockSpec((tm, k), lambda j, i, t: (i, 0)),
                pl.BlockSpec((ms, k), lambda j, i, t: (0, 0)),
                a_spec(0), a_spec(1), a_spec(2), a_spec(3),
                pl.BlockSpec((None, slab, D_MODEL), lambda j, i, t: (l, slab_idx(j, i), 0)),
            ],
            out_specs=[
                pl.BlockSpec((tm, IN_TN), lambda j, i, t: (i, j)),
                pl.BlockSpec((ms, IN_TN), lambda j, i, t: (0, j)),
                pl.BlockSpec((slab, D_MODEL), lambda j, i, t: (slab_idx(j, i), 0)),
            ],
            scratch_shapes=[pltpu.VMEM((k, IN_TN), BF16)],
        ),
        out_shape=[jax.ShapeDtypeStruct((m, PROJ_P), F32), jax.ShapeDtypeStruct((ms, PROJ_P), F32),
                   jax.ShapeDtypeStruct((D_MODEL, D_MODEL), BF16)],
        compiler_params=_cparams(("arbitrary", "arbitrary")),
        name="in_proj",
    )(tab, h, hs, w_in_t, w_in_t, w_in_t, w_in_t, w_out_all)


def _ssd_body(zx_ref, g_ref, kvdt_ref, cw_ref, cb_ref, dtb_ref, alog_ref, dsk_ref, nw_ref, scw_ref,
              y_ref, yc_ref, cst_ref, ust_ref, xpad_ref, upad_ref, st_ref):
    L = CHUNK
    zg = zx_ref[:, :D_SSD]
    raw = zx_ref[:, D_SSD:]

    xpad_ref[8:8 + L, :] = raw
    cw = cw_ref[...]
    conv = (cw[3:4] * raw + cw[2:3] * xpad_ref[7:7 + L, :] + cw[1:2] * xpad_ref[6:6 + L, :]
            + cw[0:1] * xpad_ref[5:5 + L, :])
    xpad_ref[0:8, :] = raw[L - 8:, :]
    cst_ref[...] = raw[L - 8:, :]
    xbc = _silu(conv + cb_ref[...])
    xs = xbc[:, :D_SSD]

    lane = lax.broadcasted_iota(jnp.int32, (L, 128), 1)
    row = lax.broadcasted_iota(jnp.int32, (L, 128), 0)
    lo = (lane & 64) == 0
    causal = row >= lane

    head_ok = lane[0:1] < SSM_HEADS
    a_row = jnp.where(head_ok, -jnp.exp(alog_ref[...]), 0.0)
    dtr = kvdt_ref[:, 256:384] + dtb_ref[...]
    dt = jnp.maximum(dtr, 0.0) + jnp.log1p(jnp.exp(-jnp.abs(dtr)))
    da = dt * a_row

    tril = jnp.where(causal, 1.0, 0.0).astype(BF16)
    d1, d2, d3 = _split3(da)
    cs = _dot(tril, d1) + _dot(tril, d2) + _dot(tril, d3)
    cs_t = cs.T
    ecs = jnp.exp(cs)
    te = jnp.exp(cs[L - 1:L, :] - cs)
    ecl = ecs[L - 1:L, :]

    def pair(arr, j):
        rows = arr.shape[0]
        return jnp.where(lo[:rows], arr[:, 2 * j:2 * j + 1], arr[:, 2 * j + 1:2 * j + 2])

    cbs, yoffs, updates = [], [], []
    xdts = []
    for j in range(8):
        xdts.append(xs[:, 128 * j:128 * (j + 1)] * pair(dt, j))
    for g in range(2):
        bm = xbc[:, D_SSD + 128 * g:D_SSD + 128 * (g + 1)]
        cm = xbc[:, D_SSD + 256 + 128 * g:D_SSD + 256 + 128 * (g + 1)]
        cm16 = cm.astype(BF16)
        cbs.append(jnp.where(causal, _dot_nt(cm16, bm.astype(BF16)), 0.0))
        yoffs.append(_dot(cm16, st_ref[:, 512 * g:512 * (g + 1)].astype(BF16)))
        xe = jnp.concatenate([xdts[4 * g + i] * pair(te, 4 * g + i) for i in range(4)], axis=1)
        s_new = _dot(bm.T.astype(BF16), xe.astype(BF16))
        cd = jnp.concatenate([pair(ecl, 4 * g + i) for i in range(4)], axis=1)
        updates.append((cd, s_new))
    for g in range(2):
        cd, s_new = updates[g]
        st_ref[:, 512 * g:512 * (g + 1)] = cd * st_ref[:, 512 * g:512 * (g + 1)] + s_new

    ys = []
    for j in range(8):
        g = j // 4
        ca, cb_ = cs[:, 2 * j:2 * j + 1], cs[:, 2 * j + 1:2 * j + 2]
        ra, rb = cs_t[2 * j:2 * j + 1, :], cs_t[2 * j + 1:2 * j + 2, :]
        dec_a = jnp.exp(jnp.minimum(ca - ra, 0.0))
        dec_b = jnp.exp(jnp.minimum(cb_ - rb, 0.0))
        mm = jnp.concatenate([cbs[g] * dec_a, cbs[g] * dec_b], axis=1).astype(BF16)
        xp = xdts[j]
        rhs = jnp.concatenate([jnp.where(lo, xp, 0.0), jnp.where(lo, 0.0, xp)], axis=0).astype(BF16)
        yd = _dot(mm, rhs)
        yo = yoffs[g][:, 128 * (j % 4):128 * (j % 4 + 1)] * pair(ecs, j)
        ys.append(yd + yo + dsk_ref[:, 128 * j:128 * (j + 1)] * xs[:, 128 * j:128 * (j + 1)])
    y = jnp.concatenate(ys, axis=1) * zg
    nw = nw_ref[...]
    outs = []
    for g in range(2):
        yg = y[:, 512 * g:512 * (g + 1)]
        ms = jnp.mean(yg * yg, axis=-1, keepdims=True)
        outs.append(yg * lax.rsqrt(ms + EPS) * nw[:, 512 * g:512 * (g + 1)])
    y_ref[...] = jnp.concatenate(outs, axis=1).astype(BF16)

    gg = g_ref[...]
    u = gg[:, D_CONV:2 * D_CONV] * gg[:, 2 * D_CONV:]
    upad_ref[8:8 + L, :] = u
    sw = scw_ref[...]
    sconv = sw[2:3] * u + sw[1:2] * upad_ref[7:7 + L, :] + sw[0:1] * upad_ref[6:6 + L, :]
    upad_ref[0:8, :] = u[L - 8:, :]
    ust_ref[...] = u[L - 8:, :]
    yc_ref[...] = (gg[:, :D_CONV] * sconv).astype(BF16)


def _head_norm(x, w, bd):
    sq = x * x
    hi = sq.astype(BF16)
    lo = (sq - hi.astype(F32)).astype(BF16)
    ms = _dot(hi, bd) + _dot(lo, bd)
    return x * lax.rsqrt(ms + EPS) * w


def _rope(x, cos, sin_signed, first_half):
    partner = jnp.where(first_half, pltpu.roll(x, 96, axis=1), pltpu.roll(x, 32, axis=1))
    return x * cos + partner * sin_signed


def _seg_mean_matrix():
    r = lax.broadcasted_iota(jnp.int32, (128, 128), 0)
    c = lax.broadcasted_iota(jnp.int32, (128, 128), 1)
    return jnp.where((r & 64) == (c & 64), 1.0 / HEAD_DIM, 0.0).astype(BF16)


def _attn_body(sink_ref, q_ref, kv_ref, cos_ref, sin_ref, qw_ref, kw_ref,
               o_ref, kprev_ref, vprev_ref, *, layer, has_prev):
    L = CHUNK
    bd = _seg_mean_matrix()
    lane = lax.broadcasted_iota(jnp.int32, (L, 128), 1)
    lo = (lane & 64) == 0
    first_half = (lane & 32) == 0
    cos = cos_ref[...]
    sin = sin_ref[...]

    k_new = _rope(_head_norm(kv_ref[:, 0:128], kw_ref[...], bd), cos, sin, first_half)
    v_new = kv_ref[:, 128:256]
    kband = jnp.concatenate([kprev_ref[...], k_new], axis=0)
    vband = jnp.concatenate([vprev_ref[...], v_new], axis=0)
    kprev_ref[...] = k_new
    vprev_ref[...] = v_new
    lo2 = jnp.concatenate([lo, lo], axis=0)
    kswap = pltpu.roll(kband, 64, axis=1)
    vswap = pltpu.roll(vband, 64, axis=1)
    k2 = [jnp.where(lo2, kband, kswap).astype(BF16), jnp.where(lo2, kswap, kband).astype(BF16)]
    v2 = [jnp.where(lo2, vband, vswap).astype(BF16), jnp.where(lo2, vswap, vband).astype(BF16)]

    qi = lax.broadcasted_iota(jnp.int32, (L, 2 * L), 0)
    kj = lax.broadcasted_iota(jnp.int32, (L, 2 * L), 1)
    diff = L + qi - kj
    valid = (diff >= 0) & (diff <= WINDOW) & ((kj >= L) | has_prev)

    qt = []
    for t in range(4):
        qn = _head_norm(q_ref[:, 128 * t:128 * (t + 1)], qw_ref[...], bd)
        qt.append(_rope(qn, cos, sin, first_half) * (HEAD_DIM ** -0.5))

    otiles = []
    for g in range(KV_HEADS):
        lhs = []
        for t in (2 * g, 2 * g + 1):
            lhs.append(jnp.where(lo, qt[t], 0.0))
            lhs.append(jnp.where(lo, 0.0, qt[t]))
        s = _dot_nt(jnp.concatenate(lhs, axis=0).astype(BF16), k2[g])
        ps = []
        for i in range(4):
            sink = sink_ref[layer, 4 * g + i]
            sh = jnp.where(valid, s[L * i:L * (i + 1)], -1e30)
            mx = jnp.maximum(jnp.max(sh, axis=-1, keepdims=True), sink)
            p = jnp.exp(sh - mx)
            den = jnp.sum(p, axis=-1, keepdims=True) + jnp.exp(sink - mx)
            ps.append(p * (1.0 / den))
        og = _dot(jnp.concatenate(ps, axis=0).astype(BF16), v2[g])
        otiles.append(jnp.where(lo, og[0:L], og[L:2 * L]))
        otiles.append(jnp.where(lo, og[2 * L:3 * L], og[3 * L:4 * L]))
    o_ref[...] = jnp.concatenate(otiles, axis=1).astype(BF16)


def _mixer_kernel(sink_ref, zx_ref, g_ref, kvdt_ref, q_ref, cos_ref, sin_ref,
                  cw_ref, cb_ref, dtb_ref, alog_ref, dsk_ref, nw_ref, scw_ref, qw_ref, kw_ref, *rest,
                  layer, with_mod):
    if with_mod:
        a_ref, wada_ref, bada_ref = rest[:3]
        rest = rest[3:]
    (y_ref, yc_ref, st_out_ref, cst_ref, ust_ref, o_ref, kwin_ref, vwin_ref) = rest[:8]
    rest = rest[8:]
    if with_mod:
        modn_ref = rest[0]
        rest = rest[1:]
        modn_ref[...] = _dot(a_ref[...], wada_ref[...].astype(BF16)) + bada_ref[...]
    xpad_ref, upad_ref, st_ref, kprev_ref, vprev_ref = rest
    c = pl.program_id(1)

    @pl.when(c == 0)
    def _():
        xpad_ref[0:8, :] = jnp.zeros((8, D_XBC), F32)
        upad_ref[0:8, :] = jnp.zeros((8, D_CONV), F32)
        st_ref[...] = jnp.zeros_like(st_ref)
        kprev_ref[...] = jnp.zeros_like(kprev_ref)
        vprev_ref[...] = jnp.zeros_like(vprev_ref)

    for r in range(MIX_CHUNKS):
        rows = lambda ref: ref.at[r * CHUNK:(r + 1) * CHUNK]
        _ssd_body(rows(zx_ref), rows(g_ref), rows(kvdt_ref), cw_ref, cb_ref, dtb_ref, alog_ref, dsk_ref,
                  nw_ref, scw_ref, rows(y_ref), rows(yc_ref), cst_ref, ust_ref, xpad_ref, upad_ref, st_ref)
        _attn_body(sink_ref, rows(q_ref), rows(kvdt_ref), rows(cos_ref), rows(sin_ref), qw_ref, kw_ref,
                   rows(o_ref), kprev_ref, vprev_ref, layer=layer,
                   has_prev=(c > 0) if r == 0 else True)
    kwin_ref[...] = kprev_ref[...]
    vwin_ref[...] = vprev_ref[...]

    @pl.when(c == pl.num_programs(1) - 1)
    def _():
        st_out_ref[...] = st_ref[...].T


def _mixer_prompt(proj, nb, seq, l, sinks, cos_t, sin_t, cw, cb, dtb, alog, dsk, nw, scw, qw, kw,
                  mod_next=None):
    tr = MIX_CHUNKS * CHUNK
    nc = seq // tr
    m = nb * seq
    row = lambda b, c: (b * nc + c)
    blk = lambda width, col: pl.BlockSpec((tr, width), lambda b, c: (row(b, c), col))
    per_batch = lambda rows, width: pl.BlockSpec((None, rows, width), lambda b, c: (b, 0, 0))
    with_mod = mod_next is not None
    mod_in, mod_out, mod_shape, mod_args = [], [], [], []
    if with_mod:
        a16, w_ada, b_ada3 = mod_next
        r, d = a16.shape
        n = w_ada.shape[2]
        cw_mod = n // (nb * nc)
        assert cw_mod * nb * nc == n and cw_mod % 128 == 0
        mod_in = [pl.BlockSpec((r, d), lambda b, c: (0, 0)),
                  pl.BlockSpec((None, d, cw_mod), lambda b, c: (l + 1, 0, row(b, c))),
                  pl.BlockSpec((None, 1, cw_mod), lambda b, c: (l + 1, 0, row(b, c)))]
        mod_out = [pl.BlockSpec((r, cw_mod), lambda b, c: (0, row(b, c)))]
        mod_shape = [jax.ShapeDtypeStruct((r, n), F32)]
        mod_args = [a16, w_ada, b_ada3]
    return pl.pallas_call(
        functools.partial(_mixer_kernel, layer=l, with_mod=with_mod),
        grid=(nb, nc),
        in_specs=[
            pl.BlockSpec(memory_space=pltpu.SMEM),
            blk(2560, 0), blk(1536, 2), blk(512, 9), blk(512, 5),
            pl.BlockSpec((tr, 128), lambda b, c: (c, 0)),
            pl.BlockSpec((tr, 128), lambda b, c: (c, 0)),
            _layer_spec(cw, l), _layer_spec(cb, l), _layer_spec(dtb, l), _layer_spec(alog, l),
            _layer_spec(dsk, l), _layer_spec(nw, l), _layer_spec(scw, l), _layer_spec(qw, l), _layer_spec(kw, l),
        ] + mod_in,
        out_specs=[
            blk(D_SSD, 0), blk(D_CONV, 0),
            per_batch(D_SSD, D_STATE), per_batch(8, D_XBC), per_batch(8, D_CONV),
            blk(D_ATTN, 0), per_batch(CHUNK, 128), per_batch(CHUNK, 128),
        ] + mod_out,
        out_shape=[
            jax.ShapeDtypeStruct((m, D_SSD), BF16),
            jax.ShapeDtypeStruct((m, D_CONV), BF16),
            jax.ShapeDtypeStruct((nb, D_SSD, D_STATE), F32),
            jax.ShapeDtypeStruct((nb, 8, D_XBC), F32),
            jax.ShapeDtypeStruct((nb, 8, D_CONV), F32),
            jax.ShapeDtypeStruct((m, D_ATTN), BF16),
            jax.ShapeDtypeStruct((nb, CHUNK, 128), F32),
            jax.ShapeDtypeStruct((nb, CHUNK, 128), F32),
        ] + mod_shape,
        scratch_shapes=[
            pltpu.VMEM((8 + CHUNK, D_XBC), F32),
            pltpu.VMEM((8 + CHUNK, D_CONV), F32),
            pltpu.VMEM((D_STATE, D_SSD), F32),
            pltpu.VMEM((CHUNK, 128), F32),
            pltpu.VMEM((CHUNK, 128), F32),
        ],
        compiler_params=_cparams(("arbitrary", "arbitrary")),
        name="mixer_prompt",
    )(sinks, proj, proj, proj, proj, cos_t, sin_t, cw, cb, dtb, alog, dsk, nw, scw, qw, kw, *mod_args)


def _outproj_rows(ys_ref, o_ref, yc_ref, x_ref, gate_ref, sc_ref, sh_ref, w_ref, nw_ref, x1_ref, h2_ref):
    n = x_ref.shape[0]
    sub = min(n, 256)
    for r in range(n // sub):
        rs = slice(r * sub, (r + 1) * sub)
        gs = rs if gate_ref.shape[0] == n else slice(None)
        acc = _dot(ys_ref[rs, :], w_ref[0:D_SSD, :])
        acc += _dot(o_ref[rs, :], w_ref[D_SSD:D_SSD + D_ATTN, :])
        acc += _dot(yc_ref[rs, :], w_ref[D_SSD + D_ATTN:, :])
        x1 = x_ref[rs, :] + gate_ref[gs, :] * acc
        x1_ref[rs, :] = x1
        h2_ref[rs, :] = _rms_mod(x1, nw_ref[...], sc_ref[gs, :], sh_ref[gs, :]).astype(BF16)


def _outproj_kernel(*refs):
    p_in, s_in, (w_ref, nw_ref), p_out, s_out = refs[0:7], refs[7:14], refs[14:16], refs[16:18], refs[18:20]

    @pl.when(pl.program_id(0) == 0)
    def _():
        _outproj_rows(*s_in, w_ref, nw_ref, *s_out)

    _outproj_rows(*p_in, w_ref, nw_ref, *p_out)


def _row_specs(widths, tm):
    return [pl.BlockSpec((tm, w), lambda i: (i, 0)) for w in widths]


def _const_specs(widths, rows):
    return [pl.BlockSpec((rows, w), lambda i: (0, 0)) for w in widths]


def _outproj(mix_p, x_p, mod_p, mix_s, x_s, mod_s, w_all, l, nw2_all, tm, rpb):
    m, ms = x_p.shape[0], x_s.shape[0]
    widths = (D_SSD, D_ATTN, D_CONV, D_MODEL)
    mods = lambda mod, t, r: [_mod_spec(mod, 2, t, r), _mod_spec(mod, 4, t, r), _mod_spec(mod, 3, t, r)]
    return pl.pallas_call(
        _outproj_kernel,
        grid=(m // tm,),
        in_specs=(_row_specs(widths, tm) + mods(mod_p, tm, rpb)
                  + _const_specs(widths, ms) + mods(mod_s, 0, 1)
                  + [pl.BlockSpec((D_MODEL, D_MODEL), lambda i: (0, 0), pipeline_mode=pl.Buffered(1)),
                     _layer_spec(nw2_all, l)]),
        out_specs=_row_specs((D_MODEL, D_MODEL), tm) + _const_specs((D_MODEL, D_MODEL), ms),
        out_shape=[
            jax.ShapeDtypeStruct((m, D_MODEL), F32), jax.ShapeDtypeStruct((m, D_MODEL), BF16),
            jax.ShapeDtypeStruct((ms, D_MODEL), F32), jax.ShapeDtypeStruct((ms, D_MODEL), BF16),
        ],
        compiler_params=_cparams(("arbitrary",)),
        name="out_proj",
    )(*mix_p, x_p, mod_p, mod_p, mod_p, *mix_s, x_s, mod_s, mod_s, mod_s, w_all, nw2_all)


def _up_seq_kernel(h_ref, hs_ref, wg_ref, wu_ref, cw_ref, cb_ref, prev_ref, wd_ref,
                   a_ref, gst_ref, as_ref, gts_ref, wd16_ref, pad_ref, wg16_ref, wu16_ref, *, tm, rpb):
    i = pl.program_id(1)
    cw = cw_ref[...]
    cb = cb_ref[...]
    wd16_ref[...] = wd_ref[...].astype(BF16)

    @pl.when(i == 0)
    def _():
        wg16_ref[...] = wg_ref[...].astype(BF16)
        wu16_ref[...] = wu_ref[...].astype(BF16)
        hs = hs_ref[...]
        gt = _dot(hs, wg16_ref[...])
        u = _dot(hs, wu16_ref[...])
        conv = cw[2:3] * gt + cw[1:2] * prev_ref[1] + cw[0:1] * prev_ref[0]
        gts_ref[...] = gt
        as_ref[...] = (_silu(conv + cb) * u).astype(BF16)

    @pl.when((i * tm) % rpb == 0)
    def _():
        pad_ref[0:8, :] = jnp.zeros((8, pad_ref.shape[1]), F32)

    sub = 512
    for r in range(tm // sub):
        h = h_ref[r * sub:(r + 1) * sub, :]
        gt = _dot(h, wg16_ref[...])
        u = _dot(h, wu16_ref[...])
        base = 8 + r * sub
        pad_ref[base:base + sub, :] = gt
        conv = (cw[2:3] * gt + cw[1:2] * pad_ref[base - 1:base - 1 + sub, :]
                + cw[0:1] * pad_ref[base - 2:base - 2 + sub, :])
        a_ref[r * sub:(r + 1) * sub, :] = (_silu(conv + cb) * u).astype(BF16)
    tail = pad_ref[tm:tm + 8, :]
    pad_ref[0:8, :] = tail
    gst_ref[...] = tail


def _up_proj(h, hs, w_up_all, l, cw_all, cb_all, prev_all, w_down_all, nb, tm, tn):
    m = h.shape[0]
    ms = hs.shape[0]
    rpb = m // nb
    nt = D_FF // tn
    ni = m // tm
    slab = D_FF // (nt * ni)
    assert slab * nt * ni == D_FF and slab % 16 == 0
    return pl.pallas_call(
        functools.partial(_up_seq_kernel, tm=tm, rpb=rpb),
        grid=(nt, ni),
        in_specs=[
            pl.BlockSpec((tm, D_MODEL), lambda j, i: (i, 0)),
            pl.BlockSpec((ms, D_MODEL), lambda j, i: (0, 0)),
            pl.BlockSpec((None, D_MODEL, tn), lambda j, i: (l, 0, j)),
            pl.BlockSpec((None, D_MODEL, tn), lambda j, i: (l, 0, j + nt)),
            pl.BlockSpec((None, 3, tn), lambda j, i: (l, 0, j)),
            pl.BlockSpec((None, 1, tn), lambda j, i: (l, 0, j)),
            pl.BlockSpec((None, 2, ms, tn), lambda j, i: (l, 0, 0, j)),
            pl.BlockSpec((None, slab, D_MODEL), lambda j, i: (l, j * ni + i, 0)),
        ],
        out_specs=[
            pl.BlockSpec((tm, tn), lambda j, i: (i, j)),
            pl.BlockSpec((None, 8, tn), lambda j, i: ((i * tm) // rpb, 0, j)),
            pl.BlockSpec((ms, tn), lambda j, i: (0, j)),
            pl.BlockSpec((ms, tn), lambda j, i: (0, j)),
            pl.BlockSpec((slab, D_MODEL), lambda j, i: (j * ni + i, 0)),
        ],
        out_shape=[
            jax.ShapeDtypeStruct((m, D_FF), BF16),
            jax.ShapeDtypeStruct((nb, 8, D_FF), F32),
            jax.ShapeDtypeStruct((ms, D_FF), BF16),
            jax.ShapeDtypeStruct((ms, D_FF), F32),
            jax.ShapeDtypeStruct((D_FF, D_MODEL), BF16),
        ],
        scratch_shapes=[
            pltpu.VMEM((8 + tm, tn), F32),
            pltpu.VMEM((D_MODEL, tn), BF16),
            pltpu.VMEM((D_MODEL, tn), BF16),
        ],
        compiler_params=_cparams(("arbitrary", "arbitrary")),
        name="up_proj",
    )(h, hs, w_up_all, w_up_all, cw_all, cb_all, prev_all, w_down_all)


def _down_rows(a_ref, x_ref, gate_ref, mods, w_ref, nw_ref, x2_ref, hn_ref):
    x2 = x_ref[...] + gate_ref[...] * _dot(a_ref[...], w_ref[...])
    x2_ref[...] = x2
    if hn_ref is not None:
        hn_ref[...] = _rms_mod(x2, nw_ref[...], mods[0][...], mods[1][...]).astype(BF16)


def _down_kernel(*refs, with_norm):
    refs = list(refs)
    n_in = 5 if with_norm else 3
    p_in, s_in = refs[0:n_in], refs[n_in:2 * n_in]
    w_ref = refs[2 * n_in]
    nw_ref = refs[2 * n_in + 1] if with_norm else None
    outs = refs[2 * n_in + (2 if with_norm else 1):]
    p_out = (outs[0], outs[1]) if with_norm else (outs[0], None)
    s_out = (outs[2], outs[3]) if with_norm else (outs[1], None)

    @pl.when(pl.program_id(0) == 0)
    def _():
        _down_rows(s_in[0], s_in[1], s_in[2], s_in[3:], w_ref, nw_ref, *s_out)

    _down_rows(p_in[0], p_in[1], p_in[2], p_in[3:], w_ref, nw_ref, *p_out)


def _down(a_p, x_p, mod_p, a_s, x_s, mod_s, w_all, l, nw1_all, tm, rpb):
    m, ms = x_p.shape[0], x_s.shape[0]
    with_norm = l + 1 < nw1_all.shape[0]

    def mods(mod, t, r):
        sp = [_mod_spec(mod[0], 5, t, r)]
        if with_norm:
            sp += [_mod_spec(mod[1], 1, t, r), _mod_spec(mod[1], 0, t, r)]
        return sp

    def mod_args(mod):
        return [mod[0], mod[1], mod[1]] if with_norm else [mod[0]]

    in_specs = (_row_specs((D_FF, D_MODEL), tm) + mods(mod_p, tm, rpb)
                + _const_specs((D_FF, D_MODEL), ms) + mods(mod_s, 0, 1)
                + [pl.BlockSpec((D_FF, D_MODEL), lambda i: (0, 0), pipeline_mode=pl.Buffered(1))])
    args = [a_p, x_p] + mod_args(mod_p) + [a_s, x_s] + mod_args(mod_s) + [w_all]
    out_w = (D_MODEL, D_MODEL) if with_norm else (D_MODEL,)
    out_dt = (F32, BF16) if with_norm else (F32,)
    if with_norm:
        in_specs.append(_layer_spec(nw1_all, l + 1))
        args.append(nw1_all)
    res = pl.pallas_call(
        functools.partial(_down_kernel, with_norm=with_norm),
        grid=(m // tm,),
        in_specs=in_specs,
        out_specs=_row_specs(out_w, tm) + _const_specs(out_w, ms),
        out_shape=([jax.ShapeDtypeStruct((m, D_MODEL), d) for d in out_dt]
                   + [jax.ShapeDtypeStruct((ms, D_MODEL), d) for d in out_dt]),
        compiler_params=_cparams(("arbitrary",)),
        name="down_proj",
    )(*args)
    if with_norm:
        return res[0], res[1], res[2], res[3]
    return res[0], None, res[1], None


def _expand_matrix():
    r = lax.broadcasted_iota(jnp.int32, (128, D_SSD), 0)
    c = lax.broadcasted_iota(jnp.int32, (128, D_SSD), 1)
    return jnp.where(r == (c >> 6), 1.0, 0.0).astype(BF16)


def _ssd_step_kernel(p_ref, st_ref, cprev_ref, sprev_ref, cw_ref, cb_ref, dtb_ref, alog_ref, dsk_ref, nw_ref,
                     scw_ref, y_ref, yc_ref, u_ref, stn_ref):
    nbb = p_ref.shape[0]
    z = p_ref[:, C_Z:C_Z + D_SSD]
    raw = p_ref[:, C_XBC:C_XBC + D_XBC]
    cw = cw_ref[...]
    conv = cw[3:4] * raw + cw[2:3] * cprev_ref[2] + cw[1:2] * cprev_ref[1] + cw[0:1] * cprev_ref[0]
    xbc = _silu(conv + cb_ref[...])
    xs = xbc[:, :D_SSD]
    bm = xbc[:, D_SSD:D_SSD + 256]
    cm = xbc[:, D_SSD + 256:]

    lane = lax.broadcasted_iota(jnp.int32, (1, 128), 1)
    a_row = jnp.where(lane < SSM_HEADS, -jnp.exp(alog_ref[...]), 0.0)
    dtr = p_ref[:, C_DT:C_DT + 128] + dtb_ref[...]
    dt = jnp.maximum(dtr, 0.0) + jnp.log1p(jnp.exp(-jnp.abs(dtr)))
    dec = jnp.exp(dt * a_row)
    em = _expand_matrix()

    def expand(v):
        v1, v2, v3 = _split3(v)
        return _dot(v1, em) + _dot(v2, em) + _dot(v3, em)

    xdt_t = (xs * expand(dt)).T
    dec_t = expand(dec).T
    rowid = lax.broadcasted_iota(jnp.int32, (nbb, 512), 0)
    ysum = [jnp.zeros((nbb, 512), F32), jnp.zeros((nbb, 512), F32)]
    for i in range(nbb):
        for g in range(2):
            rs = slice(512 * g, 512 * (g + 1))
            hn = st_ref[i, rs, :] * dec_t[rs, i:i + 1] + xdt_t[rs, i:i + 1] * bm[i:i + 1, 128 * g:128 * (g + 1)]
            stn_ref[i, rs, :] = hn
            r = _dot_nt(cm[:, 128 * g:128 * (g + 1)].astype(BF16), hn.astype(BF16))
            ysum[g] = ysum[g] + jnp.where(rowid == i, r, 0.0)
    y = jnp.concatenate(ysum, axis=1) + dsk_ref[...] * xs
    y = y * _silu(z)
    nw = nw_ref[...]
    outs = []
    for g in range(2):
        yg = y[:, 512 * g:512 * (g + 1)]
        ms = jnp.mean(yg * yg, axis=-1, keepdims=True)
        outs.append(yg * lax.rsqrt(ms + EPS) * nw[:, 512 * g:512 * (g + 1)])
    y_ref[...] = jnp.concatenate(outs, axis=1).astype(BF16)

    u = p_ref[:, C_G + D_CONV:C_G + 2 * D_CONV] * p_ref[:, C_G + 2 * D_CONV:C_G + 3 * D_CONV]
    sw = scw_ref[...]
    sconv = sw[2:3] * u + sw[1:2] * sprev_ref[1] + sw[0:1] * sprev_ref[0]
    u_ref[...] = u
    yc_ref[...] = (p_ref[:, C_G:C_G + D_CONV] * sconv).astype(BF16)


def _ssd_step(proj, state_all, cprev_all, sprev_all, l, cw, cb, dtb, alog, dsk, nw, scw, nbb=8):
    nb = proj.shape[0]
    return pl.pallas_call(
        _ssd_step_kernel,
        grid=(nb // nbb,),
        in_specs=[
            pl.BlockSpec((nbb, PROJ_P), lambda i: (i, 0)),
            pl.BlockSpec((None, nbb, D_SSD, D_STATE), lambda i: (l, i, 0, 0)),
            pl.BlockSpec((None, 3, nbb, D_XBC), lambda i: (l, 0, i, 0)),
            pl.BlockSpec((None, 2, nbb, D_CONV), lambda i: (l, 0, i, 0)),
            _layer_spec(cw, l), _layer_spec(cb, l), _layer_spec(dtb, l), _layer_spec(alog, l),
            _layer_spec(dsk, l), _layer_spec(nw, l), _layer_spec(scw, l),
        ],
        out_specs=[
            pl.BlockSpec((nbb, D_SSD), lambda i: (i, 0)),
            pl.BlockSpec((nbb, D_CONV), lambda i: (i, 0)),
            pl.BlockSpec((nbb, D_CONV), lambda i: (i, 0)),
            pl.BlockSpec((nbb, D_SSD, D_STATE), lambda i: (i, 0, 0)),
        ],
        out_shape=[
            jax.ShapeDtypeStruct((nb, D_SSD), BF16),
            jax.ShapeDtypeStruct((nb, D_CONV), BF16),
            jax.ShapeDtypeStruct((nb, D_CONV), F32),
            jax.ShapeDtypeStruct((nb, D_SSD, D_STATE), F32),
        ],
        compiler_params=_cparams(("arbitrary",)),
        name="ssd_step",
    )(proj, state_all, cprev_all, sprev_all, cw, cb, dtb, alog, dsk, nw, scw)


def _qk_step_kernel(p_ref, cos_ref, sin_ref, qw_ref, kw_ref, q_ref, k_ref):
    bd = _seg_mean_matrix()
    lane = lax.broadcasted_iota(jnp.int32, (p_ref.shape[0], 128), 1)
    first_half = (lane & 32) == 0
    cos = cos_ref[...]
    sin = sin_ref[...]
    for t in range(4):
        qn = _head_norm(p_ref[:, C_Q + 128 * t:C_Q + 128 * (t + 1)], qw_ref[...], bd)
        q_ref[:, 128 * t:128 * (t + 1)] = _rope(qn, cos, sin, first_half) * (HEAD_DIM ** -0.5)
    k_ref[...] = _rope(_head_norm(p_ref[:, C_K:C_K + 128], kw_ref[...], bd), cos, sin, first_half)


def _qk_step(proj, cos_r, sin_r, l, qw, kw):
    nb = proj.shape[0]
    full = lambda a: pl.BlockSpec(a.shape, lambda i: (0,) * a.ndim)
    return pl.pallas_call(
        _qk_step_kernel,
        grid=(1,),
        in_specs=[full(proj), full(cos_r), full(sin_r), _layer_spec(qw, l), _layer_spec(kw, l)],
        out_specs=[pl.BlockSpec((nb, D_ATTN), lambda i: (0, 0)), pl.BlockSpec((nb, 128), lambda i: (0, 0))],
        out_shape=[jax.ShapeDtypeStruct((nb, D_ATTN), F32), jax.ShapeDtypeStruct((nb, 128), F32)],
        name="qk_step",
    )(proj, cos_r, sin_r, qw, kw)


def _attn_step_kernel(q_ref, kc_ref, vc_ref, knew_ref, vnew_ref, sink_ref, o_ref):
    q = q_ref[...]
    s = lax.dot_general(q.astype(BF16), kc_ref[...].astype(BF16), (((2,), (1,)), ((0,), (0,))),
                        preferred_element_type=F32)
    s_new = jnp.sum(q * knew_ref[...], axis=-1, keepdims=True)
    sink = sink_ref[...]
    mx = jnp.maximum(jnp.maximum(jnp.max(s, axis=-1, keepdims=True), s_new), sink)
    p = jnp.exp(s - mx)
    p_new = jnp.exp(s_new - mx)
    den = jnp.sum(p, axis=-1, keepdims=True) + p_new + jnp.exp(sink - mx)
    o = lax.dot_general((p / den).astype(BF16), vc_ref[...].astype(BF16), (((2,), (2,)), ((0,), (0,))),
                        preferred_element_type=F32)
    o_ref[...] = o + (p_new / den) * vnew_ref[...]


def _attn_step(q3, kc_all, vc_all, l, knew, vnew, sink_all):
    nb = q3.shape[0]
    full = lambda a: pl.BlockSpec(a.shape, lambda i: (0,) * a.ndim)
    return pl.pallas_call(
        _attn_step_kernel,
        grid=(1,),
        in_specs=[full(q3), _layer_spec(kc_all, l), _layer_spec(vc_all, l), full(knew), full(vnew),
                  _layer_spec(sink_all, l)],
        out_specs=pl.BlockSpec((nb, ATT_HEADS, 128), lambda i: (0, 0, 0)),
        out_shape=jax.ShapeDtypeStruct((nb, ATT_HEADS, 128), F32),
        name="attn_step",
    )(q3, kc_all, vc_all, knew, vnew, sink_all)


def _rope_tables(pos):
    half = HEAD_DIM // 2
    inv = np.power(ROPE_THETA, -np.arange(half, dtype=np.float64) / half)
    ang = np.asarray(pos, np.float64)[:, None] * inv[None, :]
    cos = np.cos(ang)
    sin = np.sin(ang)
    cos_t = np.concatenate([cos, cos, cos, cos], axis=-1).astype(np.float32)
    sin_t = np.concatenate([-sin, sin, -sin, sin], axis=-1).astype(np.float32)
    return jnp.asarray(cos_t), jnp.asarray(sin_t)


def kernel(x_prompt, x_sample, state_ssm, state_ssm_conv, cache_win_k, cache_win_v, state_short_conv,
           state_ffn_conv, c_prompt, c_sample, norm1_w, norm2_w, w_ada, b_ada, w_in, ssm_conv_w, ssm_conv_b,
           dt_bias, a_log, d_skip, ssm_norm_w, q_norm_w, k_norm_w, sinks, sconv_w, w_out, w_up, ffn_conv_w,
           ffn_conv_b, w_down):
    nbp, seq, _ = x_prompt.shape
    nbs = x_sample.shape[0]
    depth = w_in.shape[0]
    mp = nbp * seq
    nbuf = cache_win_k.shape[2]

    in_tab = _in_proj_table()
    w_in_t = jnp.swapaxes(w_in, 1, 2)

    n1 = norm1_w.reshape(depth, 1, D_MODEL)
    n2 = norm2_w.reshape(depth, 1, D_MODEL)
    cb = ssm_conv_b.reshape(depth, 1, D_XBC)
    dtb = jnp.pad(dt_bias, ((0, 0), (0, 128 - SSM_HEADS))).reshape(depth, 1, 128)
    alog = jnp.pad(a_log, ((0, 0), (0, 128 - SSM_HEADS))).reshape(depth, 1, 128)
    dsk = jnp.repeat(d_skip, SSM_HEAD_DIM, axis=1).reshape(depth, 1, D_SSD)
    nw = ssm_norm_w.reshape(depth, 1, D_SSD)
    qw = jnp.tile(q_norm_w, (1, 2)).reshape(depth, 1, 128)
    kw = jnp.tile(k_norm_w, (1, 2)).reshape(depth, 1, 128)
    fcb = ffn_conv_b.reshape(depth, 1, D_FF)
    sink3 = sinks.reshape(depth, ATT_HEADS, 1)

    st_all = state_ssm.reshape(depth, nbs, D_SSD, D_STATE)
    cprev = jnp.transpose(state_ssm_conv, (0, 2, 1, 3))
    sprev = jnp.transpose(state_short_conv, (0, 2, 1, 3))
    fprev = jnp.transpose(state_ffn_conv, (0, 2, 1, 3))
    ck = cache_win_k.reshape(depth, nbs, nbuf, 128)
    cv = cache_win_v.reshape(depth, nbs, nbuf, 128)
    ck_t = jnp.swapaxes(ck, 2, 3)
    cv_t = jnp.swapaxes(cv, 2, 3)

    c_all = jnp.concatenate([c_prompt, jnp.zeros((8 - nbp, D_MODEL), F32), c_sample], axis=0)
    b_ada3 = b_ada.reshape(depth, 1, 6 * D_MODEL)
    mod0, a16 = _modulation_first(c_all, w_ada, b_ada3)

    def split_mod(mod):
        return mod[:nbp].reshape(nbp, 1, 6 * D_MODEL), mod[8:].reshape(1, nbs, 6 * D_MODEL)

    mod_p, mod_s = [None] * (depth + 1), [None] * (depth + 1)
    mod_p[0], mod_s[0] = split_mod(mod0)

    cos_p, sin_p = _rope_tables(np.arange(seq))
    cos_s, sin_s = _rope_tables(np.full((1,), PAST_LEN))

    head_g = jnp.arange(ATT_HEADS) // (ATT_HEADS // KV_HEADS)
    q_place = (head_g[:, None] == jnp.arange(KV_HEADS)[None, :]).astype(F32)[None, :, :, None]

    xp = x_prompt.reshape(mp, D_MODEL)
    xs = x_sample.reshape(nbs, D_MODEL)
    TM = 512
    hp = _norm_mod(xp, n1, mod_p[0], 0, 1, 0, TM, seq)
    hs = _norm_mod(xs, n1, mod_s[0], 0, 1, 0, nbs, nbs)

    outs_p = [[] for _ in range(6)]
    outs_s = [[] for _ in range(6)]
    for l in range(depth):
        proj, projs, w_out_l = _in_proj(hp, hs, w_in_t, l, in_tab, w_out, min(1024, seq))

        mixed = _mixer_prompt(
            proj, nbp, seq, l, sinks, cos_p, sin_p, ssm_conv_w, cb, dtb, alog, dsk, nw, sconv_w, qw, kw,
            mod_next=(a16, w_ada, b_ada3) if l + 1 < depth else None)
        y_ssm, y_conv, st_t, cst, ust, o, kwin, vwin = mixed[:8]
        if l + 1 < depth:
            mod_p[l + 1], mod_s[l + 1] = split_mod(mixed[8])

        y_s, yc_s, u_s, st_new = _ssd_step(projs, st_all, cprev, sprev, l, ssm_conv_w, cb, dtb, alog, dsk, nw,
                                           sconv_w)
        q_r, k_r = _qk_step(projs, cos_s, sin_s, l, qw, kw)
        v_r = projs[:, C_V:C_V + 128]
        q3 = (q_r.reshape(nbs, ATT_HEADS, 1, HEAD_DIM) * q_place).reshape(nbs, ATT_HEADS, 128)
        o3 = _attn_step(q3, ck_t, cv_t, l, k_r[:, None, :], v_r[:, None, :], sink3)
        o4 = o3.reshape(nbs, ATT_HEADS, KV_HEADS, HEAD_DIM)
        o_s = jnp.concatenate([o4[:, :4, 0], o4[:, 4:, 1]], axis=1).reshape(nbs, D_ATTN).astype(BF16)

        xp, h2, xs, h2s = _outproj((y_ssm, o, y_conv), xp, mod_p[l], (y_s, o_s, yc_s), xs, mod_s[l], w_out_l,
                                   l, n2, TM, seq)
        act, gst, act_s, gt_s, w_down_l = _up_proj(h2, h2s, w_up, l, ffn_conv_w, fcb, fprev, w_down, nbp,
                                                   min(2048, seq), 512)
        xp, hp, xs, hs = _down(act, xp, mod_p[l:l + 2], act_s, xs, mod_s[l:l + 2], w_down_l, l, n1, 256, seq)

        outs_p[0].append(st_t.reshape(nbp, SSM_HEADS, SSM_HEAD_DIM, D_STATE))
        outs_p[1].append(cst[:, 8 - (SSM_CONV - 1):])
        outs_p[2].append(kwin.reshape(nbp, WINDOW, KV_HEADS, HEAD_DIM))
        outs_p[3].append(vwin.reshape(nbp, WINDOW, KV_HEADS, HEAD_DIM))
        outs_p[4].append(ust[:, 6:])
        outs_p[5].append(gst[:, 6:])

        outs_s[0].append(st_new.reshape(nbs, SSM_HEADS, SSM_HEAD_DIM, D_STATE))
        outs_s[1].append(projs[:, C_XBC:C_XBC + D_XBC])
        outs_s[2].append(k_r)
        outs_s[3].append(v_r)
        outs_s[4].append(u_s)
        outs_s[5].append(gt_s)

    res_p = [jnp.stack(a, axis=0) for a in outs_p]
    new_rows = [jnp.stack(a, axis=0) for a in outs_s]
    s_ssm = new_rows[0]
    s_ssm_conv = jnp.concatenate([state_ssm_conv[:, :, 1:], new_rows[1][:, :, None]], axis=2)
    s_win_k = jnp.concatenate([ck[:, :, 1:], new_rows[2][:, :, None]], axis=2).reshape(cache_win_k.shape)
    s_win_v = jnp.concatenate([cv[:, :, 1:], new_rows[3][:, :, None]], axis=2).reshape(cache_win_v.shape)
    s_short = jnp.concatenate([state_short_conv[:, :, 1:], new_rows[4][:, :, None]], axis=2)
    s_ffn = jnp.concatenate([state_ffn_conv[:, :, 1:], new_rows[5][:, :, None]], axis=2)
    return (xp.reshape(nbp, seq, D_MODEL), xs.reshape(nbs, 1, D_MODEL), *res_p,
            s_ssm, s_ssm_conv, s_win_k, s_win_v, s_short, s_ffn)
```

```python
import functools

import jax
import jax.numpy as jnp
import numpy as np
from jax import lax
from jax.experimental import pallas as pl
from jax.experimental.pallas import tpu as pltpu

F32 = jnp.float32
BF16 = jnp.bfloat16

D_MODEL = 2048
D_SSD = 1024
SSM_HEADS = 16
SSM_HEAD_DIM = 64
D_STATE = 128
SSM_CONV = 4
CHUNK = 128
D_XBC = 1536
HEAD_DIM = 64
D_ATTN = 512
ATT_HEADS = 8
KV_HEADS = 2
WINDOW = 128
ROPE_THETA = 10000.0
D_CONV = 512
D_FF = 5632
PROJ_W = 4880
EPS = 1e-6
PAST_LEN = 16384

C_Z, C_XBC, C_Q, C_G, C_K, C_V, C_DT, PROJ_P = 0, 1024, 2560, 3072, 4608, 4736, 4864, 5120

VMEM_LIMIT = 56 * 1024 * 1024


def _cparams(sem):
    return pltpu.CompilerParams(dimension_semantics=sem, vmem_limit_bytes=VMEM_LIMIT)


def _silu(x):
    return x * jax.nn.sigmoid(x)


def _rms_mod(x, w, scale, shift):
    ms = jnp.mean(x * x, axis=-1, keepdims=True)
    return (x * lax.rsqrt(ms + EPS) * w) * (1.0 + scale) + shift


def _split3(x):
    h1 = x.astype(BF16)
    r1 = x - h1.astype(F32)
    h2 = r1.astype(BF16)
    r2 = r1 - h2.astype(F32)
    return h1, h2, r2.astype(BF16)


def _dot(a, b):
    return jnp.dot(a, b, preferred_element_type=F32)


def _dot_nt(a, b):
    return lax.dot_general(a, b, (((1,), (1,)), ((), ())), preferred_element_type=F32)


def _layer_spec(arr, l):
    shape = arr.shape[1:]
    return pl.BlockSpec((None,) + shape, lambda *_: (l,) + (0,) * len(shape))


def _mod_kernel(c_ref, w_ref, b_ref, o_ref, a_ref):
    a = _silu(c_ref[...]).astype(BF16)
    a_ref[...] = a
    o_ref[...] = _dot(a, w_ref[...].astype(BF16)) + b_ref[...]


def _modulation_first(c_all, w_ada, b_ada3, tn=1024):
    _, d, n = w_ada.shape
    r = c_all.shape[0]
    return pl.pallas_call(
        _mod_kernel,
        grid=(n // tn,),
        in_specs=[
            pl.BlockSpec((r, d), lambda j: (0, 0)),
            pl.BlockSpec((None, d, tn), lambda j: (0, 0, j)),
            pl.BlockSpec((None, 1, tn), lambda j: (0, 0, j)),
        ],
        out_specs=[pl.BlockSpec((r, tn), lambda j: (0, j)), pl.BlockSpec((r, d), lambda j: (0, 0))],
        out_shape=[jax.ShapeDtypeStruct((r, n), F32), jax.ShapeDtypeStruct((r, d), BF16)],
        compiler_params=_cparams(("arbitrary",)),
        name="modulation",
    )(c_all, w_ada, b_ada3)


def _norm_kernel(x_ref, w_ref, sc_ref, sh_ref, o_ref):
    o_ref[...] = _rms_mod(x_ref[...], w_ref[...], sc_ref[...], sh_ref[...]).astype(BF16)


def _mod_spec(mod, j, tm, rpb):
    r = mod.shape[1]
    return pl.BlockSpec((None, r, D_MODEL), lambda i: ((i * tm) // rpb, 0, j))


def _norm_mod(x, w_all, mod, l, j_scale, j_shift, tm, rpb):
    m = x.shape[0]
    return pl.pallas_call(
        _norm_kernel,
        grid=(m // tm,),
        in_specs=[
            pl.BlockSpec((tm, D_MODEL), lambda i: (i, 0)),
            _layer_spec(w_all, l),
            _mod_spec(mod, j_scale, tm, rpb),
            _mod_spec(mod, j_shift, tm, rpb),
        ],
        out_specs=pl.BlockSpec((tm, D_MODEL), lambda i: (i, 0)),
        out_shape=jax.ShapeDtypeStruct((m, D_MODEL), BF16),
        compiler_params=_cparams(("arbitrary",)),
        name="norm_mod",
    )(x, w_all, mod, mod)


MIX_CHUNKS = 2
IN_TN = 1024
IN_SRC = 256
IN_SUB = 256
IN_ROW_UNIT = 16


def _in_proj_table():
    segs = [(C_Z, C_Q, 0), (C_Q, C_G, 2576), (C_G, C_K, 3344), (C_K, C_DT, 3088), (C_DT, PROJ_P, 2560)]
    rows = []
    for st in range(PROJ_P // IN_SRC):
        c = st * IN_SRC
        lo, _, src = [g for g in segs if g[0] <= c < g[1]][0]
        rows.append((src + c - lo) // IN_ROW_UNIT)
    return jnp.array(rows, jnp.int32)


def _in_proj_kernel(tab_ref, h_ref, hs_ref, a0_ref, a1_ref, a2_ref, a3_ref, wo_ref,
                    o_ref, os_ref, wo16_ref, w16_ref, *, tm):
    del tab_ref
    j = pl.program_id(0)
    i = pl.program_id(1)
    nsub = tm // IN_SUB
    wo16_ref[...] = wo_ref[...].astype(BF16)

    @pl.when(i == 0)
    def _():
        for s, a_ref in enumerate((a0_ref, a1_ref, a2_ref, a3_ref)):
            w16_ref[:, IN_SRC * s:IN_SRC * (s + 1)] = a_ref[0].T.astype(BF16)
        os_ref[...] = _dot(hs_ref[...], w16_ref[...])

    def rows(r):
        return slice(r * IN_SUB, (r + 1) * IN_SUB)

    def tile(r):
        return _dot(h_ref[rows(r), :], w16_ref[...])

    @pl.when(j == 0)
    def _():
        for r in range(nsub):
            o_ref[rows(r), :] = _silu(tile(r))

    @pl.when(j > 0)
    def _():
        for r in range(nsub):
            o_ref[rows(r), :] = tile(r)


def _in_proj(h, hs, w_in_t, l, tab, w_out_all, tm):
    m, k = h.shape
    ms = hs.shape[0]
    nsrc = IN_TN // IN_SRC
    ni = m // tm
    nsteps = (PROJ_P // IN_TN) * ni
    nslab = 1 << (nsteps.bit_length() - 1)
    slab = D_MODEL // nslab
    a_spec = lambda s: pl.BlockSpec((pl.Element(1), pl.Element(IN_SRC), pl.Element(k)),
                                    lambda j, i, t: (l, t[nsrc * j + s] * IN_ROW_UNIT, 0))
    slab_idx = lambda j, i: jnp.minimum(j * ni + i, nslab - 1)
    return pl.pallas_call(
        functools.partial(_in_proj_kernel, tm=tm),
        grid_spec=pltpu.PrefetchScalarGridSpec(
            num_scalar_prefetch=1,
            grid=(PROJ_P // IN_TN, ni),
            in_specs=[
                pl.BlockSpec((tm, k), lambda j, i, t: (i, 0)),
                pl.BlockSpec((ms, k), lambda j, i, t: (0, 0)),
                a_spec(0), a_spec(1), a_spec(2), a_spec(3),
                pl.BlockSpec((None, slab, D_MODEL), lambda j, i, t: (l, slab_idx(j, i), 0)),
            ],
            out_specs=[
                pl.BlockSpec((tm, IN_TN), lambda j, i, t: (i, j)),
                pl.BlockSpec((ms, IN_TN), lambda j, i, t: (0, j)),
                pl.BlockSpec((slab, D_MODEL), lambda j, i, t: (slab_idx(j, i), 0)),
            ],
            scratch_shapes=[pltpu.VMEM((k, IN_TN), BF16)],
        ),
        out_shape=[jax.ShapeDtypeStruct((m, PROJ_P), F32), jax.ShapeDtypeStruct((ms, PROJ_P), F32),
                   jax.ShapeDtypeStruct((D_MODEL, D_MODEL), BF16)],
        compiler_params=_cparams(("arbitrary", "arbitrary")),
        name="in_proj",
    )(tab, h, hs, w_in_t, w_in_t, w_in_t, w_in_t, w_out_all)


def _ssd_body(zx_ref, g_ref, kvdt_ref, cw_ref, cb_ref, dtb_ref, alog_ref, dsk_ref, nw_ref, scw_ref,
              y_ref, yc_ref, cst_ref, ust_ref, xpad_ref, upad_ref, st_ref):
    L = CHUNK
    zg = zx_ref[:, :D_SSD]
    raw = zx_ref[:, D_SSD:]

    xpad_ref[8:8 + L, :] = raw
    cw = cw_ref[...]
    conv = (cw[3:4] * raw + cw[2:3] * xpad_ref[7:7 + L, :] + cw[1:2] * xpad_ref[6:6 + L, :]
            + cw[0:1] * xpad_ref[5:5 + L, :])
    xpad_ref[0:8, :] = raw[L - 8:, :]
    cst_ref[...] = raw[L - 8:, :]
    xbc = _silu(conv + cb_ref[...])
    xs = xbc[:, :D_SSD]

    lane = lax.broadcasted_iota(jnp.int32, (L, 128), 1)
    row = lax.broadcasted_iota(jnp.int32, (L, 128), 0)
    lo = (lane & 64) == 0
    causal = row >= lane

    head_ok = lane[0:1] < SSM_HEADS
    a_row = jnp.where(head_ok, -jnp.exp(alog_ref[...]), 0.0)
    dtr = kvdt_ref[:, 256:384] + dtb_ref[...]
    dt = jnp.maximum(dtr, 0.0) + jnp.log1p(jnp.exp(-jnp.abs(dtr)))
    da = dt * a_row

    tril = jnp.where(causal, 1.0, 0.0).astype(BF16)
    d1, d2, d3 = _split3(da)
    cs = _dot(tril, d1) + _dot(tril, d2) + _dot(tril, d3)
    cs_t = cs.T
    ecs = jnp.exp(cs)
    te = jnp.exp(cs[L - 1:L, :] - cs)
    ecl = ecs[L - 1:L, :]

    def pair(arr, j):
        rows = arr.shape[0]
        return jnp.where(lo[:rows], arr[:, 2 * j:2 * j + 1], arr[:, 2 * j + 1:2 * j + 2])

    cbs, yoffs, updates = [], [], []
    xdts = []
    for j in range(8):
        xdts.append(xs[:, 128 * j:128 * (j + 1)] * pair(dt, j))
    for g in range(2):
        bm = xbc[:, D_SSD + 128 * g:D_SSD + 128 * (g + 1)]
        cm = xbc[:, D_SSD + 256 + 128 * g:D_SSD + 256 + 128 * (g + 1)]
        cm16 = cm.astype(BF16)
        cbs.append(jnp.where(causal, _dot_nt(cm16, bm.astype(BF16)), 0.0))
        yoffs.append(_dot(cm16, st_ref[:, 512 * g:512 * (g + 1)].astype(BF16)))
        xe = jnp.concatenate([xdts[4 * g + i] * pair(te, 4 * g + i) for i in range(4)], axis=1)
        s_new = _dot(bm.T.astype(BF16), xe.astype(BF16))
        cd = jnp.concatenate([pair(ecl, 4 * g + i) for i in range(4)], axis=1)
        updates.append((cd, s_new))
    for g in range(2):
        cd, s_new = updates[g]
        st_ref[:, 512 * g:512 * (g + 1)] = cd * st_ref[:, 512 * g:512 * (g + 1)] + s_new

    ys = []
    for j in range(8):
        g = j // 4
        ca, cb_ = cs[:, 2 * j:2 * j + 1], cs[:, 2 * j + 1:2 * j + 2]
        ra, rb = cs_t[2 * j:2 * j + 1, :], cs_t[2 * j + 1:2 * j + 2, :]
        dec_a = jnp.exp(jnp.minimum(ca - ra, 0.0))
        dec_b = jnp.exp(jnp.minimum(cb_ - rb, 0.0))
        mm = jnp.concatenate([cbs[g] * dec_a, cbs[g] * dec_b], axis=1).astype(BF16)
        xp = xdts[j]
        rhs = jnp.concatenate([jnp.where(lo, xp, 0.0), jnp.where(lo, 0.0, xp)], axis=0).astype(BF16)
        yd = _dot(mm, rhs)
        yo = yoffs[g][:, 128 * (j % 4):128 * (j % 4 + 1)] * pair(ecs, j)
        ys.append(yd + yo + dsk_ref[:, 128 * j:128 * (j + 1)] * xs[:, 128 * j:128 * (j + 1)])
    y = jnp.concatenate(ys, axis=1) * zg
    nw = nw_ref[...]
    outs = []
    for g in range(2):
        yg = y[:, 512 * g:512 * (g + 1)]
        ms = jnp.mean(yg * yg, axis=-1, keepdims=True)
        outs.append(yg * lax.rsqrt(ms + EPS) * nw[:, 512 * g:512 * (g + 1)])
    y_ref[...] = jnp.concatenate(outs, axis=1).astype(BF16)

    gg = g_ref[...]
    u = gg[:, D_CONV:2 * D_CONV] * gg[:, 2 * D_CONV:]
    upad_ref[8:8 + L, :] = u
    sw = scw_ref[...]
    sconv = sw[2:3] * u + sw[1:2] * upad_ref[7:7 + L, :] + sw[0:1] * upad_ref[6:6 + L, :]
    upad_ref[0:8, :] = u[L - 8:, :]
    ust_ref[...] = u[L - 8:, :]
    yc_ref[...] = (gg[:, :D_CONV] * sconv).astype(BF16)


def _head_norm(x, w, bd):
    sq = x * x
    hi = sq.astype(BF16)
    lo = (sq - hi.astype(F32)).astype(BF16)
    ms = _dot(hi, bd) + _dot(lo, bd)
    return x * lax.rsqrt(ms + EPS) * w


def _rope(x, cos, sin_signed, first_half):
    partner = jnp.where(first_half, pltpu.roll(x, 96, axis=1), pltpu.roll(x, 32, axis=1))
    return x * cos + partner * sin_signed


def _seg_mean_matrix():
    r = lax.broadcasted_iota(jnp.int32, (128, 128), 0)
    c = lax.broadcasted_iota(jnp.int32, (128, 128), 1)
    return jnp.where((r & 64) == (c & 64), 1.0 / HEAD_DIM, 0.0).astype(BF16)


def _attn_body(sink_ref, q_ref, kv_ref, cos_ref, sin_ref, qw_ref, kw_ref,
               o_ref, kprev_ref, vprev_ref, *, layer, has_prev):
    L = CHUNK
    bd = _seg_mean_matrix()
    lane = lax.broadcasted_iota(jnp.int32, (L, 128), 1)
    lo = (lane & 64) == 0
    first_half = (lane & 32) == 0
    cos = cos_ref[...]
    sin = sin_ref[...]

    k_new = _rope(_head_norm(kv_ref[:, 0:128], kw_ref[...], bd), cos, sin, first_half)
    v_new = kv_ref[:, 128:256]
    kband = jnp.concatenate([kprev_ref[...], k_new], axis=0)
    vband = jnp.concatenate([vprev_ref[...], v_new], axis=0)
    kprev_ref[...] = k_new
    vprev_ref[...] = v_new
    lo2 = jnp.concatenate([lo, lo], axis=0)
    kswap = pltpu.roll(kband, 64, axis=1)
    vswap = pltpu.roll(vband, 64, axis=1)
    k2 = [jnp.where(lo2, kband, kswap).astype(BF16), jnp.where(lo2, kswap, kband).astype(BF16)]
    v2 = [jnp.where(lo2, vband, vswap).astype(BF16), jnp.where(lo2, vswap, vband).astype(BF16)]

    qi = lax.broadcasted_iota(jnp.int32, (L, 2 * L), 0)
    kj = lax.broadcasted_iota(jnp.int32, (L, 2 * L), 1)
    diff = L + qi - kj
    valid = (diff >= 0) & (diff <= WINDOW) & ((kj >= L) | has_prev)

    qt = []
    for t in range(4):
        qn = _head_norm(q_ref[:, 128 * t:128 * (t + 1)], qw_ref[...], bd)
        qt.append(_rope(qn, cos, sin, first_half) * (HEAD_DIM ** -0.5))

    otiles = []
    for g in range(KV_HEADS):
        lhs = []
        for t in (2 * g, 2 * g + 1):
            lhs.append(jnp.where(lo, qt[t], 0.0))
            lhs.append(jnp.where(lo, 0.0, qt[t]))
        s = _dot_nt(jnp.concatenate(lhs, axis=0).astype(BF16), k2[g])
        ps = []
        for i in range(4):
            sink = sink_ref[layer, 4 * g + i]
            sh = jnp.where(valid, s[L * i:L * (i + 1)], -1e30)
            mx = jnp.maximum(jnp.max(sh, axis=-1, keepdims=True), sink)
            p = jnp.exp(sh - mx)
            den = jnp.sum(p, axis=-1, keepdims=True) + jnp.exp(sink - mx)
            ps.append(p * (1.0 / den))
        og = _dot(jnp.concatenate(ps, axis=0).astype(BF16), v2[g])
        otiles.append(jnp.where(lo, og[0:L], og[L:2 * L]))
        otiles.append(jnp.where(lo, og[2 * L:3 * L], og[3 * L:4 * L]))
    o_ref[...] = jnp.concatenate(otiles, axis=1).astype(BF16)


def _mixer_kernel(sink_ref, zx_ref, g_ref, kvdt_ref, q_ref, cos_ref, sin_ref,
                  cw_ref, cb_ref, dtb_ref, alog_ref, dsk_ref, nw_ref, scw_ref, qw_ref, kw_ref, *rest,
                  layer, with_mod):
    if with_mod:
        a_ref, wada_ref, bada_ref = rest[:3]
        rest = rest[3:]
    (y_ref, yc_ref, st_out_ref, cst_ref, ust_ref, o_ref, kwin_ref, vwin_ref) = rest[:8]
    rest = rest[8:]
    if with_mod:
        modn_ref = rest[0]
        rest = rest[1:]
        modn_ref[...] = _dot(a_ref[...], wada_ref[...].astype(BF16)) + bada_ref[...]
    xpad_ref, upad_ref, st_ref, kprev_ref, vprev_ref = rest
    c = pl.program_id(1)

    @pl.when(c == 0)
    def _():
        xpad_ref[0:8, :] = jnp.zeros((8, D_XBC), F32)
        upad_ref[0:8, :] = jnp.zeros((8, D_CONV), F32)
        st_ref[...] = jnp.zeros_like(st_ref)
        kprev_ref[...] = jnp.zeros_like(kprev_ref)
        vprev_ref[...] = jnp.zeros_like(vprev_ref)

    for r in range(MIX_CHUNKS):
        rows = lambda ref: ref.at[r * CHUNK:(r + 1) * CHUNK]
        _ssd_body(rows(zx_ref), rows(g_ref), rows(kvdt_ref), cw_ref, cb_ref, dtb_ref, alog_ref, dsk_ref,
                  nw_ref, scw_ref, rows(y_ref), rows(yc_ref), cst_ref, ust_ref, xpad_ref, upad_ref, st_ref)
        _attn_body(sink_ref, rows(q_ref), rows(kvdt_ref), rows(cos_ref), rows(sin_ref), qw_ref, kw_ref,
                   rows(o_ref), kprev_ref, vprev_ref, layer=layer,
                   has_prev=(c > 0) if r == 0 else True)
    kwin_ref[...] = kprev_ref[...]
    vwin_ref[...] = vprev_ref[...]

    @pl.when(c == pl.num_programs(1) - 1)
    def _():
        st_out_ref[...] = st_ref[...].T


def _mixer_prompt(proj, nb, seq, l, sinks, cos_t, sin_t, cw, cb, dtb, alog, dsk, nw, scw, qw, kw,
                  mod_next=None):
    tr = MIX_CHUNKS * CHUNK
    nc = seq // tr
    m = nb * seq
    row = lambda b, c: (b * nc + c)
    blk = lambda width, col: pl.BlockSpec((tr, width), lambda b, c: (row(b, c), col))
    per_batch = lambda rows, width: pl.BlockSpec((None, rows, width), lambda b, c: (b, 0, 0))
    with_mod = mod_next is not None
    mod_in, mod_out, mod_shape, mod_args = [], [], [], []
    if with_mod:
        a16, w_ada, b_ada3 = mod_next
        r, d = a16.shape
        n = w_ada.shape[2]
        cw_mod = n // (nb * nc)
        assert cw_mod * nb * nc == n and cw_mod % 128 == 0
        mod_in = [pl.BlockSpec((r, d), lambda b, c: (0, 0)),
                  pl.BlockSpec((None, d, cw_mod), lambda b, c: (l + 1, 0, row(b, c))),
                  pl.BlockSpec((None, 1, cw_mod), lambda b, c: (l + 1, 0, row(b, c)))]
        mod_out = [pl.BlockSpec((r, cw_mod), lambda b, c: (0, row(b, c)))]
        mod_shape = [jax.ShapeDtypeStruct((r, n), F32)]
        mod_args = [a16, w_ada, b_ada3]
    return pl.pallas_call(
        functools.partial(_mixer_kernel, layer=l, with_mod=with_mod),
        grid=(nb, nc),
        in_specs=[
            pl.BlockSpec(memory_space=pltpu.SMEM),
            blk(2560, 0), blk(1536, 2), blk(512, 9), blk(512, 5),
            pl.BlockSpec((tr, 128), lambda b, c: (c, 0)),
            pl.BlockSpec((tr, 128), lambda b, c: (c, 0)),
            _layer_spec(cw, l), _layer_spec(cb, l), _layer_spec(dtb, l), _layer_spec(alog, l),
            _layer_spec(dsk, l), _layer_spec(nw, l), _layer_spec(scw, l), _layer_spec(qw, l), _layer_spec(kw, l),
        ] + mod_in,
        out_specs=[
            blk(D_SSD, 0), blk(D_CONV, 0),
            per_batch(D_SSD, D_STATE), per_batch(8, D_XBC), per_batch(8, D_CONV),
            blk(D_ATTN, 0), per_batch(CHUNK, 128), per_batch(CHUNK, 128),
        ] + mod_out,
        out_shape=[
            jax.ShapeDtypeStruct((m, D_SSD), BF16),
            jax.ShapeDtypeStruct((m, D_CONV), BF16),
            jax.ShapeDtypeStruct((nb, D_SSD, D_STATE), F32),
            jax.ShapeDtypeStruct((nb, 8, D_XBC), F32),
            jax.ShapeDtypeStruct((nb, 8, D_CONV), F32),
            jax.ShapeDtypeStruct((m, D_ATTN), BF16),
            jax.ShapeDtypeStruct((nb, CHUNK, 128), F32),
            jax.ShapeDtypeStruct((nb, CHUNK, 128), F32),
        ] + mod_shape,
        scratch_shapes=[
            pltpu.VMEM((8 + CHUNK, D_XBC), F32),
            pltpu.VMEM((8 + CHUNK, D_CONV), F32),
            pltpu.VMEM((D_STATE, D_SSD), F32),
            pltpu.VMEM((CHUNK, 128), F32),
            pltpu.VMEM((CHUNK, 128), F32),
        ],
        compiler_params=_cparams(("arbitrary", "arbitrary")),
        name="mixer_prompt",
    )(sinks, proj, proj, proj, proj, cos_t, sin_t, cw, cb, dtb, alog, dsk, nw, scw, qw, kw, *mod_args)


def _outproj_rows(ys_ref, o_ref, yc_ref, x_ref, gate_ref, sc_ref, sh_ref, w_ref, nw_ref, x1_ref, h2_ref):
    n = x_ref.shape[0]
    sub = min(n, 256)
    for r in range(n // sub):
        rs = slice(r * sub, (r + 1) * sub)
        gs = rs if gate_ref.shape[0] == n else slice(None)
        acc = _dot(ys_ref[rs, :], w_ref[0:D_SSD, :])
        acc += _dot(o_ref[rs, :], w_ref[D_SSD:D_SSD + D_ATTN, :])
        acc += _dot(yc_ref[rs, :], w_ref[D_SSD + D_ATTN:, :])
        x1 = x_ref[rs, :] + gate_ref[gs, :] * acc
        x1_ref[rs, :] = x1
        h2_ref[rs, :] = _rms_mod(x1, nw_ref[...], sc_ref[gs, :], sh_ref[gs, :]).astype(BF16)


def _outproj_kernel(*refs):
    p_in, s_in, (w_ref, nw_ref), p_out, s_out = refs[0:7], refs[7:14], refs[14:16], refs[16:18], refs[18:20]

    @pl.when(pl.program_id(0) == 0)
    def _():
        _outproj_rows(*s_in, w_ref, nw_ref, *s_out)

    _outproj_rows(*p_in, w_ref, nw_ref, *p_out)


def _row_specs(widths, tm):
    return [pl.BlockSpec((tm, w), lambda i: (i, 0)) for w in widths]


def _const_specs(widths, rows):
    return [pl.BlockSpec((rows, w), lambda i: (0, 0)) for w in widths]


def _outproj(mix_p, x_p, mod_p, mix_s, x_s, mod_s, w_all, l, nw2_all, tm, rpb):
    m, ms = x_p.shape[0], x_s.shape[0]
    widths = (D_SSD, D_ATTN, D_CONV, D_MODEL)
    mods = lambda mod, t, r: [_mod_spec(mod, 2, t, r), _mod_spec(mod, 4, t, r), _mod_spec(mod, 3, t, r)]
    return pl.pallas_call(
        _outproj_kernel,
        grid=(m // tm,),
        in_specs=(_row_specs(widths, tm) + mods(mod_p, tm, rpb)
                  + _const_specs(widths, ms) + mods(mod_s, 0, 1)
                  + [pl.BlockSpec((D_MODEL, D_MODEL), lambda i: (0, 0), pipeline_mode=pl.Buffered(1)),
                     _layer_spec(nw2_all, l)]),
        out_specs=_row_specs((D_MODEL, D_MODEL), tm) + _const_specs((D_MODEL, D_MODEL), ms),
        out_shape=[
            jax.ShapeDtypeStruct((m, D_MODEL), F32), jax.ShapeDtypeStruct((m, D_MODEL), BF16),
            jax.ShapeDtypeStruct((ms, D_MODEL), F32), jax.ShapeDtypeStruct((ms, D_MODEL), BF16),
        ],
        compiler_params=_cparams(("arbitrary",)),
        name="out_proj",
    )(*mix_p, x_p, mod_p, mod_p, mod_p, *mix_s, x_s, mod_s, mod_s, mod_s, w_all, nw2_all)


def _mix_out_kernel(*refs, layer, with_mod):
    it = iter(refs)
    take = lambda n: [next(it) for _ in range(n)]
    (sink_ref, zx_ref, g_ref, kvdt_ref, q_ref, cos_ref, sin_ref,
     cw_ref, cb_ref, dtb_ref, alog_ref, dsk_ref, nw_ref, scw_ref, qw_ref, kw_ref) = take(16)
    x_ref, gate_ref, sc_ref, sh_ref = take(4)
    sample_in = take(7)
    w_ref, n2_ref = take(2)
    if with_mod:
        a_ref, wada_ref, bada_ref = take(3)
    st_out_ref, cst_ref, ust_ref, kwin_ref, vwin_ref, x1_ref, h2_ref, x1s_ref, h2s_ref = take(9)
    if with_mod:
        (modn_ref,) = take(1)
        modn_ref[...] = _dot(a_ref[...], wada_ref[...].astype(BF16)) + bada_ref[...]
    y_ref, yc_ref, o_ref, xpad_ref, upad_ref, st_ref, kprev_ref, vprev_ref = take(8)
    c = pl.program_id(1)

    @pl.when(c == 0)
    def _():
        xpad_ref[0:8, :] = jnp.zeros((8, D_XBC), F32)
        upad_ref[0:8, :] = jnp.zeros((8, D_CONV), F32)
        st_ref[...] = jnp.zeros_like(st_ref)
        kprev_ref[...] = jnp.zeros_like(kprev_ref)
        vprev_ref[...] = jnp.zeros_like(vprev_ref)

    @pl.when((c == 0) & (pl.program_id(0) == 0))
    def _():
        _outproj_rows(*sample_in, w_ref, n2_ref, x1s_ref, h2s_ref)

    for r in range(MIX_CHUNKS):
        rows = lambda ref: ref.at[r * CHUNK:(r + 1) * CHUNK]
        _ssd_body(rows(zx_ref), rows(g_ref), rows(kvdt_ref), cw_ref, cb_ref, dtb_ref, alog_ref, dsk_ref,
                  nw_ref, scw_ref, rows(y_ref), rows(yc_ref), cst_ref, ust_ref, xpad_ref, upad_ref, st_ref)
        _attn_body(sink_ref, rows(q_ref), rows(kvdt_ref), rows(cos_ref), rows(sin_ref), qw_ref, kw_ref,
                   rows(o_ref), kprev_ref, vprev_ref, layer=layer,
                   has_prev=(c > 0) if r == 0 else True)
    kwin_ref[...] = kprev_ref[...]
    vwin_ref[...] = vprev_ref[...]

    _outproj_rows(y_ref, o_ref, yc_ref, x_ref, gate_ref, sc_ref, sh_ref, w_ref, n2_ref, x1_ref, h2_ref)

    @pl.when(c == pl.num_programs(1) - 1)
    def _():
        st_out_ref[...] = st_ref[...].T


def _mix_out(proj, x_p, mod_p, mix_s, x_s, mod_s, w_out_l, n2_all, nb, seq, l, sinks, cos_t, sin_t,
             cw, cb, dtb, alog, dsk, nw, scw, qw, kw, mod_next=None):
    tr = MIX_CHUNKS * CHUNK
    nc = seq // tr
    m = nb * seq
    ms = x_s.shape[0]
    row = lambda b, c: (b * nc + c)
    blk = lambda width, col: pl.BlockSpec((tr, width), lambda b, c: (row(b, c), col))
    per_batch = lambda rows, width: pl.BlockSpec((None, rows, width), lambda b, c: (b, 0, 0))
    const = lambda rows, width: pl.BlockSpec((rows, width), lambda b, c: (0, 0))
    mod_blk = lambda mod, j, bsel: pl.BlockSpec((None, mod.shape[1], D_MODEL), lambda b, c: (bsel(b), 0, j))
    mods = lambda mod, bsel: [mod_blk(mod, 2, bsel), mod_blk(mod, 4, bsel), mod_blk(mod, 3, bsel)]
    with_mod = mod_next is not None
    mod_in, mod_out, mod_shape, mod_args = [], [], [], []
    if with_mod:
        a16, w_ada, b_ada3 = mod_next
        r, d = a16.shape
        n = w_ada.shape[2]
        cw_mod = n // (nb * nc)
        assert cw_mod * nb * nc == n and cw_mod % 128 == 0
        mod_in = [const(r, d),
                  pl.BlockSpec((None, d, cw_mod), lambda b, c: (l + 1, 0, row(b, c))),
                  pl.BlockSpec((None, 1, cw_mod), lambda b, c: (l + 1, 0, row(b, c)))]
        mod_out = [pl.BlockSpec((r, cw_mod), lambda b, c: (0, row(b, c)))]
        mod_shape = [jax.ShapeDtypeStruct((r, n), F32)]
        mod_args = [a16, w_ada, b_ada3]
    return pl.pallas_call(
        functools.partial(_mix_out_kernel, layer=l, with_mod=with_mod),
        grid=(nb, nc),
        in_specs=([
            pl.BlockSpec(memory_space=pltpu.SMEM),
            blk(2560, 0), blk(1536, 2), blk(512, 9), blk(512, 5),
            pl.BlockSpec((tr, 128), lambda b, c: (c, 0)),
            pl.BlockSpec((tr, 128), lambda b, c: (c, 0)),
            _layer_spec(cw, l), _layer_spec(cb, l), _layer_spec(dtb, l), _layer_spec(alog, l),
            _layer_spec(dsk, l), _layer_spec(nw, l), _layer_spec(scw, l), _layer_spec(qw, l), _layer_spec(kw, l),
            blk(D_MODEL, 0)] + mods(mod_p, lambda b: b)
            + [const(ms, D_SSD), const(ms, D_ATTN), const(ms, D_CONV), const(ms, D_MODEL)]
            + mods(mod_s, lambda b: 0)
            + [pl.BlockSpec((D_MODEL, D_MODEL), lambda b, c: (0, 0), pipeline_mode=pl.Buffered(1)),
               _layer_spec(n2_all, l)]
            + mod_in),
        out_specs=[
            per_batch(D_SSD, D_STATE), per_batch(8, D_XBC), per_batch(8, D_CONV),
            per_batch(CHUNK, 128), per_batch(CHUNK, 128),
            blk(D_MODEL, 0), blk(D_MODEL, 0), const(ms, D_MODEL), const(ms, D_MODEL),
        ] + mod_out,
        out_shape=[
            jax.ShapeDtypeStruct((nb, D_SSD, D_STATE), F32),
            jax.ShapeDtypeStruct((nb, 8, D_XBC), F32),
            jax.ShapeDtypeStruct((nb, 8, D_CONV), F32),
            jax.ShapeDtypeStruct((nb, CHUNK, 128), F32),
            jax.ShapeDtypeStruct((nb, CHUNK, 128), F32),
            jax.ShapeDtypeStruct((m, D_MODEL), F32), jax.ShapeDtypeStruct((m, D_MODEL), BF16),
            jax.ShapeDtypeStruct((ms, D_MODEL), F32), jax.ShapeDtypeStruct((ms, D_MODEL), BF16),
        ] + mod_shape,
        scratch_shapes=[
            pltpu.VMEM((tr, D_SSD), BF16), pltpu.VMEM((tr, D_CONV), BF16), pltpu.VMEM((tr, D_ATTN), BF16),
            pltpu.VMEM((8 + CHUNK, D_XBC), F32),
            pltpu.VMEM((8 + CHUNK, D_CONV), F32),
            pltpu.VMEM((D_STATE, D_SSD), F32),
            pltpu.VMEM((CHUNK, 128), F32),
            pltpu.VMEM((CHUNK, 128), F32),
        ],
        compiler_params=_cparams(("arbitrary", "arbitrary")),
        name="mix_out",
    )(sinks, proj, proj, proj, proj, cos_t, sin_t, cw, cb, dtb, alog, dsk, nw, scw, qw, kw,
      x_p, mod_p, mod_p, mod_p, *mix_s, x_s, mod_s, mod_s, mod_s, w_out_l, n2_all, *mod_args)


def _up_seq_kernel(h_ref, hs_ref, wg_ref, wu_ref, cw_ref, cb_ref, prev_ref, wd_ref,
                   a_ref, gst_ref, as_ref, gts_ref, wd16_ref, pad_ref, wg16_ref, wu16_ref, *, tm, rpb):
    i = pl.program_id(1)
    cw = cw_ref[...]
    cb = cb_ref[...]
    wd16_ref[...] = wd_ref[...].astype(BF16)

    @pl.when(i == 0)
    def _():
        wg16_ref[...] = wg_ref[...].astype(BF16)
        wu16_ref[...] = wu_ref[...].astype(BF16)
        hs = hs_ref[...]
        gt = _dot(hs, wg16_ref[...])
        u = _dot(hs, wu16_ref[...])
        conv = cw[2:3] * gt + cw[1:2] * prev_ref[1] + cw[0:1] * prev_ref[0]
        gts_ref[...] = gt
        as_ref[...] = (_silu(conv + cb) * u).astype(BF16)

    @pl.when((i * tm) % rpb == 0)
    def _():
        pad_ref[0:8, :] = jnp.zeros((8, pad_ref.shape[1]), F32)

    sub = 512
    for r in range(tm // sub):
        h = h_ref[r * sub:(r + 1) * sub, :]
        gt = _dot(h, wg16_ref[...])
        u = _dot(h, wu16_ref[...])
        base = 8 + r * sub
        pad_ref[base:base + sub, :] = gt
        conv = (cw[2:3] * gt + cw[1:2] * pad_ref[base - 1:base - 1 + sub, :]
                + cw[0:1] * pad_ref[base - 2:base - 2 + sub, :])
        a_ref[r * sub:(r + 1) * sub, :] = (_silu(conv + cb) * u).astype(BF16)
    tail = pad_ref[tm:tm + 8, :]
    pad_ref[0:8, :] = tail
    gst_ref[...] = tail


def _up_proj(h, hs, w_up_all, l, cw_all, cb_all, prev_all, w_down_all, nb, tm, tn):
    m = h.shape[0]
    ms = hs.shape[0]
    rpb = m // nb
    nt = D_FF // tn
    ni = m // tm
    slab = D_FF // (nt * ni)
    assert slab * nt * ni == D_FF and slab % 16 == 0
    return pl.pallas_call(
        functools.partial(_up_seq_kernel, tm=tm, rpb=rpb),
        grid=(nt, ni),
        in_specs=[
            pl.BlockSpec((tm, D_MODEL), lambda j, i: (i, 0)),
            pl.BlockSpec((ms, D_MODEL), lambda j, i: (0, 0)),
            pl.BlockSpec((None, D_MODEL, tn), lambda j, i: (l, 0, j)),
            pl.BlockSpec((None, D_MODEL, tn), lambda j, i: (l, 0, j + nt)),
            pl.BlockSpec((None, 3, tn), lambda j, i: (l, 0, j)),
            pl.BlockSpec((None, 1, tn), lambda j, i: (l, 0, j)),
            pl.BlockSpec((None, 2, ms, tn), lambda j, i: (l, 0, 0, j)),
            pl.BlockSpec((None, slab, D_MODEL), lambda j, i: (l, j * ni + i, 0)),
        ],
        out_specs=[
            pl.BlockSpec((tm, tn), lambda j, i: (i, j)),
            pl.BlockSpec((None, 8, tn), lambda j, i: ((i * tm) // rpb, 0, j)),
            pl.BlockSpec((ms, tn), lambda j, i: (0, j)),
            pl.BlockSpec((ms, tn), lambda j, i: (0, j)),
            pl.BlockSpec((slab, D_MODEL), lambda j, i: (j * ni + i, 0)),
        ],
        out_shape=[
            jax.ShapeDtypeStruct((m, D_FF), BF16),
            jax.ShapeDtypeStruct((nb, 8, D_FF), F32),
            jax.ShapeDtypeStruct((ms, D_FF), BF16),
            jax.ShapeDtypeStruct((ms, D_FF), F32),
            jax.ShapeDtypeStruct((D_FF, D_MODEL), BF16),
        ],
        scratch_shapes=[
            pltpu.VMEM((8 + tm, tn), F32),
            pltpu.VMEM((D_MODEL, tn), BF16),
            pltpu.VMEM((D_MODEL, tn), BF16),
        ],
        compiler_params=_cparams(("arbitrary", "arbitrary")),
        name="up_proj",
    )(h, hs, w_up_all, w_up_all, cw_all, cb_all, prev_all, w_down_all)


def _down_rows(a_ref, x_ref, gate_ref, mods, w_ref, nw_ref, x2_ref, hn_ref):
    x2 = x_ref[...] + gate_ref[...] * _dot(a_ref[...], w_ref[...])
    x2_ref[...] = x2
    if hn_ref is not None:
        hn_ref[...] = _rms_mod(x2, nw_ref[...], mods[0][...], mods[1][...]).astype(BF16)


def _down_kernel(*refs, with_norm):
    refs = list(refs)
    n_in = 5 if with_norm else 3
    p_in, s_in = refs[0:n_in], refs[n_in:2 * n_in]
    w_ref = refs[2 * n_in]
    nw_ref = refs[2 * n_in + 1] if with_norm else None
    outs = refs[2 * n_in + (2 if with_norm else 1):]
    p_out = (outs[0], outs[1]) if with_norm else (outs[0], None)
    s_out = (outs[2], outs[3]) if with_norm else (outs[1], None)

    @pl.when(pl.program_id(0) == 0)
    def _():
        _down_rows(s_in[0], s_in[1], s_in[2], s_in[3:], w_ref, nw_ref, *s_out)

    _down_rows(p_in[0], p_in[1], p_in[2], p_in[3:], w_ref, nw_ref, *p_out)


def _down(a_p, x_p, mod_p, a_s, x_s, mod_s, w_all, l, nw1_all, tm, rpb):
    m, ms = x_p.shape[0], x_s.shape[0]
    with_norm = l + 1 < nw1_all.shape[0]

    def mods(mod, t, r):
        sp = [_mod_spec(mod[0], 5, t, r)]
        if with_norm:
            sp += [_mod_spec(mod[1], 1, t, r), _mod_spec(mod[1], 0, t, r)]
        return sp

    def mod_args(mod):
        return [mod[0], mod[1], mod[1]] if with_norm else [mod[0]]

    in_specs = (_row_specs((D_FF, D_MODEL), tm) + mods(mod_p, tm, rpb)
                + _const_specs((D_FF, D_MODEL), ms) + mods(mod_s, 0, 1)
                + [pl.BlockSpec((D_FF, D_MODEL), lambda i: (0, 0), pipeline_mode=pl.Buffered(1))])
    args = [a_p, x_p] + mod_args(mod_p) + [a_s, x_s] + mod_args(mod_s) + [w_all]
    out_w = (D_MODEL, D_MODEL) if with_norm else (D_MODEL,)
    out_dt = (F32, BF16) if with_norm else (F32,)
    if with_norm:
        in_specs.append(_layer_spec(nw1_all, l + 1))
        args.append(nw1_all)
    res = pl.pallas_call(
        functools.partial(_down_kernel, with_norm=with_norm),
        grid=(m // tm,),
        in_specs=in_specs,
        out_specs=_row_specs(out_w, tm) + _const_specs(out_w, ms),
        out_shape=([jax.ShapeDtypeStruct((m, D_MODEL), d) for d in out_dt]
                   + [jax.ShapeDtypeStruct((ms, D_MODEL), d) for d in out_dt]),
        compiler_params=_cparams(("arbitrary",)),
        name="down_proj",
    )(*args)
    if with_norm:
        return res[0], res[1], res[2], res[3]
    return res[0], None, res[1], None


def _expand_matrix():
    r = lax.broadcasted_iota(jnp.int32, (128, D_SSD), 0)
    c = lax.broadcasted_iota(jnp.int32, (128, D_SSD), 1)
    return jnp.where(r == (c >> 6), 1.0, 0.0).astype(BF16)


def _ssd_step_kernel(p_ref, st_ref, cprev_ref, sprev_ref, cw_ref, cb_ref, dtb_ref, alog_ref, dsk_ref, nw_ref,
                     scw_ref, y_ref, yc_ref, u_ref, stn_ref):
    nbb = p_ref.shape[0]
    z = p_ref[:, C_Z:C_Z + D_SSD]
    raw = p_ref[:, C_XBC:C_XBC + D_XBC]
    cw = cw_ref[...]
    conv = cw[3:4] * raw + cw[2:3] * cprev_ref[2] + cw[1:2] * cprev_ref[1] + cw[0:1] * cprev_ref[0]
    xbc = _silu(conv + cb_ref[...])
    xs = xbc[:, :D_SSD]
    bm = xbc[:, D_SSD:D_SSD + 256]
    cm = xbc[:, D_SSD + 256:]

    lane = lax.broadcasted_iota(jnp.int32, (1, 128), 1)
    a_row = jnp.where(lane < SSM_HEADS, -jnp.exp(alog_ref[...]), 0.0)
    dtr = p_ref[:, C_DT:C_DT + 128] + dtb_ref[...]
    dt = jnp.maximum(dtr, 0.0) + jnp.log1p(jnp.exp(-jnp.abs(dtr)))
    dec = jnp.exp(dt * a_row)
    em = _expand_matrix()

    def expand(v):
        v1, v2, v3 = _split3(v)
        return _dot(v1, em) + _dot(v2, em) + _dot(v3, em)

    xdt_t = (xs * expand(dt)).T
    dec_t = expand(dec).T
    rowid = lax.broadcasted_iota(jnp.int32, (nbb, 512), 0)
    ysum = [jnp.zeros((nbb, 512), F32), jnp.zeros((nbb, 512), F32)]
    for i in range(nbb):
        for g in range(2):
            rs = slice(512 * g, 512 * (g + 1))
            hn = st_ref[i, rs, :] * dec_t[rs, i:i + 1] + xdt_t[rs, i:i + 1] * bm[i:i + 1, 128 * g:128 * (g + 1)]
            stn_ref[i, rs, :] = hn
            r = _dot_nt(cm[:, 128 * g:128 * (g + 1)].astype(BF16), hn.astype(BF16))
            ysum[g] = ysum[g] + jnp.where(rowid == i, r, 0.0)
    y = jnp.concatenate(ysum, axis=1) + dsk_ref[...] * xs
    y = y * _silu(z)
    nw = nw_ref[...]
    outs = []
    for g in range(2):
        yg = y[:, 512 * g:512 * (g + 1)]
        ms = jnp.mean(yg * yg, axis=-1, keepdims=True)
        outs.append(yg * lax.rsqrt(ms + EPS) * nw[:, 512 * g:512 * (g + 1)])
    y_ref[...] = jnp.concatenate(outs, axis=1).astype(BF16)

    u = p_ref[:, C_G + D_CONV:C_G + 2 * D_CONV] * p_ref[:, C_G + 2 * D_CONV:C_G + 3 * D_CONV]
    sw = scw_ref[...]
    sconv = sw[2:3] * u + sw[1:2] * sprev_ref[1] + sw[0:1] * sprev_ref[0]
    u_ref[...] = u
    yc_ref[...] = (p_ref[:, C_G:C_G + D_CONV] * sconv).astype(BF16)


def _ssd_step(proj, state_all, cprev_all, sprev_all, l, cw, cb, dtb, alog, dsk, nw, scw, nbb=8):
    nb = proj.shape[0]
    return pl.pallas_call(
        _ssd_step_kernel,
        grid=(nb // nbb,),
        in_specs=[
            pl.BlockSpec((nbb, PROJ_P), lambda i: (i, 0)),
            pl.BlockSpec((None, nbb, D_SSD, D_STATE), lambda i: (l, i, 0, 0)),
            pl.BlockSpec((None, 3, nbb, D_XBC), lambda i: (l, 0, i, 0)),
            pl.BlockSpec((None, 2, nbb, D_CONV), lambda i: (l, 0, i, 0)),
            _layer_spec(cw, l), _layer_spec(cb, l), _layer_spec(dtb, l), _layer_spec(alog, l),
            _layer_spec(dsk, l), _layer_spec(nw, l), _layer_spec(scw, l),
        ],
        out_specs=[
            pl.BlockSpec((nbb, D_SSD), lambda i: (i, 0)),
            pl.BlockSpec((nbb, D_CONV), lambda i: (i, 0)),
            pl.BlockSpec((nbb, D_CONV), lambda i: (i, 0)),
            pl.BlockSpec((nbb, D_SSD, D_STATE), lambda i: (i, 0, 0)),
        ],
        out_shape=[
            jax.ShapeDtypeStruct((nb, D_SSD), BF16),
            jax.ShapeDtypeStruct((nb, D_CONV), BF16),
            jax.ShapeDtypeStruct((nb, D_CONV), F32),
            jax.ShapeDtypeStruct((nb, D_SSD, D_STATE), F32),
        ],
        compiler_params=_cparams(("arbitrary",)),
        name="ssd_step",
    )(proj, state_all, cprev_all, sprev_all, cw, cb, dtb, alog, dsk, nw, scw)


def _qk_step_kernel(p_ref, cos_ref, sin_ref, qw_ref, kw_ref, q_ref, k_ref):
    bd = _seg_mean_matrix()
    lane = lax.broadcasted_iota(jnp.int32, (p_ref.shape[0], 128), 1)
    first_half = (lane & 32) == 0
    cos = cos_ref[...]
    sin = sin_ref[...]
    for t in range(4):
        qn = _head_norm(p_ref[:, C_Q + 128 * t:C_Q + 128 * (t + 1)], qw_ref[...], bd)
        q_ref[:, 128 * t:128 * (t + 1)] = _rope(qn, cos, sin, first_half) * (HEAD_DIM ** -0.5)
    k_ref[...] = _rope(_head_norm(p_ref[:, C_K:C_K + 128], kw_ref[...], bd), cos, sin, first_half)


def _qk_step(proj, cos_r, sin_r, l, qw, kw):
    nb = proj.shape[0]
    full = lambda a: pl.BlockSpec(a.shape, lambda i: (0,) * a.ndim)
    return pl.pallas_call(
        _qk_step_kernel,
        grid=(1,),
        in_specs=[full(proj), full(cos_r), full(sin_r), _layer_spec(qw, l), _layer_spec(kw, l)],
        out_specs=[pl.BlockSpec((nb, D_ATTN), lambda i: (0, 0)), pl.BlockSpec((nb, 128), lambda i: (0, 0))],
        out_shape=[jax.ShapeDtypeStruct((nb, D_ATTN), F32), jax.ShapeDtypeStruct((nb, 128), F32)],
        name="qk_step",
    )(proj, cos_r, sin_r, qw, kw)


def _attn_step_kernel(q_ref, kc_ref, vc_ref, knew_ref, vnew_ref, sink_ref, o_ref):
    q = q_ref[...]
    s = lax.dot_general(q.astype(BF16), kc_ref[...].astype(BF16), (((2,), (1,)), ((0,), (0,))),
                        preferred_element_type=F32)
    s_new = jnp.sum(q * knew_ref[...], axis=-1, keepdims=True)
    sink = sink_ref[...]
    mx = jnp.maximum(jnp.maximum(jnp.max(s, axis=-1, keepdims=True), s_new), sink)
    p = jnp.exp(s - mx)
    p_new = jnp.exp(s_new - mx)
    den = jnp.sum(p, axis=-1, keepdims=True) + p_new + jnp.exp(sink - mx)
    o = lax.dot_general((p / den).astype(BF16), vc_ref[...].astype(BF16), (((2,), (2,)), ((0,), (0,))),
                        preferred_element_type=F32)
    o_ref[...] = o + (p_new / den) * vnew_ref[...]


def _attn_step(q3, kc_all, vc_all, l, knew, vnew, sink_all):
    nb = q3.shape[0]
    full = lambda a: pl.BlockSpec(a.shape, lambda i: (0,) * a.ndim)
    return pl.pallas_call(
        _attn_step_kernel,
        grid=(1,),
        in_specs=[full(q3), _layer_spec(kc_all, l), _layer_spec(vc_all, l), full(knew), full(vnew),
                  _layer_spec(sink_all, l)],
        out_specs=pl.BlockSpec((nb, ATT_HEADS, 128), lambda i: (0, 0, 0)),
        out_shape=jax.ShapeDtypeStruct((nb, ATT_HEADS, 128), F32),
        name="attn_step",
    )(q3, kc_all, vc_all, knew, vnew, sink_all)


def _rope_tables(pos):
    half = HEAD_DIM // 2
    inv = np.power(ROPE_THETA, -np.arange(half, dtype=np.float64) / half)
    ang = np.asarray(pos, np.float64)[:, None] * inv[None, :]
    cos = np.cos(ang)
    sin = np.sin(ang)
    cos_t = np.concatenate([cos, cos, cos, cos], axis=-1).astype(np.float32)
    sin_t = np.concatenate([-sin, sin, -sin, sin], axis=-1).astype(np.float32)
    return jnp.asarray(cos_t), jnp.asarray(sin_t)


def kernel(x_prompt, x_sample, state_ssm, state_ssm_conv, cache_win_k, cache_win_v, state_short_conv,
           state_ffn_conv, c_prompt, c_sample, norm1_w, norm2_w, w_ada, b_ada, w_in, ssm_conv_w, ssm_conv_b,
           dt_bias, a_log, d_skip, ssm_norm_w, q_norm_w, k_norm_w, sinks, sconv_w, w_out, w_up, ffn_conv_w,
           ffn_conv_b, w_down):
    nbp, seq, _ = x_prompt.shape
    nbs = x_sample.shape[0]
    depth = w_in.shape[0]
    mp = nbp * seq
    nbuf = cache_win_k.shape[2]

    in_tab = _in_proj_table()
    w_in_t = jnp.swapaxes(w_in, 1, 2)

    n1 = norm1_w.reshape(depth, 1, D_MODEL)
    n2 = norm2_w.reshape(depth, 1, D_MODEL)
    cb = ssm_conv_b.reshape(depth, 1, D_XBC)
    dtb = jnp.pad(dt_bias, ((0, 0), (0, 128 - SSM_HEADS))).reshape(depth, 1, 128)
    alog = jnp.pad(a_log, ((0, 0), (0, 128 - SSM_HEADS))).reshape(depth, 1, 128)
    dsk = jnp.repeat(d_skip, SSM_HEAD_DIM, axis=1).reshape(depth, 1, D_SSD)
    nw = ssm_norm_w.reshape(depth, 1, D_SSD)
    qw = jnp.tile(q_norm_w, (1, 2)).reshape(depth, 1, 128)
    kw = jnp.tile(k_norm_w, (1, 2)).reshape(depth, 1, 128)
    fcb = ffn_conv_b.reshape(depth, 1, D_FF)
    sink3 = sinks.reshape(depth, ATT_HEADS, 1)

    st_all = state_ssm.reshape(depth, nbs, D_SSD, D_STATE)
    cprev = jnp.transpose(state_ssm_conv, (0, 2, 1, 3))
    sprev = jnp.transpose(state_short_conv, (0, 2, 1, 3))
    fprev = jnp.transpose(state_ffn_conv, (0, 2, 1, 3))
    ck = cache_win_k.reshape(depth, nbs, nbuf, 128)
    cv = cache_win_v.reshape(depth, nbs, nbuf, 128)
    ck_t = jnp.swapaxes(ck, 2, 3)
    cv_t = jnp.swapaxes(cv, 2, 3)

    c_all = jnp.concatenate([c_prompt, jnp.zeros((8 - nbp, D_MODEL), F32), c_sample], axis=0)
    b_ada3 = b_ada.reshape(depth, 1, 6 * D_MODEL)
    mod0, a16 = _modulation_first(c_all, w_ada, b_ada3)

    def split_mod(mod):
        return mod[:nbp].reshape(nbp, 1, 6 * D_MODEL), mod[8:].reshape(1, nbs, 6 * D_MODEL)

    mod_p, mod_s = [None] * (depth + 1), [None] * (depth + 1)
    mod_p[0], mod_s[0] = split_mod(mod0)

    cos_p, sin_p = _rope_tables(np.arange(seq))
    cos_s, sin_s = _rope_tables(np.full((1,), PAST_LEN))

    head_g = jnp.arange(ATT_HEADS) // (ATT_HEADS // KV_HEADS)
    q_place = (head_g[:, None] == jnp.arange(KV_HEADS)[None, :]).astype(F32)[None, :, :, None]

    xp = x_prompt.reshape(mp, D_MODEL)
    xs = x_sample.reshape(nbs, D_MODEL)
    TM = 512
    hp = _norm_mod(xp, n1, mod_p[0], 0, 1, 0, TM, seq)
    hs = _norm_mod(xs, n1, mod_s[0], 0, 1, 0, nbs, nbs)

    outs_p = [[] for _ in range(6)]
    outs_s = [[] for _ in range(6)]
    for l in range(depth):
        proj, projs, w_out_l = _in_proj(hp, hs, w_in_t, l, in_tab, w_out, min(1024, seq))

        y_s, yc_s, u_s, st_new = _ssd_step(projs, st_all, cprev, sprev, l, ssm_conv_w, cb, dtb, alog, dsk, nw,
                                           sconv_w)
        q_r, k_r = _qk_step(projs, cos_s, sin_s, l, qw, kw)
        v_r = projs[:, C_V:C_V + 128]
        q3 = (q_r.reshape(nbs, ATT_HEADS, 1, HEAD_DIM) * q_place).reshape(nbs, ATT_HEADS, 128)
        o3 = _attn_step(q3, ck_t, cv_t, l, k_r[:, None, :], v_r[:, None, :], sink3)
        o4 = o3.reshape(nbs, ATT_HEADS, KV_HEADS, HEAD_DIM)
        o_s = jnp.concatenate([o4[:, :4, 0], o4[:, 4:, 1]], axis=1).reshape(nbs, D_ATTN).astype(BF16)

        mixed = _mix_out(proj, xp, mod_p[l], (y_s, o_s, yc_s), xs, mod_s[l], w_out_l, n2, nbp, seq, l,
                         sinks, cos_p, sin_p, ssm_conv_w, cb, dtb, alog, dsk, nw, sconv_w, qw, kw,
                         mod_next=(a16, w_ada, b_ada3) if l + 1 < depth else None)
        st_t, cst, ust, kwin, vwin, xp, h2, xs, h2s = mixed[:9]
        if l + 1 < depth:
            mod_p[l + 1], mod_s[l + 1] = split_mod(mixed[9])
        act, gst, act_s, gt_s, w_down_l = _up_proj(h2, h2s, w_up, l, ffn_conv_w, fcb, fprev, w_down, nbp,
                                                   min(2048, seq), 512)
        xp, hp, xs, hs = _down(act, xp, mod_p[l:l + 2], act_s, xs, mod_s[l:l + 2], w_down_l, l, n1, 256, seq)

        outs_p[0].append(st_t.reshape(nbp, SSM_HEADS, SSM_HEAD_DIM, D_STATE))
        outs_p[1].append(cst[:, 8 - (SSM_CONV - 1):])
        outs_p[2].append(kwin.reshape(nbp, WINDOW, KV_HEADS, HEAD_DIM))
        outs_p[3].append(vwin.reshape(nbp, WINDOW, KV_HEADS, HEAD_DIM))
        outs_p[4].append(ust[:, 6:])
        outs_p[5].append(gst[:, 6:])

        outs_s[0].append(st_new.reshape(nbs, SSM_HEADS, SSM_HEAD_DIM, D_STATE))
        outs_s[1].append(projs[:, C_XBC:C_XBC + D_XBC])
        outs_s[2].append(k_r)
        outs_s[3].append(v_r)
        outs_s[4].append(u_s)
        outs_s[5].append(gt_s)

    res_p = [jnp.stack(a, axis=0) for a in outs_p]
    new_rows = [jnp.stack(a, axis=0) for a in outs_s]
    s_ssm = new_rows[0]
    s_ssm_conv = jnp.concatenate([state_ssm_conv[:, :, 1:], new_rows[1][:, :, None]], axis=2)
    s_win_k = jnp.concatenate([ck[:, :, 1:], new_rows[2][:, :, None]], axis=2).reshape(cache_win_k.shape)
    s_win_v = jnp.concatenate([cv[:, :, 1:], new_rows[3][:, :, None]], axis=2).reshape(cache_win_v.shape)
    s_short = jnp.concatenate([state_short_conv[:, :, 1:], new_rows[4][:, :, None]], axis=2)
    s_ffn = jnp.concatenate([state_ffn_conv[:, :, 1:], new_rows[5][:, :, None]], axis=2)
    return (xp.reshape(nbp, seq, D_MODEL), xs.reshape(nbs, 1, D_MODEL), *res_p,
            s_ssm, s_ssm_conv, s_win_k, s_win_v, s_short, s_ffn)
```

```python
import functools

import jax
import jax.numpy as jnp
import numpy as np
from jax import lax
from jax.experimental import pallas as pl
from jax.experimental.pallas import tpu as pltpu

F32 = jnp.float32
BF16 = jnp.bfloat16

D_MODEL = 2048
D_SSD = 1024
SSM_HEADS = 16
SSM_HEAD_DIM = 64
D_STATE = 128
SSM_CONV = 4
CHUNK = 128
D_XBC = 1536
HEAD_DIM = 64
D_ATTN = 512
ATT_HEADS = 8
KV_HEADS = 2
WINDOW = 128
ROPE_THETA = 10000.0
D_CONV = 512
D_FF = 5632
PROJ_W = 4880
EPS = 1e-6
PAST_LEN = 16384

C_Z, C_XBC, C_Q, C_G, C_K, C_V, C_DT, PROJ_P = 0, 1024, 2560, 3072, 4608, 4736, 4864, 5120

VMEM_LIMIT = 56 * 1024 * 1024


def _cparams(sem):
    return pltpu.CompilerParams(dimension_semantics=sem, vmem_limit_bytes=VMEM_LIMIT)


def _silu(x):
    return x * jax.nn.sigmoid(x)


def _rms_mod(x, w, scale, shift):
    ms = jnp.mean(x * x, axis=-1, keepdims=True)
    return (x * lax.rsqrt(ms + EPS) * w) * (1.0 + scale) + shift


def _split3(x):
    h1 = x.astype(BF16)
    r1 = x - h1.astype(F32)
    h2 = r1.astype(BF16)
    r2 = r1 - h2.astype(F32)
    return h1, h2, r2.astype(BF16)


def _dot(a, b):
    return jnp.dot(a, b, preferred_element_type=F32)


def _dot_nt(a, b):
    return lax.dot_general(a, b, (((1,), (1,)), ((), ())), preferred_element_type=F32)


def _layer_spec(arr, l):
    shape = arr.shape[1:]
    return pl.BlockSpec((None,) + shape, lambda *_: (l,) + (0,) * len(shape))


def _mod_kernel(c_ref, w_ref, b_ref, o_ref, a_ref):
    a = _silu(c_ref[...]).astype(BF16)
    a_ref[...] = a
    o_ref[...] = _dot(a, w_ref[...].astype(BF16)) + b_ref[...]


def _modulation_first(c_all, w_ada, b_ada3, tn=1024):
    _, d, n = w_ada.shape
    r = c_all.shape[0]
    return pl.pallas_call(
        _mod_kernel,
        grid=(n // tn,),
        in_specs=[
            pl.BlockSpec((r, d), lambda j: (0, 0)),
            pl.BlockSpec((None, d, tn), lambda j: (0, 0, j)),
            pl.BlockSpec((None, 1, tn), lambda j: (0, 0, j)),
        ],
        out_specs=[pl.BlockSpec((r, tn), lambda j: (0, j)), pl.BlockSpec((r, d), lambda j: (0, 0))],
        out_shape=[jax.ShapeDtypeStruct((r, n), F32), jax.ShapeDtypeStruct((r, d), BF16)],
        compiler_params=_cparams(("arbitrary",)),
        name="modulation",
    )(c_all, w_ada, b_ada3)


def _norm_kernel(x_ref, w_ref, sc_ref, sh_ref, o_ref):
    o_ref[...] = _rms_mod(x_ref[...], w_ref[...], sc_ref[...], sh_ref[...]).astype(BF16)


def _mod_spec(mod, j, tm, rpb):
    r = mod.shape[1]
    return pl.BlockSpec((None, r, D_MODEL), lambda i: ((i * tm) // rpb, 0, j))


def _norm_mod(x, w_all, mod, l, j_scale, j_shift, tm, rpb):
    m = x.shape[0]
    return pl.pallas_call(
        _norm_kernel,
        grid=(m // tm,),
        in_specs=[
            pl.BlockSpec((tm, D_MODEL), lambda i: (i, 0)),
            _layer_spec(w_all, l),
            _mod_spec(mod, j_scale, tm, rpb),
            _mod_spec(mod, j_shift, tm, rpb),
        ],
        out_specs=pl.BlockSpec((tm, D_MODEL), lambda i: (i, 0)),
        out_shape=jax.ShapeDtypeStruct((m, D_MODEL), BF16),
        compiler_params=_cparams(("arbitrary",)),
        name="norm_mod",
    )(x, w_all, mod, mod)


MIX_CHUNKS = 2
IN_TN = 1024
IN_SRC = 256
IN_SUB = 256
IN_ROW_UNIT = 16


def _in_proj_table():
    segs = [(C_Z, C_Q, 0), (C_Q, C_G, 2576), (C_G, C_K, 3344), (C_K, C_DT, 3088), (C_DT, PROJ_P, 2560)]
    rows = []
    for st in range(PROJ_P // IN_SRC):
        c = st * IN_SRC
        lo, _, src = [g for g in segs if g[0] <= c < g[1]][0]
        rows.append((src + c - lo) // IN_ROW_UNIT)
    return jnp.array(rows, jnp.int32)


def _in_proj_kernel(tab_ref, h_ref, hs_ref, a0_ref, a1_ref, a2_ref, a3_ref, wo_ref,
                    o_ref, os_ref, wo16_ref, w16_ref, *, tm):
    del tab_ref
    j = pl.program_id(0)
    i = pl.program_id(1)
    nsub = tm // IN_SUB
    wo16_ref[...] = wo_ref[...].astype(BF16)

    @pl.when(i == 0)
    def _():
        for s, a_ref in enumerate((a0_ref, a1_ref, a2_ref, a3_ref)):
            w16_ref[:, IN_SRC * s:IN_SRC * (s + 1)] = a_ref[0].T.astype(BF16)
        os_ref[...] = _dot(hs_ref[...], w16_ref[...])

    def rows(r):
        return slice(r * IN_SUB, (r + 1) * IN_SUB)

    def tile(r):
        return _dot(h_ref[rows(r), :], w16_ref[...])

    @pl.when(j == 0)
    def _():
        for r in range(nsub):
            o_ref[rows(r), :] = _silu(tile(r))

    @pl.when(j > 0)
    def _():
        for r in range(nsub):
            o_ref[rows(r), :] = tile(r)


def _in_proj(h, hs, w_in_t, l, tab, w_out_all, tm):
    m, k = h.shape
    ms = hs.shape[0]
    nsrc = IN_TN // IN_SRC
    ni = m // tm
    nsteps = (PROJ_P // IN_TN) * ni
    nslab = 1 << (nsteps.bit_length() - 1)
    slab = D_MODEL // nslab
    a_spec = lambda s: pl.BlockSpec((pl.Element(1), pl.Element(IN_SRC), pl.Element(k)),
                                    lambda j, i, t: (l, t[nsrc * j + s] * IN_ROW_UNIT, 0))
    slab_idx = lambda j, i: jnp.minimum(j * ni + i, nslab - 1)
    return pl.pallas_call(
        functools.partial(_in_proj_kernel, tm=tm),
        grid_spec=pltpu.PrefetchScalarGridSpec(
            num_scalar_prefetch=1,
            grid=(PROJ_P // IN_TN, ni),
            in_specs=[
                pl.BlockSpec((tm, k), lambda j, i, t: (i, 0)),
                pl.BlockSpec((ms, k), lambda j, i, t: (0, 0)),
                a_spec(0), a_spec(1), a_spec(2), a_spec(3),
                pl.BlockSpec((None, slab, D_MODEL), lambda j, i, t: (l, slab_idx(j, i), 0)),
            ],
            out_specs=[
                pl.BlockSpec((tm, IN_TN), lambda j, i, t: (i, j)),
                pl.BlockSpec((ms, IN_TN), lambda j, i, t: (0, j)),
                pl.BlockSpec((slab, D_MODEL), lambda j, i, t: (slab_idx(j, i), 0)),
            ],
            scratch_shapes=[pltpu.VMEM((k, IN_TN), BF16)],
        ),
        out_shape=[jax.ShapeDtypeStruct((m, PROJ_P), F32), jax.ShapeDtypeStruct((ms, PROJ_P), F32),
                   jax.ShapeDtypeStruct((D_MODEL, D_MODEL), BF16)],
        compiler_params=_cparams(("arbitrary", "arbitrary")),
        name="in_proj",
    )(tab, h, hs, w_in_t, w_in_t, w_in_t, w_in_t, w_out_all)


def _ssd_body(zx_ref, g_ref, kvdt_ref, cw_ref, cb_ref, dtb_ref, alog_ref, dsk_ref, nw_ref, scw_ref,
              y_ref, yc_ref, cst_ref, ust_ref, xpad_ref, upad_ref, st_ref):
    L = CHUNK
    zg = zx_ref[:, :D_SSD]
    raw = zx_ref[:, D_SSD:]

    xpad_ref[8:8 + L, :] = raw
    cw = cw_ref[...]
    conv = (cw[3:4] * raw + cw[2:3] * xpad_ref[7:7 + L, :] + cw[1:2] * xpad_ref[6:6 + L, :]
            + cw[0:1] * xpad_ref[5:5 + L, :])
    xpad_ref[0:8, :] = raw[L - 8:, :]
    cst_ref[...] = raw[L - 8:, :]
    xbc = _silu(conv + cb_ref[...])
    xs = xbc[:, :D_SSD]

    lane = lax.broadcasted_iota(jnp.int32, (L, 128), 1)
    row = lax.broadcasted_iota(jnp.int32, (L, 128), 0)
    lo = (lane & 64) == 0
    causal = row >= lane

    head_ok = lane[0:1] < SSM_HEADS
    a_row = jnp.where(head_ok, -jnp.exp(alog_ref[...]), 0.0)
    dtr = kvdt_ref[:, 256:384] + dtb_ref[...]
    dt = jnp.maximum(dtr, 0.0) + jnp.log1p(jnp.exp(-jnp.abs(dtr)))
    da = dt * a_row

    tril = jnp.where(causal, 1.0, 0.0).astype(BF16)
    d1, d2, d3 = _split3(da)
    cs = _dot(tril, d1) + _dot(tril, d2) + _dot(tril, d3)
    cs_t = cs.T
    ecs = jnp.exp(cs)
    te = jnp.exp(cs[L - 1:L, :] - cs)
    ecl = ecs[L - 1:L, :]

    def pair(arr, j):
        rows = arr.shape[0]
        return jnp.where(lo[:rows], arr[:, 2 * j:2 * j + 1], arr[:, 2 * j + 1:2 * j + 2])

    yield
    cbs, yoffs, updates = [], [], []
    xdts = []
    for j in range(8):
        xdts.append(xs[:, 128 * j:128 * (j + 1)] * pair(dt, j))
    for g in range(2):
        bm = xbc[:, D_SSD + 128 * g:D_SSD + 128 * (g + 1)]
        cm = xbc[:, D_SSD + 256 + 128 * g:D_SSD + 256 + 128 * (g + 1)]
        cm16 = cm.astype(BF16)
        cbs.append(jnp.where(causal, _dot_nt(cm16, bm.astype(BF16)), 0.0))
        yoffs.append(_dot(cm16, st_ref[:, 512 * g:512 * (g + 1)].astype(BF16)))
        xe = jnp.concatenate([xdts[4 * g + i] * pair(te, 4 * g + i) for i in range(4)], axis=1)
        s_new = _dot(bm.T.astype(BF16), xe.astype(BF16))
        cd = jnp.concatenate([pair(ecl, 4 * g + i) for i in range(4)], axis=1)
        updates.append((cd, s_new))
    for g in range(2):
        cd, s_new = updates[g]
        st_ref[:, 512 * g:512 * (g + 1)] = cd * st_ref[:, 512 * g:512 * (g + 1)] + s_new

    ys = []
    for j in range(8):
        g = j // 4
        ca, cb_ = cs[:, 2 * j:2 * j + 1], cs[:, 2 * j + 1:2 * j + 2]
        ra, rb = cs_t[2 * j:2 * j + 1, :], cs_t[2 * j + 1:2 * j + 2, :]
        dec_a = jnp.exp(jnp.minimum(ca - ra, 0.0))
        dec_b = jnp.exp(jnp.minimum(cb_ - rb, 0.0))
        mm = jnp.concatenate([cbs[g] * dec_a, cbs[g] * dec_b], axis=1).astype(BF16)
        xp = xdts[j]
        rhs = jnp.concatenate([jnp.where(lo, xp, 0.0), jnp.where(lo, 0.0, xp)], axis=0).astype(BF16)
        yd = _dot(mm, rhs)
        yo = yoffs[g][:, 128 * (j % 4):128 * (j % 4 + 1)] * pair(ecs, j)
        ys.append(yd + yo + dsk_ref[:, 128 * j:128 * (j + 1)] * xs[:, 128 * j:128 * (j + 1)])
    y = jnp.concatenate(ys, axis=1) * zg
    nw = nw_ref[...]
    outs = []
    for g in range(2):
        yg = y[:, 512 * g:512 * (g + 1)]
        ms = jnp.mean(yg * yg, axis=-1, keepdims=True)
        outs.append(yg * lax.rsqrt(ms + EPS) * nw[:, 512 * g:512 * (g + 1)])
    y_ref[...] = jnp.concatenate(outs, axis=1).astype(BF16)

    gg = g_ref[...]
    u = gg[:, D_CONV:2 * D_CONV] * gg[:, 2 * D_CONV:]
    upad_ref[8:8 + L, :] = u
    sw = scw_ref[...]
    sconv = sw[2:3] * u + sw[1:2] * upad_ref[7:7 + L, :] + sw[0:1] * upad_ref[6:6 + L, :]
    upad_ref[0:8, :] = u[L - 8:, :]
    ust_ref[...] = u[L - 8:, :]
    yc_ref[...] = (gg[:, :D_CONV] * sconv).astype(BF16)


def _head_norm(x, w, bd):
    sq = x * x
    hi = sq.astype(BF16)
    lo = (sq - hi.astype(F32)).astype(BF16)
    ms = _dot(hi, bd) + _dot(lo, bd)
    return x * lax.rsqrt(ms + EPS) * w


def _rope(x, cos, sin_signed, first_half):
    partner = jnp.where(first_half, pltpu.roll(x, 96, axis=1), pltpu.roll(x, 32, axis=1))
    return x * cos + partner * sin_signed


def _seg_mean_matrix():
    r = lax.broadcasted_iota(jnp.int32, (128, 128), 0)
    c = lax.broadcasted_iota(jnp.int32, (128, 128), 1)
    return jnp.where((r & 64) == (c & 64), 1.0 / HEAD_DIM, 0.0).astype(BF16)


def _attn_body(sink_ref, q_ref, kv_ref, cos_ref, sin_ref, qw_ref, kw_ref,
               o_ref, kprev_ref, vprev_ref, *, layer, has_prev):
    L = CHUNK
    bd = _seg_mean_matrix()
    lane = lax.broadcasted_iota(jnp.int32, (L, 128), 1)
    lo = (lane & 64) == 0
    first_half = (lane & 32) == 0
    cos = cos_ref[...]
    sin = sin_ref[...]

    k_new = _rope(_head_norm(kv_ref[:, 0:128], kw_ref[...], bd), cos, sin, first_half)
    v_new = kv_ref[:, 128:256]
    kband = jnp.concatenate([kprev_ref[...], k_new], axis=0)
    vband = jnp.concatenate([vprev_ref[...], v_new], axis=0)
    kprev_ref[...] = k_new
    vprev_ref[...] = v_new
    lo2 = jnp.concatenate([lo, lo], axis=0)
    kswap = pltpu.roll(kband, 64, axis=1)
    vswap = pltpu.roll(vband, 64, axis=1)
    k2 = [jnp.where(lo2, kband, kswap).astype(BF16), jnp.where(lo2, kswap, kband).astype(BF16)]
    v2 = [jnp.where(lo2, vband, vswap).astype(BF16), jnp.where(lo2, vswap, vband).astype(BF16)]

    qi = lax.broadcasted_iota(jnp.int32, (L, 2 * L), 0)
    kj = lax.broadcasted_iota(jnp.int32, (L, 2 * L), 1)
    diff = L + qi - kj
    valid = (diff >= 0) & (diff <= WINDOW) & ((kj >= L) | has_prev)

    qt = []
    for t in range(4):
        qn = _head_norm(q_ref[:, 128 * t:128 * (t + 1)], qw_ref[...], bd)
        qt.append(_rope(qn, cos, sin, first_half) * (HEAD_DIM ** -0.5))

    otiles = []
    for g in range(KV_HEADS):
        if g == 1:
            yield
        lhs = []
        for t in (2 * g, 2 * g + 1):
            lhs.append(jnp.where(lo, qt[t], 0.0))
            lhs.append(jnp.where(lo, 0.0, qt[t]))
        s = _dot_nt(jnp.concatenate(lhs, axis=0).astype(BF16), k2[g])
        ps = []
        for i in range(4):
            sink = sink_ref[layer, 4 * g + i]
            sh = jnp.where(valid, s[L * i:L * (i + 1)], -1e30)
            mx = jnp.maximum(jnp.max(sh, axis=-1, keepdims=True), sink)
            p = jnp.exp(sh - mx)
            den = jnp.sum(p, axis=-1, keepdims=True) + jnp.exp(sink - mx)
            ps.append(p * (1.0 / den))
        og = _dot(jnp.concatenate(ps, axis=0).astype(BF16), v2[g])
        otiles.append(jnp.where(lo, og[0:L], og[L:2 * L]))
        otiles.append(jnp.where(lo, og[2 * L:3 * L], og[3 * L:4 * L]))
    o_ref[...] = jnp.concatenate(otiles, axis=1).astype(BF16)


def _mixer_kernel(sink_ref, zx_ref, g_ref, kvdt_ref, q_ref, cos_ref, sin_ref,
                  cw_ref, cb_ref, dtb_ref, alog_ref, dsk_ref, nw_ref, scw_ref, qw_ref, kw_ref, *rest,
                  layer, with_mod):
    if with_mod:
        a_ref, wada_ref, bada_ref = rest[:3]
        rest = rest[3:]
    (y_ref, yc_ref, st_out_ref, cst_ref, ust_ref, o_ref, kwin_ref, vwin_ref) = rest[:8]
    rest = rest[8:]
    if with_mod:
        modn_ref = rest[0]
        rest = rest[1:]
        modn_ref[...] = _dot(a_ref[...], wada_ref[...].astype(BF16)) + bada_ref[...]
    xpad_ref, upad_ref, st_ref, kprev_ref, vprev_ref = rest
    c = pl.program_id(1)

    @pl.when(c == 0)
    def _():
        xpad_ref[0:8, :] = jnp.zeros((8, D_XBC), F32)
        upad_ref[0:8, :] = jnp.zeros((8, D_CONV), F32)
        st_ref[...] = jnp.zeros_like(st_ref)
        kprev_ref[...] = jnp.zeros_like(kprev_ref)
        vprev_ref[...] = jnp.zeros_like(vprev_ref)

    for r in range(MIX_CHUNKS):
        rows = lambda ref: ref.at[r * CHUNK:(r + 1) * CHUNK]
        _ssd_body(rows(zx_ref), rows(g_ref), rows(kvdt_ref), cw_ref, cb_ref, dtb_ref, alog_ref, dsk_ref,
                  nw_ref, scw_ref, rows(y_ref), rows(yc_ref), cst_ref, ust_ref, xpad_ref, upad_ref, st_ref)
        _attn_body(sink_ref, rows(q_ref), rows(kvdt_ref), rows(cos_ref), rows(sin_ref), qw_ref, kw_ref,
                   rows(o_ref), kprev_ref, vprev_ref, layer=layer,
                   has_prev=(c > 0) if r == 0 else True)
    kwin_ref[...] = kprev_ref[...]
    vwin_ref[...] = vprev_ref[...]

    @pl.when(c == pl.num_programs(1) - 1)
    def _():
        st_out_ref[...] = st_ref[...].T


def _mixer_prompt(proj, nb, seq, l, sinks, cos_t, sin_t, cw, cb, dtb, alog, dsk, nw, scw, qw, kw,
                  mod_next=None):
    tr = MIX_CHUNKS * CHUNK
    nc = seq // tr
    m = nb * seq
    row = lambda b, c: (b * nc + c)
    blk = lambda width, col: pl.BlockSpec((tr, width), lambda b, c: (row(b, c), col))
    per_batch = lambda rows, width: pl.BlockSpec((None, rows, width), lambda b, c: (b, 0, 0))
    with_mod = mod_next is not None
    mod_in, mod_out, mod_shape, mod_args = [], [], [], []
    if with_mod:
        a16, w_ada, b_ada3 = mod_next
        r, d = a16.shape
        n = w_ada.shape[2]
        cw_mod = n // (nb * nc)
        assert cw_mod * nb * nc == n and cw_mod % 128 == 0
        mod_in = [pl.BlockSpec((r, d), lambda b, c: (0, 0)),
                  pl.BlockSpec((None, d, cw_mod), lambda b, c: (l + 1, 0, row(b, c))),
                  pl.BlockSpec((None, 1, cw_mod), lambda b, c: (l + 1, 0, row(b, c)))]
        mod_out = [pl.BlockSpec((r, cw_mod), lambda b, c: (0, row(b, c)))]
        mod_shape = [jax.ShapeDtypeStruct((r, n), F32)]
        mod_args = [a16, w_ada, b_ada3]
    return pl.pallas_call(
        functools.partial(_mixer_kernel, layer=l, with_mod=with_mod),
        grid=(nb, nc),
        in_specs=[
            pl.BlockSpec(memory_space=pltpu.SMEM),
            blk(2560, 0), blk(1536, 2), blk(512, 9), blk(512, 5),
            pl.BlockSpec((tr, 128), lambda b, c: (c, 0)),
            pl.BlockSpec((tr, 128), lambda b, c: (c, 0)),
            _layer_spec(cw, l), _layer_spec(cb, l), _layer_spec(dtb, l), _layer_spec(alog, l),
            _layer_spec(dsk, l), _layer_spec(nw, l), _layer_spec(scw, l), _layer_spec(qw, l), _layer_spec(kw, l),
        ] + mod_in,
        out_specs=[
            blk(D_SSD, 0), blk(D_CONV, 0),
            per_batch(D_SSD, D_STATE), per_batch(8, D_XBC), per_batch(8, D_CONV),
            blk(D_ATTN, 0), per_batch(CHUNK, 128), per_batch(CHUNK, 128),
        ] + mod_out,
        out_shape=[
            jax.ShapeDtypeStruct((m, D_SSD), BF16),
            jax.ShapeDtypeStruct((m, D_CONV), BF16),
            jax.ShapeDtypeStruct((nb, D_SSD, D_STATE), F32),
            jax.ShapeDtypeStruct((nb, 8, D_XBC), F32),
            jax.ShapeDtypeStruct((nb, 8, D_CONV), F32),
            jax.ShapeDtypeStruct((m, D_ATTN), BF16),
            jax.ShapeDtypeStruct((nb, CHUNK, 128), F32),
            jax.ShapeDtypeStruct((nb, CHUNK, 128), F32),
        ] + mod_shape,
        scratch_shapes=[
            pltpu.VMEM((8 + CHUNK, D_XBC), F32),
            pltpu.VMEM((8 + CHUNK, D_CONV), F32),
            pltpu.VMEM((D_STATE, D_SSD), F32),
            pltpu.VMEM((CHUNK, 128), F32),
            pltpu.VMEM((CHUNK, 128), F32),
        ],
        compiler_params=_cparams(("arbitrary", "arbitrary")),
        name="mixer_prompt",
    )(sinks, proj, proj, proj, proj, cos_t, sin_t, cw, cb, dtb, alog, dsk, nw, scw, qw, kw, *mod_args)


def _outproj_rows(ys_ref, o_ref, yc_ref, x_ref, gate_ref, sc_ref, sh_ref, w_ref, nw_ref, x1_ref, h2_ref):
    n = x_ref.shape[0]
    sub = min(n, 256)
    for r in range(n // sub):
        rs = slice(r * sub, (r + 1) * sub)
        gs = rs if gate_ref.shape[0] == n else slice(None)
        acc = _dot(ys_ref[rs, :], w_ref[0:D_SSD, :])
        acc += _dot(o_ref[rs, :], w_ref[D_SSD:D_SSD + D_ATTN, :])
        acc += _dot(yc_ref[rs, :], w_ref[D_SSD + D_ATTN:, :])
        x1 = x_ref[rs, :] + gate_ref[gs, :] * acc
        x1_ref[rs, :] = x1
        h2_ref[rs, :] = _rms_mod(x1, nw_ref[...], sc_ref[gs, :], sh_ref[gs, :]).astype(BF16)


def _outproj_kernel(*refs):
    p_in, s_in, (w_ref, nw_ref), p_out, s_out = refs[0:7], refs[7:14], refs[14:16], refs[16:18], refs[18:20]

    @pl.when(pl.program_id(0) == 0)
    def _():
        _outproj_rows(*s_in, w_ref, nw_ref, *s_out)

    _outproj_rows(*p_in, w_ref, nw_ref, *p_out)


def _row_specs(widths, tm):
    return [pl.BlockSpec((tm, w), lambda i: (i, 0)) for w in widths]


def _const_specs(widths, rows):
    return [pl.BlockSpec((rows, w), lambda i: (0, 0)) for w in widths]


def _outproj(mix_p, x_p, mod_p, mix_s, x_s, mod_s, w_all, l, nw2_all, tm, rpb):
    m, ms = x_p.shape[0], x_s.shape[0]
    widths = (D_SSD, D_ATTN, D_CONV, D_MODEL)
    mods = lambda mod, t, r: [_mod_spec(mod, 2, t, r), _mod_spec(mod, 4, t, r), _mod_spec(mod, 3, t, r)]
    return pl.pallas_call(
        _outproj_kernel,
        grid=(m // tm,),
        in_specs=(_row_specs(widths, tm) + mods(mod_p, tm, rpb)
                  + _const_specs(widths, ms) + mods(mod_s, 0, 1)
                  + [pl.BlockSpec((D_MODEL, D_MODEL), lambda i: (0, 0), pipeline_mode=pl.Buffered(1)),
                     _layer_spec(nw2_all, l)]),
        out_specs=_row_specs((D_MODEL, D_MODEL), tm) + _const_specs((D_MODEL, D_MODEL), ms),
        out_shape=[
            jax.ShapeDtypeStruct((m, D_MODEL), F32), jax.ShapeDtypeStruct((m, D_MODEL), BF16),
            jax.ShapeDtypeStruct((ms, D_MODEL), F32), jax.ShapeDtypeStruct((ms, D_MODEL), BF16),
        ],
        compiler_params=_cparams(("arbitrary",)),
        name="out_proj",
    )(*mix_p, x_p, mod_p, mod_p, mod_p, *mix_s, x_s, mod_s, mod_s, mod_s, w_all, nw2_all)


def _mix_out_kernel(*refs, layer, with_mod, nc, nsteps):
    it = iter(refs)
    take = lambda n: [next(it) for _ in range(n)]
    (sink_ref, zx_ref, g_ref, kvdt_ref, q_ref, cos_ref, sin_ref,
     cw_ref, cb_ref, dtb_ref, alog_ref, dsk_ref, nw_ref, scw_ref, qw_ref, kw_ref) = take(16)
    x_ref, gate_ref, sc_ref, sh_ref = take(4)
    sample_in = take(7)
    w_ref, n2_ref = take(2)
    if with_mod:
        a_ref, wada_ref, bada_ref = take(3)
    st_out_ref, cst_ref, ust_ref, kwin_ref, vwin_ref, x1_ref, h2_ref, x1s_ref, h2s_ref = take(9)
    if with_mod:
        (modn_ref,) = take(1)
        modn_ref[...] = _dot(a_ref[...], wada_ref[...].astype(BF16)) + bada_ref[...]
    y_ref, yc_ref, o_ref, xpad_ref, upad_ref, st_ref, kprev_ref, vprev_ref = take(8)
    s = pl.program_id(0)
    sm = jnp.minimum(s, nsteps - 1)
    c = sm % nc
    cur = s % 2
    prev = 1 - cur

    @pl.when((c == 0) & (s < nsteps))
    def _():
        xpad_ref[0:8, :] = jnp.zeros((8, D_XBC), F32)
        upad_ref[0:8, :] = jnp.zeros((8, D_CONV), F32)
        st_ref[...] = jnp.zeros_like(st_ref)
        kprev_ref[...] = jnp.zeros_like(kprev_ref)
        vprev_ref[...] = jnp.zeros_like(vprev_ref)

    @pl.when(s == 0)
    def _():
        y_ref[...] = jnp.zeros_like(y_ref)
        yc_ref[...] = jnp.zeros_like(yc_ref)
        o_ref[...] = jnp.zeros_like(o_ref)
        _outproj_rows(*sample_in, w_ref, n2_ref, x1s_ref, h2s_ref)

    nparts = 4 * MIX_CHUNKS
    wcol = D_MODEL // nparts

    def proj_part(k):
        cols = slice(k * wcol, (k + 1) * wcol)
        acc = _dot(y_ref[prev], w_ref[0:D_SSD, cols])
        acc += _dot(o_ref[prev], w_ref[D_SSD:D_SSD + D_ATTN, cols])
        acc += _dot(yc_ref[prev], w_ref[D_SSD + D_ATTN:, cols])
        x1_ref[:, cols] = acc

    for r in range(MIX_CHUNKS):
        rows = lambda ref: ref.at[r * CHUNK:(r + 1) * CHUNK]
        ssd = _ssd_body(rows(zx_ref), rows(g_ref), rows(kvdt_ref), cw_ref, cb_ref, dtb_ref, alog_ref, dsk_ref,
                        nw_ref, scw_ref, rows(y_ref.at[cur]), rows(yc_ref.at[cur]), cst_ref, ust_ref,
                        xpad_ref, upad_ref, st_ref)
        attn = _attn_body(sink_ref, rows(q_ref), rows(kvdt_ref), rows(cos_ref), rows(sin_ref), qw_ref,
                          kw_ref, rows(o_ref.at[cur]), kprev_ref, vprev_ref, layer=layer,
                          has_prev=(c > 0) if r == 0 else True)
        for k, stage in enumerate((ssd, ssd, attn, attn)):
            proj_part(4 * r + k)
            next(stage, None)
    x1 = x_ref[...] + gate_ref[...] * x1_ref[...]
    x1_ref[...] = x1
    h2_ref[...] = _rms_mod(x1, n2_ref[...], sc_ref[...], sh_ref[...]).astype(BF16)
    kwin_ref[...] = kprev_ref[...]
    vwin_ref[...] = vprev_ref[...]

    @pl.when((c == nc - 1) & (s < nsteps))
    def _():
        st_out_ref[...] = st_ref[...].T


def _mix_out(proj, x_p, mod_p, mix_s, x_s, mod_s, w_out_l, n2_all, nb, seq, l, sinks, cos_t, sin_t,
             cw, cb, dtb, alog, dsk, nw, scw, qw, kw, mod_next=None):
    tr = MIX_CHUNKS * CHUNK
    nc = seq // tr
    m = nb * seq
    ms = x_s.shape[0]
    nsteps = nb * nc
    cur = lambda s: jnp.minimum(s, nsteps - 1)
    prv = lambda s: jnp.maximum(s - 1, 0)
    blk = lambda width, col: pl.BlockSpec((tr, width), lambda s: (cur(s), col))
    prv_blk = lambda width: pl.BlockSpec((tr, width), lambda s: (prv(s), 0))
    per_batch = lambda rows, width: pl.BlockSpec((None, rows, width), lambda s: (cur(s) // nc, 0, 0))
    const = lambda rows, width: pl.BlockSpec((rows, width), lambda s: (0, 0))
    mod_blk = lambda mod, j, bsel: pl.BlockSpec((None, mod.shape[1], D_MODEL), lambda s: (bsel(s), 0, j))
    mods = lambda mod, bsel: [mod_blk(mod, 2, bsel), mod_blk(mod, 4, bsel), mod_blk(mod, 3, bsel)]
    with_mod = mod_next is not None
    mod_in, mod_out, mod_shape, mod_args = [], [], [], []
    if with_mod:
        a16, w_ada, b_ada3 = mod_next
        r, d = a16.shape
        n = w_ada.shape[2]
        cw_mod = n // nsteps
        assert cw_mod * nsteps == n and cw_mod % 128 == 0
        mod_in = [const(r, d),
                  pl.BlockSpec((None, d, cw_mod), lambda s: (l + 1, 0, cur(s))),
                  pl.BlockSpec((None, 1, cw_mod), lambda s: (l + 1, 0, cur(s)))]
        mod_out = [pl.BlockSpec((r, cw_mod), lambda s: (0, cur(s)))]
        mod_shape = [jax.ShapeDtypeStruct((r, n), F32)]
        mod_args = [a16, w_ada, b_ada3]
    return pl.pallas_call(
        functools.partial(_mix_out_kernel, layer=l, with_mod=with_mod, nc=nc, nsteps=nsteps),
        grid=(nsteps + 1,),
        in_specs=([
            pl.BlockSpec(memory_space=pltpu.SMEM),
            blk(2560, 0), blk(1536, 2), blk(512, 9), blk(512, 5),
            pl.BlockSpec((tr, 128), lambda s: (cur(s) % nc, 0)),
            pl.BlockSpec((tr, 128), lambda s: (cur(s) % nc, 0)),
            _layer_spec(cw, l), _layer_spec(cb, l), _layer_spec(dtb, l), _layer_spec(alog, l),
            _layer_spec(dsk, l), _layer_spec(nw, l), _layer_spec(scw, l), _layer_spec(qw, l), _layer_spec(kw, l),
            prv_blk(D_MODEL)] + mods(mod_p, lambda s: prv(s) // nc)
            + [const(ms, D_SSD), const(ms, D_ATTN), const(ms, D_CONV), const(ms, D_MODEL)]
            + mods(mod_s, lambda s: 0)
            + [pl.BlockSpec((D_MODEL, D_MODEL), lambda s: (0, 0), pipeline_mode=pl.Buffered(1)),
               _layer_spec(n2_all, l)]
            + mod_in),
        out_specs=[
            per_batch(D_SSD, D_STATE), per_batch(8, D_XBC), per_batch(8, D_CONV),
            per_batch(CHUNK, 128), per_batch(CHUNK, 128),
            prv_blk(D_MODEL), prv_blk(D_MODEL), const(ms, D_MODEL), const(ms, D_MODEL),
        ] + mod_out,
        out_shape=[
            jax.ShapeDtypeStruct((nb, D_SSD, D_STATE), F32),
            jax.ShapeDtypeStruct((nb, 8, D_XBC), F32),
            jax.ShapeDtypeStruct((nb, 8, D_CONV), F32),
            jax.ShapeDtypeStruct((nb, CHUNK, 128), F32),
            jax.ShapeDtypeStruct((nb, CHUNK, 128), F32),
            jax.ShapeDtypeStruct((m, D_MODEL), F32), jax.ShapeDtypeStruct((m, D_MODEL), BF16),
            jax.ShapeDtypeStruct((ms, D_MODEL), F32), jax.ShapeDtypeStruct((ms, D_MODEL), BF16),
        ] + mod_shape,
        scratch_shapes=[
            pltpu.VMEM((2, tr, D_SSD), BF16), pltpu.VMEM((2, tr, D_CONV), BF16), pltpu.VMEM((2, tr, D_ATTN), BF16),
            pltpu.VMEM((8 + CHUNK, D_XBC), F32),
            pltpu.VMEM((8 + CHUNK, D_CONV), F32),
            pltpu.VMEM((D_STATE, D_SSD), F32),
            pltpu.VMEM((CHUNK, 128), F32),
            pltpu.VMEM((CHUNK, 128), F32),
        ],
        compiler_params=_cparams(("arbitrary",)),
        name="mix_out",
    )(sinks, proj, proj, proj, proj, cos_t, sin_t, cw, cb, dtb, alog, dsk, nw, scw, qw, kw,
      x_p, mod_p, mod_p, mod_p, *mix_s, x_s, mod_s, mod_s, mod_s, w_out_l, n2_all, *mod_args)


def _up_seq_kernel(h_ref, hs_ref, wg_ref, wu_ref, cw_ref, cb_ref, prev_ref, wd_ref,
                   a_ref, gst_ref, as_ref, gts_ref, wd16_ref, pad_ref, wg16_ref, wu16_ref, *, tm, rpb):
    i = pl.program_id(1)
    cw = cw_ref[...]
    cb = cb_ref[...]
    wd16_ref[...] = wd_ref[...].astype(BF16)

    @pl.when(i == 0)
    def _():
        wg16_ref[...] = wg_ref[...].astype(BF16)
        wu16_ref[...] = wu_ref[...].astype(BF16)
        hs = hs_ref[...]
        gt = _dot(hs, wg16_ref[...])
        u = _dot(hs, wu16_ref[...])
        conv = cw[2:3] * gt + cw[1:2] * prev_ref[1] + cw[0:1] * prev_ref[0]
        gts_ref[...] = gt
        as_ref[...] = (_silu(conv + cb) * u).astype(BF16)

    @pl.when((i * tm) % rpb == 0)
    def _():
        pad_ref[0:8, :] = jnp.zeros((8, pad_ref.shape[1]), F32)

    sub = 512
    for r in range(tm // sub):
        h = h_ref[r * sub:(r + 1) * sub, :]
        gt = _dot(h, wg16_ref[...])
        u = _dot(h, wu16_ref[...])
        base = 8 + r * sub
        pad_ref[base:base + sub, :] = gt
        conv = (cw[2:3] * gt + cw[1:2] * pad_ref[base - 1:base - 1 + sub, :]
                + cw[0:1] * pad_ref[base - 2:base - 2 + sub, :])
        a_ref[r * sub:(r + 1) * sub, :] = (_silu(conv + cb) * u).astype(BF16)
    tail = pad_ref[tm:tm + 8, :]
    pad_ref[0:8, :] = tail
    gst_ref[...] = tail


def _up_proj(h, hs, w_up_all, l, cw_all, cb_all, prev_all, w_down_all, nb, tm, tn):
    m = h.shape[0]
    ms = hs.shape[0]
    rpb = m // nb
    nt = D_FF // tn
    ni = m // tm
    slab = D_FF // (nt * ni)
    assert slab * nt * ni == D_FF and slab % 16 == 0
    return pl.pallas_call(
        functools.partial(_up_seq_kernel, tm=tm, rpb=rpb),
        grid=(nt, ni),
        in_specs=[
            pl.BlockSpec((tm, D_MODEL), lambda j, i: (i, 0)),
            pl.BlockSpec((ms, D_MODEL), lambda j, i: (0, 0)),
            pl.BlockSpec((None, D_MODEL, tn), lambda j, i: (l, 0, j)),
            pl.BlockSpec((None, D_MODEL, tn), lambda j, i: (l, 0, j + nt)),
            pl.BlockSpec((None, 3, tn), lambda j, i: (l, 0, j)),
            pl.BlockSpec((None, 1, tn), lambda j, i: (l, 0, j)),
            pl.BlockSpec((None, 2, ms, tn), lambda j, i: (l, 0, 0, j)),
            pl.BlockSpec((None, slab, D_MODEL), lambda j, i: (l, j * ni + i, 0)),
        ],
        out_specs=[
            pl.BlockSpec((tm, tn), lambda j, i: (i, j)),
            pl.BlockSpec((None, 8, tn), lambda j, i: ((i * tm) // rpb, 0, j)),
            pl.BlockSpec((ms, tn), lambda j, i: (0, j)),
            pl.BlockSpec((ms, tn), lambda j, i: (0, j)),
            pl.BlockSpec((slab, D_MODEL), lambda j, i: (j * ni + i, 0)),
        ],
        out_shape=[
            jax.ShapeDtypeStruct((m, D_FF), BF16),
            jax.ShapeDtypeStruct((nb, 8, D_FF), F32),
            jax.ShapeDtypeStruct((ms, D_FF), BF16),
            jax.ShapeDtypeStruct((ms, D_FF), F32),
            jax.ShapeDtypeStruct((D_FF, D_MODEL), BF16),
        ],
        scratch_shapes=[
            pltpu.VMEM((8 + tm, tn), F32),
            pltpu.VMEM((D_MODEL, tn), BF16),
            pltpu.VMEM((D_MODEL, tn), BF16),
        ],
        compiler_params=_cparams(("arbitrary", "arbitrary")),
        name="up_proj",
    )(h, hs, w_up_all, w_up_all, cw_all, cb_all, prev_all, w_down_all)


def _down_rows(a_ref, x_ref, gate_ref, mods, w_ref, nw_ref, x2_ref, hn_ref):
    x2 = x_ref[...] + gate_ref[...] * _dot(a_ref[...], w_ref[...])
    x2_ref[...] = x2
    if hn_ref is not None:
        hn_ref[...] = _rms_mod(x2, nw_ref[...], mods[0][...], mods[1][...]).astype(BF16)


def _down_kernel(*refs, with_norm):
    refs = list(refs)
    n_in = 5 if with_norm else 3
    p_in, s_in = refs[0:n_in], refs[n_in:2 * n_in]
    w_ref = refs[2 * n_in]
    nw_ref = refs[2 * n_in + 1] if with_norm else None
    outs = refs[2 * n_in + (2 if with_norm else 1):]
    p_out = (outs[0], outs[1]) if with_norm else (outs[0], None)
    s_out = (outs[2], outs[3]) if with_norm else (outs[1], None)

    @pl.when(pl.program_id(0) == 0)
    def _():
        _down_rows(s_in[0], s_in[1], s_in[2], s_in[3:], w_ref, nw_ref, *s_out)

    _down_rows(p_in[0], p_in[1], p_in[2], p_in[3:], w_ref, nw_ref, *p_out)


def _down(a_p, x_p, mod_p, a_s, x_s, mod_s, w_all, l, nw1_all, tm, rpb):
    m, ms = x_p.shape[0], x_s.shape[0]
    with_norm = l + 1 < nw1_all.shape[0]

    def mods(mod, t, r):
        sp = [_mod_spec(mod[0], 5, t, r)]
        if with_norm:
            sp += [_mod_spec(mod[1], 1, t, r), _mod_spec(mod[1], 0, t, r)]
        return sp

    def mod_args(mod):
        return [mod[0], mod[1], mod[1]] if with_norm else [mod[0]]

    in_specs = (_row_specs((D_FF, D_MODEL), tm) + mods(mod_p, tm, rpb)
                + _const_specs((D_FF, D_MODEL), ms) + mods(mod_s, 0, 1)
                + [pl.BlockSpec((D_FF, D_MODEL), lambda i: (0, 0), pipeline_mode=pl.Buffered(1))])
    args = [a_p, x_p] + mod_args(mod_p) + [a_s, x_s] + mod_args(mod_s) + [w_all]
    out_w = (D_MODEL, D_MODEL) if with_norm else (D_MODEL,)
    out_dt = (F32, BF16) if with_norm else (F32,)
    if with_norm:
        in_specs.append(_layer_spec(nw1_all, l + 1))
        args.append(nw1_all)
    res = pl.pallas_call(
        functools.partial(_down_kernel, with_norm=with_norm),
        grid=(m // tm,),
        in_specs=in_specs,
        out_specs=_row_specs(out_w, tm) + _const_specs(out_w, ms),
        out_shape=([jax.ShapeDtypeStruct((m, D_MODEL), d) for d in out_dt]
                   + [jax.ShapeDtypeStruct((ms, D_MODEL), d) for d in out_dt]),
        compiler_params=_cparams(("arbitrary",)),
        name="down_proj",
    )(*args)
    if with_norm:
        return res[0], res[1], res[2], res[3]
    return res[0], None, res[1], None


def _expand_matrix():
    r = lax.broadcasted_iota(jnp.int32, (128, D_SSD), 0)
    c = lax.broadcasted_iota(jnp.int32, (128, D_SSD), 1)
    return jnp.where(r == (c >> 6), 1.0, 0.0).astype(BF16)


def _ssd_step_kernel(p_ref, st_ref, cprev_ref, sprev_ref, cw_ref, cb_ref, dtb_ref, alog_ref, dsk_ref, nw_ref,
                     scw_ref, y_ref, yc_ref, u_ref, stn_ref):
    nbb = p_ref.shape[0]
    z = p_ref[:, C_Z:C_Z + D_SSD]
    raw = p_ref[:, C_XBC:C_XBC + D_XBC]
    cw = cw_ref[...]
    conv = cw[3:4] * raw + cw[2:3] * cprev_ref[2] + cw[1:2] * cprev_ref[1] + cw[0:1] * cprev_ref[0]
    xbc = _silu(conv + cb_ref[...])
    xs = xbc[:, :D_SSD]
    bm = xbc[:, D_SSD:D_SSD + 256]
    cm = xbc[:, D_SSD + 256:]

    lane = lax.broadcasted_iota(jnp.int32, (1, 128), 1)
    a_row = jnp.where(lane < SSM_HEADS, -jnp.exp(alog_ref[...]), 0.0)
    dtr = p_ref[:, C_DT:C_DT + 128] + dtb_ref[...]
    dt = jnp.maximum(dtr, 0.0) + jnp.log1p(jnp.exp(-jnp.abs(dtr)))
    dec = jnp.exp(dt * a_row)
    em = _expand_matrix()

    def expand(v):
        v1, v2, v3 = _split3(v)
        return _dot(v1, em) + _dot(v2, em) + _dot(v3, em)

    xdt_t = (xs * expand(dt)).T
    dec_t = expand(dec).T
    rowid = lax.broadcasted_iota(jnp.int32, (nbb, 512), 0)
    ysum = [jnp.zeros((nbb, 512), F32), jnp.zeros((nbb, 512), F32)]
    for i in range(nbb):
        for g in range(2):
            rs = slice(512 * g, 512 * (g + 1))
            hn = st_ref[i, rs, :] * dec_t[rs, i:i + 1] + xdt_t[rs, i:i + 1] * bm[i:i + 1, 128 * g:128 * (g + 1)]
            stn_ref[i, rs, :] = hn
            r = _dot_nt(cm[:, 128 * g:128 * (g + 1)].astype(BF16), hn.astype(BF16))
            ysum[g] = ysum[g] + jnp.where(rowid == i, r, 0.0)
    y = jnp.concatenate(ysum, axis=1) + dsk_ref[...] * xs
    y = y * _silu(z)
    nw = nw_ref[...]
    outs = []
    for g in range(2):
        yg = y[:, 512 * g:512 * (g + 1)]
        ms = jnp.mean(yg * yg, axis=-1, keepdims=True)
        outs.append(yg * lax.rsqrt(ms + EPS) * nw[:, 512 * g:512 * (g + 1)])
    y_ref[...] = jnp.concatenate(outs, axis=1).astype(BF16)

    u = p_ref[:, C_G + D_CONV:C_G + 2 * D_CONV] * p_ref[:, C_G + 2 * D_CONV:C_G + 3 * D_CONV]
    sw = scw_ref[...]
    sconv = sw[2:3] * u + sw[1:2] * sprev_ref[1] + sw[0:1] * sprev_ref[0]
    u_ref[...] = u
    yc_ref[...] = (p_ref[:, C_G:C_G + D_CONV] * sconv).astype(BF16)


def _ssd_step(proj, state_all, cprev_all, sprev_all, l, cw, cb, dtb, alog, dsk, nw, scw, nbb=8):
    nb = proj.shape[0]
    return pl.pallas_call(
        _ssd_step_kernel,
        grid=(nb // nbb,),
        in_specs=[
            pl.BlockSpec((nbb, PROJ_P), lambda i: (i, 0)),
            pl.BlockSpec((None, nbb, D_SSD, D_STATE), lambda i: (l, i, 0, 0)),
            pl.BlockSpec((None, 3, nbb, D_XBC), lambda i: (l, 0, i, 0)),
            pl.BlockSpec((None, 2, nbb, D_CONV), lambda i: (l, 0, i, 0)),
            _layer_spec(cw, l), _layer_spec(cb, l), _layer_spec(dtb, l), _layer_spec(alog, l),
            _layer_spec(dsk, l), _layer_spec(nw, l), _layer_spec(scw, l),
        ],
        out_specs=[
            pl.BlockSpec((nbb, D_SSD), lambda i: (i, 0)),
            pl.BlockSpec((nbb, D_CONV), lambda i: (i, 0)),
            pl.BlockSpec((nbb, D_CONV), lambda i: (i, 0)),
            pl.BlockSpec((nbb, D_SSD, D_STATE), lambda i: (i, 0, 0)),
        ],
        out_shape=[
            jax.ShapeDtypeStruct((nb, D_SSD), BF16),
            jax.ShapeDtypeStruct((nb, D_CONV), BF16),
            jax.ShapeDtypeStruct((nb, D_CONV), F32),
            jax.ShapeDtypeStruct((nb, D_SSD, D_STATE), F32),
        ],
        compiler_params=_cparams(("arbitrary",)),
        name="ssd_step",
    )(proj, state_all, cprev_all, sprev_all, cw, cb, dtb, alog, dsk, nw, scw)


def _qk_step_kernel(p_ref, cos_ref, sin_ref, qw_ref, kw_ref, q_ref, k_ref):
    bd = _seg_mean_matrix()
    lane = lax.broadcasted_iota(jnp.int32, (p_ref.shape[0], 128), 1)
    first_half = (lane & 32) == 0
    cos = cos_ref[...]
    sin = sin_ref[...]
    for t in range(4):
        qn = _head_norm(p_ref[:, C_Q + 128 * t:C_Q + 128 * (t + 1)], qw_ref[...], bd)
        q_ref[:, 128 * t:128 * (t + 1)] = _rope(qn, cos, sin, first_half) * (HEAD_DIM ** -0.5)
    k_ref[...] = _rope(_head_norm(p_ref[:, C_K:C_K + 128], kw_ref[...], bd), cos, sin, first_half)


def _qk_step(proj, cos_r, sin_r, l, qw, kw):
    nb = proj.shape[0]
    full = lambda a: pl.BlockSpec(a.shape, lambda i: (0,) * a.ndim)
    return pl.pallas_call(
        _qk_step_kernel,
        grid=(1,),
        in_specs=[full(proj), full(cos_r), full(sin_r), _layer_spec(qw, l), _layer_spec(kw, l)],
        out_specs=[pl.BlockSpec((nb, D_ATTN), lambda i: (0, 0)), pl.BlockSpec((nb, 128), lambda i: (0, 0))],
        out_shape=[jax.ShapeDtypeStruct((nb, D_ATTN), F32), jax.ShapeDtypeStruct((nb, 128), F32)],
        name="qk_step",
    )(proj, cos_r, sin_r, qw, kw)


def _attn_step_kernel(q_ref, kc_ref, vc_ref, knew_ref, vnew_ref, sink_ref, o_ref):
    q = q_ref[...]
    s = lax.dot_general(q.astype(BF16), kc_ref[...].astype(BF16), (((2,), (1,)), ((0,), (0,))),
                        preferred_element_type=F32)
    s_new = jnp.sum(q * knew_ref[...], axis=-1, keepdims=True)
    sink = sink_ref[...]
    mx = jnp.maximum(jnp.maximum(jnp.max(s, axis=-1, keepdims=True), s_new), sink)
    p = jnp.exp(s - mx)
    p_new = jnp.exp(s_new - mx)
    den = jnp.sum(p, axis=-1, keepdims=True) + p_new + jnp.exp(sink - mx)
    o = lax.dot_general((p / den).astype(BF16), vc_ref[...].astype(BF16), (((2,), (2,)), ((0,), (0,))),
                        preferred_element_type=F32)
    o_ref[...] = o + (p_new / den) * vnew_ref[...]


def _attn_step(q3, kc_all, vc_all, l, knew, vnew, sink_all):
    nb = q3.shape[0]
    full = lambda a: pl.BlockSpec(a.shape, lambda i: (0,) * a.ndim)
    return pl.pallas_call(
        _attn_step_kernel,
        grid=(1,),
        in_specs=[full(q3), _layer_spec(kc_all, l), _layer_spec(vc_all, l), full(knew), full(vnew),
                  _layer_spec(sink_all, l)],
        out_specs=pl.BlockSpec((nb, ATT_HEADS, 128), lambda i: (0, 0, 0)),
        out_shape=jax.ShapeDtypeStruct((nb, ATT_HEADS, 128), F32),
        name="attn_step",
    )(q3, kc_all, vc_all, knew, vnew, sink_all)


def _rope_tables(pos):
    half = HEAD_DIM // 2
    inv = np.power(ROPE_THETA, -np.arange(half, dtype=np.float64) / half)
    ang = np.asarray(pos, np.float64)[:, None] * inv[None, :]
    cos = np.cos(ang)
    sin = np.sin(ang)
    cos_t = np.concatenate([cos, cos, cos, cos], axis=-1).astype(np.float32)
    sin_t = np.concatenate([-sin, sin, -sin, sin], axis=-1).astype(np.float32)
    return jnp.asarray(cos_t), jnp.asarray(sin_t)


def kernel(x_prompt, x_sample, state_ssm, state_ssm_conv, cache_win_k, cache_win_v, state_short_conv,
           state_ffn_conv, c_prompt, c_sample, norm1_w, norm2_w, w_ada, b_ada, w_in, ssm_conv_w, ssm_conv_b,
           dt_bias, a_log, d_skip, ssm_norm_w, q_norm_w, k_norm_w, sinks, sconv_w, w_out, w_up, ffn_conv_w,
           ffn_conv_b, w_down):
    nbp, seq, _ = x_prompt.shape
    nbs = x_sample.shape[0]
    depth = w_in.shape[0]
    mp = nbp * seq
    nbuf = cache_win_k.shape[2]

    in_tab = _in_proj_table()
    w_in_t = jnp.swapaxes(w_in, 1, 2)

    n1 = norm1_w.reshape(depth, 1, D_MODEL)
    n2 = norm2_w.reshape(depth, 1, D_MODEL)
    cb = ssm_conv_b.reshape(depth, 1, D_XBC)
    dtb = jnp.pad(dt_bias, ((0, 0), (0, 128 - SSM_HEADS))).reshape(depth, 1, 128)
    alog = jnp.pad(a_log, ((0, 0), (0, 128 - SSM_HEADS))).reshape(depth, 1, 128)
    dsk = jnp.repeat(d_skip, SSM_HEAD_DIM, axis=1).reshape(depth, 1, D_SSD)
    nw = ssm_norm_w.reshape(depth, 1, D_SSD)
    qw = jnp.tile(q_norm_w, (1, 2)).reshape(depth, 1, 128)
    kw = jnp.tile(k_norm_w, (1, 2)).reshape(depth, 1, 128)
    fcb = ffn_conv_b.reshape(depth, 1, D_FF)
    sink3 = sinks.reshape(depth, ATT_HEADS, 1)

    st_all = state_ssm.reshape(depth, nbs, D_SSD, D_STATE)
    cprev = jnp.transpose(state_ssm_conv, (0, 2, 1, 3))
    sprev = jnp.transpose(state_short_conv, (0, 2, 1, 3))
    fprev = jnp.transpose(state_ffn_conv, (0, 2, 1, 3))
    ck = cache_win_k.reshape(depth, nbs, nbuf, 128)
    cv = cache_win_v.reshape(depth, nbs, nbuf, 128)
    ck_t = jnp.swapaxes(ck, 2, 3)
    cv_t = jnp.swapaxes(cv, 2, 3)

    c_all = jnp.concatenate([c_prompt, jnp.zeros((8 - nbp, D_MODEL), F32), c_sample], axis=0)
    b_ada3 = b_ada.reshape(depth, 1, 6 * D_MODEL)
    mod0, a16 = _modulation_first(c_all, w_ada, b_ada3)

    def split_mod(mod):
        return mod[:nbp].reshape(nbp, 1, 6 * D_MODEL), mod[8:].reshape(1, nbs, 6 * D_MODEL)

    mod_p, mod_s = [None] * (depth + 1), [None] * (depth + 1)
    mod_p[0], mod_s[0] = split_mod(mod0)

    cos_p, sin_p = _rope_tables(np.arange(seq))
    cos_s, sin_s = _rope_tables(np.full((1,), PAST_LEN))

    head_g = jnp.arange(ATT_HEADS) // (ATT_HEADS // KV_HEADS)
    q_place = (head_g[:, None] == jnp.arange(KV_HEADS)[None, :]).astype(F32)[None, :, :, None]

    xp = x_prompt.reshape(mp, D_MODEL)
    xs = x_sample.reshape(nbs, D_MODEL)
    TM = 512
    hp = _norm_mod(xp, n1, mod_p[0], 0, 1, 0, TM, seq)
    hs = _norm_mod(xs, n1, mod_s[0], 0, 1, 0, nbs, nbs)

    outs_p = [[] for _ in range(6)]
    outs_s = [[] for _ in range(6)]
    for l in range(depth):
        proj, projs, w_out_l = _in_proj(hp, hs, w_in_t, l, in_tab, w_out, min(1024, seq))

        y_s, yc_s, u_s, st_new = _ssd_step(projs, st_all, cprev, sprev, l, ssm_conv_w, cb, dtb, alog, dsk, nw,
                                           sconv_w)
        q_r, k_r = _qk_step(projs, cos_s, sin_s, l, qw, kw)
        v_r = projs[:, C_V:C_V + 128]
        q3 = (q_r.reshape(nbs, ATT_HEADS, 1, HEAD_DIM) * q_place).reshape(nbs, ATT_HEADS, 128)
        o3 = _attn_step(q3, ck_t, cv_t, l, k_r[:, None, :], v_r[:, None, :], sink3)
        o4 = o3.reshape(nbs, ATT_HEADS, KV_HEADS, HEAD_DIM)
        o_s = jnp.concatenate([o4[:, :4, 0], o4[:, 4:, 1]], axis=1).reshape(nbs, D_ATTN).astype(BF16)

        mixed = _mix_out(proj, xp, mod_p[l], (y_s, o_s, yc_s), xs, mod_s[l], w_out_l, n2, nbp, seq, l,
                         sinks, cos_p, sin_p, ssm_conv_w, cb, dtb, alog, dsk, nw, sconv_w, qw, kw,
                         mod_next=(a16, w_ada, b_ada3) if l + 1 < depth else None)
        st_t, cst, ust, kwin, vwin, xp, h2, xs, h2s = mixed[:9]
        if l + 1 < depth:
            mod_p[l + 1], mod_s[l + 1] = split_mod(mixed[9])
        act, gst, act_s, gt_s, w_down_l = _up_proj(h2, h2s, w_up, l, ffn_conv_w, fcb, fprev, w_down, nbp,
                                                   min(2048, seq), 512)
        xp, hp, xs, hs = _down(act, xp, mod_p[l:l + 2], act_s, xs, mod_s[l:l + 2], w_down_l, l, n1, 256, seq)

        outs_p[0].append(st_t.reshape(nbp, SSM_HEADS, SSM_HEAD_DIM, D_STATE))
        outs_p[1].append(cst[:, 8 - (SSM_CONV - 1):])
        outs_p[2].append(kwin.reshape(nbp, WINDOW, KV_HEADS, HEAD_DIM))
        outs_p[3].append(vwin.reshape(nbp, WINDOW, KV_HEADS, HEAD_DIM))
        outs_p[4].append(ust[:, 6:])
        outs_p[5].append(gst[:, 6:])

        outs_s[0].append(st_new.reshape(nbs, SSM_HEADS, SSM_HEAD_DIM, D_STATE))
        outs_s[1].append(projs[:, C_XBC:C_XBC + D_XBC])
        outs_s[2].append(k_r)
        outs_s[3].append(v_r)
        outs_s[4].append(u_s)
        outs_s[5].append(gt_s)

    res_p = [jnp.stack(a, axis=0) for a in outs_p]
    new_rows = [jnp.stack(a, axis=0) for a in outs_s]
    s_ssm = new_rows[0]
    s_ssm_conv = jnp.concatenate([state_ssm_conv[:, :, 1:], new_rows[1][:, :, None]], axis=2)
    s_win_k = jnp.concatenate([ck[:, :, 1:], new_rows[2][:, :, None]], axis=2).reshape(cache_win_k.shape)
    s_win_v = jnp.concatenate([cv[:, :, 1:], new_rows[3][:, :, None]], axis=2).reshape(cache_win_v.shape)
    s_short = jnp.concatenate([state_short_conv[:, :, 1:], new_rows[4][:, :, None]], axis=2)
    s_ffn = jnp.concatenate([state_ffn_conv[:, :, 1:], new_rows[5][:, :, None]], axis=2)
    return (xp.reshape(nbp, seq, D_MODEL), xs.reshape(nbs, 1, D_MODEL), *res_p,
            s_ssm, s_ssm_conv, s_win_k, s_win_v, s_short, s_ffn)
```

```python
import functools

import jax
import jax.numpy as jnp
import numpy as np
from jax import lax
from jax.experimental import pallas as pl
from jax.experimental.pallas import tpu as pltpu

F32 = jnp.float32
BF16 = jnp.bfloat16

D_MODEL = 2048
D_SSD = 1024
SSM_HEADS = 16
SSM_HEAD_DIM = 64
D_STATE = 128
SSM_CONV = 4
CHUNK = 128
D_XBC = 1536
HEAD_DIM = 64
D_ATTN = 512
ATT_HEADS = 8
KV_HEADS = 2
WINDOW = 128
ROPE_THETA = 10000.0
D_CONV = 512
D_FF = 5632
PROJ_W = 4880
EPS = 1e-6
PAST_LEN = 16384

C_Z, C_XBC, C_Q, C_G, C_K, C_V, C_DT, PROJ_P = 0, 1024, 2560, 3072, 4608, 4736, 4864, 5120

VMEM_LIMIT = 56 * 1024 * 1024


def _cparams(sem):
    return pltpu.CompilerParams(dimension_semantics=sem, vmem_limit_bytes=VMEM_LIMIT)


def _silu(x):
    return x * jax.nn.sigmoid(x)


def _rms_mod(x, w, scale, shift):
    ms = jnp.mean(x * x, axis=-1, keepdims=True)
    return (x * lax.rsqrt(ms + EPS) * w) * (1.0 + scale) + shift


def _split3(x):
    h1 = x.astype(BF16)
    r1 = x - h1.astype(F32)
    h2 = r1.astype(BF16)
    r2 = r1 - h2.astype(F32)
    return h1, h2, r2.astype(BF16)


def _dot(a, b):
    return jnp.dot(a, b, preferred_element_type=F32)


def _dot_nt(a, b):
    return lax.dot_general(a, b, (((1,), (1,)), ((), ())), preferred_element_type=F32)


def _layer_spec(arr, l):
    shape = arr.shape[1:]
    return pl.BlockSpec((None,) + shape, lambda *_: (l,) + (0,) * len(shape))


def _mod_kernel(c_ref, w_ref, b_ref, o_ref, a_ref):
    a = _silu(c_ref[...]).astype(BF16)
    a_ref[...] = a
    o_ref[...] = _dot(a, w_ref[...].astype(BF16)) + b_ref[...]


def _modulation_first(c_all, w_ada, b_ada3, tn=1024):
    _, d, n = w_ada.shape
    r = c_all.shape[0]
    return pl.pallas_call(
        _mod_kernel,
        grid=(n // tn,),
        in_specs=[
            pl.BlockSpec((r, d), lambda j: (0, 0)),
            pl.BlockSpec((None, d, tn), lambda j: (0, 0, j)),
            pl.BlockSpec((None, 1, tn), lambda j: (0, 0, j)),
        ],
        out_specs=[pl.BlockSpec((r, tn), lambda j: (0, j)), pl.BlockSpec((r, d), lambda j: (0, 0))],
        out_shape=[jax.ShapeDtypeStruct((r, n), F32), jax.ShapeDtypeStruct((r, d), BF16)],
        compiler_params=_cparams(("arbitrary",)),
        name="modulation",
    )(c_all, w_ada, b_ada3)


def _norm_kernel(x_ref, w_ref, sc_ref, sh_ref, o_ref):
    o_ref[...] = _rms_mod(x_ref[...], w_ref[...], sc_ref[...], sh_ref[...]).astype(BF16)


def _mod_spec(mod, j, tm, rpb):
    r = mod.shape[1]
    return pl.BlockSpec((None, r, D_MODEL), lambda i: ((i * tm) // rpb, 0, j))


def _norm_mod(x, w_all, mod, l, j_scale, j_shift, tm, rpb):
    m = x.shape[0]
    return pl.pallas_call(
        _norm_kernel,
        grid=(m // tm,),
        in_specs=[
            pl.BlockSpec((tm, D_MODEL), lambda i: (i, 0)),
            _layer_spec(w_all, l),
            _mod_spec(mod, j_scale, tm, rpb),
            _mod_spec(mod, j_shift, tm, rpb),
        ],
        out_specs=pl.BlockSpec((tm, D_MODEL), lambda i: (i, 0)),
        out_shape=jax.ShapeDtypeStruct((m, D_MODEL), BF16),
        compiler_params=_cparams(("arbitrary",)),
        name="norm_mod",
    )(x, w_all, mod, mod)


MIX_CHUNKS = 2
IN_TN = 1024
IN_SRC = 256
IN_SUB = 256
IN_ROW_UNIT = 16


def _in_proj_table():
    segs = [(C_Z, C_Q, 0), (C_Q, C_G, 2576), (C_G, C_K, 3344), (C_K, C_DT, 3088), (C_DT, PROJ_P, 2560)]
    rows = []
    for st in range(PROJ_P // IN_SRC):
        c = st * IN_SRC
        lo, _, src = [g for g in segs if g[0] <= c < g[1]][0]
        rows.append((src + c - lo) // IN_ROW_UNIT)
    return jnp.array(rows, jnp.int32)


def _in_proj_kernel(tab_ref, h_ref, hs_ref, a0_ref, a1_ref, a2_ref, a3_ref, wo_ref,
                    o_ref, os_ref, wo16_ref, w16_ref, *, tm):
    del tab_ref
    j = pl.program_id(0)
    i = pl.program_id(1)
    nsub = tm // IN_SUB
    wo16_ref[...] = wo_ref[...].astype(BF16)

    @pl.when(i == 0)
    def _():
        for s, a_ref in enumerate((a0_ref, a1_ref, a2_ref, a3_ref)):
            w16_ref[:, IN_SRC * s:IN_SRC * (s + 1)] = a_ref[0].T.astype(BF16)
        os_ref[...] = _dot(hs_ref[...], w16_ref[...])

    def rows(r):
        return slice(r * IN_SUB, (r + 1) * IN_SUB)

    def tile(r):
        return _dot(h_ref[rows(r), :], w16_ref[...])

    @pl.when(j == 0)
    def _():
        for r in range(nsub):
            o_ref[rows(r), :] = _silu(tile(r))

    @pl.when(j > 0)
    def _():
        for r in range(nsub):
            o_ref[rows(r), :] = tile(r)


def _in_proj(h, hs, w_in_t, l, tab, w_out_all, tm):
    m, k = h.shape
    ms = hs.shape[0]
    nsrc = IN_TN // IN_SRC
    ni = m // tm
    nsteps = (PROJ_P // IN_TN) * ni
    nslab = 1 << (nsteps.bit_length() - 1)
    slab = D_MODEL // nslab
    a_spec = lambda s: pl.BlockSpec((pl.Element(1), pl.Element(IN_SRC), pl.Element(k)),
                                    lambda j, i, t: (l, t[nsrc * j + s] * IN_ROW_UNIT, 0))
    slab_idx = lambda j, i: jnp.minimum(j * ni + i, nslab - 1)
    return pl.pallas_call(
        functools.partial(_in_proj_kernel, tm=tm),
        grid_spec=pltpu.PrefetchScalarGridSpec(
            num_scalar_prefetch=1,
            grid=(PROJ_P // IN_TN, ni),
            in_specs=[
                pl.BlockSpec((tm, k), lambda j, i, t: (i, 0)),
                pl.BlockSpec((ms, k), lambda j, i, t: (0, 0)),
                a_spec(0), a_spec(1), a_spec(2), a_spec(3),
                pl.BlockSpec((None, slab, D_MODEL), lambda j, i, t: (l, slab_idx(j, i), 0)),
            ],
            out_specs=[
                pl.BlockSpec((tm, IN_TN), lambda j, i, t: (i, j)),
                pl.BlockSpec((ms, IN_TN), lambda j, i, t: (0, j)),
                pl.BlockSpec((slab, D_MODEL), lambda j, i, t: (slab_idx(j, i), 0)),
            ],
            scratch_shapes=[pltpu.VMEM((k, IN_TN), BF16)],
        ),
        out_shape=[jax.ShapeDtypeStruct((m, PROJ_P), F32), jax.ShapeDtypeStruct((ms, PROJ_P), F32),
                   jax.ShapeDtypeStruct((D_MODEL, D_MODEL), BF16)],
        compiler_params=_cparams(("arbitrary", "arbitrary")),
        name="in_proj",
    )(tab, h, hs, w_in_t, w_in_t, w_in_t, w_in_t, w_out_all)


def _ssd_body(zx_ref, g_ref, kvdt_ref, cw_ref, cb_ref, dtb_ref, alog_ref, dsk_ref, nw_ref, scw_ref,
              y_ref, yc_ref, cst_ref, ust_ref, xpad_ref, upad_ref, st_ref):
    L = CHUNK
    zg = zx_ref[:, :D_SSD]
    raw = zx_ref[:, D_SSD:]

    xpad_ref[8:8 + L, :] = raw
    cw = cw_ref[...]
    conv = (cw[3:4] * raw + cw[2:3] * xpad_ref[7:7 + L, :] + cw[1:2] * xpad_ref[6:6 + L, :]
            + cw[0:1] * xpad_ref[5:5 + L, :])
    xpad_ref[0:8, :] = raw[L - 8:, :]
    cst_ref[...] = raw[L - 8:, :]
    xbc = _silu(conv + cb_ref[...])
    xs = xbc[:, :D_SSD]

    lane = lax.broadcasted_iota(jnp.int32, (L, 128), 1)
    row = lax.broadcasted_iota(jnp.int32, (L, 128), 0)
    lo = (lane & 64) == 0
    causal = row >= lane

    head_ok = lane[0:1] < SSM_HEADS
    a_row = jnp.where(head_ok, -jnp.exp(alog_ref[...]), 0.0)
    dtr = kvdt_ref[:, 256:384] + dtb_ref[...]
    dt = jnp.maximum(dtr, 0.0) + jnp.log1p(jnp.exp(-jnp.abs(dtr)))
    da = dt * a_row

    tril = jnp.where(causal, 1.0, 0.0).astype(BF16)
    d1, d2, d3 = _split3(da)
    cs = _dot(tril, d1) + _dot(tril, d2) + _dot(tril, d3)
    cs_t = cs.T
    ecs = jnp.exp(cs)
    te = jnp.exp(cs[L - 1:L, :] - cs)
    ecl = ecs[L - 1:L, :]

    def pair(arr, j):
        rows = arr.shape[0]
        return jnp.where(lo[:rows], arr[:, 2 * j:2 * j + 1], arr[:, 2 * j + 1:2 * j + 2])

    yield
    cbs, yoffs, updates = [], [], []
    xdts = []
    for j in range(8):
        xdts.append(xs[:, 128 * j:128 * (j + 1)] * pair(dt, j))
    for g in range(2):
        bm = xbc[:, D_SSD + 128 * g:D_SSD + 128 * (g + 1)]
        cm = xbc[:, D_SSD + 256 + 128 * g:D_SSD + 256 + 128 * (g + 1)]
        cm16 = cm.astype(BF16)
        cbs.append(jnp.where(causal, _dot_nt(cm16, bm.astype(BF16)), 0.0))
        yoffs.append(_dot(cm16, st_ref[:, 512 * g:512 * (g + 1)].astype(BF16)))
        xe = jnp.concatenate([xdts[4 * g + i] * pair(te, 4 * g + i) for i in range(4)], axis=1)
        s_new = _dot(bm.T.astype(BF16), xe.astype(BF16))
        cd = jnp.concatenate([pair(ecl, 4 * g + i) for i in range(4)], axis=1)
        updates.append((cd, s_new))
    for g in range(2):
        cd, s_new = updates[g]
        st_ref[:, 512 * g:512 * (g + 1)] = cd * st_ref[:, 512 * g:512 * (g + 1)] + s_new

    ys = []
    for j in range(8):
        g = j // 4
        ca, cb_ = cs[:, 2 * j:2 * j + 1], cs[:, 2 * j + 1:2 * j + 2]
        ra, rb = cs_t[2 * j:2 * j + 1, :], cs_t[2 * j + 1:2 * j + 2, :]
        dec_a = jnp.exp(jnp.minimum(ca - ra, 0.0))
        dec_b = jnp.exp(jnp.minimum(cb_ - rb, 0.0))
        mm = jnp.concatenate([cbs[g] * dec_a, cbs[g] * dec_b], axis=1).astype(BF16)
        xp = xdts[j]
        rhs = jnp.concatenate([jnp.where(lo, xp, 0.0), jnp.where(lo, 0.0, xp)], axis=0).astype(BF16)
        yd = _dot(mm, rhs)
        yo = yoffs[g][:, 128 * (j % 4):128 * (j % 4 + 1)] * pair(ecs, j)
        ys.append(yd + yo + dsk_ref[:, 128 * j:128 * (j + 1)] * xs[:, 128 * j:128 * (j + 1)])
    y = jnp.concatenate(ys, axis=1) * zg
    nw = nw_ref[...]
    outs = []
    for g in range(2):
        yg = y[:, 512 * g:512 * (g + 1)]
        ms = jnp.mean(yg * yg, axis=-1, keepdims=True)
        outs.append(yg * lax.rsqrt(ms + EPS) * nw[:, 512 * g:512 * (g + 1)])
    y_ref[...] = jnp.concatenate(outs, axis=1).astype(BF16)

    gg = g_ref[...]
    u = gg[:, D_CONV:2 * D_CONV] * gg[:, 2 * D_CONV:]
    upad_ref[8:8 + L, :] = u
    sw = scw_ref[...]
    sconv = sw[2:3] * u + sw[1:2] * upad_ref[7:7 + L, :] + sw[0:1] * upad_ref[6:6 + L, :]
    upad_ref[0:8, :] = u[L - 8:, :]
    ust_ref[...] = u[L - 8:, :]
    yc_ref[...] = (gg[:, :D_CONV] * sconv).astype(BF16)


def _head_norm(x, w, bd):
    sq = x * x
    hi = sq.astype(BF16)
    lo = (sq - hi.astype(F32)).astype(BF16)
    ms = _dot(hi, bd) + _dot(lo, bd)
    return x * lax.rsqrt(ms + EPS) * w


def _rope(x, cos, sin_signed, first_half):
    partner = jnp.where(first_half, pltpu.roll(x, 96, axis=1), pltpu.roll(x, 32, axis=1))
    return x * cos + partner * sin_signed


def _seg_mean_matrix():
    r = lax.broadcasted_iota(jnp.int32, (128, 128), 0)
    c = lax.broadcasted_iota(jnp.int32, (128, 128), 1)
    return jnp.where((r & 64) == (c & 64), 1.0 / HEAD_DIM, 0.0).astype(BF16)


def _attn_body(sink_ref, q_ref, kv_ref, cos_ref, sin_ref, qw_ref, kw_ref,
               o_ref, kprev_ref, vprev_ref, *, layer, has_prev):
    L = CHUNK
    bd = _seg_mean_matrix()
    lane = lax.broadcasted_iota(jnp.int32, (L, 128), 1)
    lo = (lane & 64) == 0
    first_half = (lane & 32) == 0
    cos = cos_ref[...]
    sin = sin_ref[...]

    k_new = _rope(_head_norm(kv_ref[:, 0:128], kw_ref[...], bd), cos, sin, first_half)
    v_new = kv_ref[:, 128:256]
    kband = jnp.concatenate([kprev_ref[...], k_new], axis=0)
    vband = jnp.concatenate([vprev_ref[...], v_new], axis=0)
    kprev_ref[...] = k_new
    vprev_ref[...] = v_new
    lo2 = jnp.concatenate([lo, lo], axis=0)
    kswap = pltpu.roll(kband, 64, axis=1)
    vswap = pltpu.roll(vband, 64, axis=1)
    k2 = [jnp.where(lo2, kband, kswap).astype(BF16), jnp.where(lo2, kswap, kband).astype(BF16)]
    v2 = [jnp.where(lo2, vband, vswap).astype(BF16), jnp.where(lo2, vswap, vband).astype(BF16)]

    qi = lax.broadcasted_iota(jnp.int32, (L, 2 * L), 0)
    kj = lax.broadcasted_iota(jnp.int32, (L, 2 * L), 1)
    diff = L + qi - kj
    valid = (diff >= 0) & (diff <= WINDOW) & ((kj >= L) | has_prev)

    qt = []
    for t in range(4):
        qn = _head_norm(q_ref[:, 128 * t:128 * (t + 1)], qw_ref[...], bd)
        qt.append(_rope(qn, cos, sin, first_half) * (HEAD_DIM ** -0.5))

    otiles = []
    for g in range(KV_HEADS):
        if g == 1:
            yield
        lhs = []
        for t in (2 * g, 2 * g + 1):
            lhs.append(jnp.where(lo, qt[t], 0.0))
            lhs.append(jnp.where(lo, 0.0, qt[t]))
        s = _dot_nt(jnp.concatenate(lhs, axis=0).astype(BF16), k2[g])
        ps = []
        for i in range(4):
            sink = sink_ref[layer, 4 * g + i]
            sh = jnp.where(valid, s[L * i:L * (i + 1)], -1e30)
            mx = jnp.maximum(jnp.max(sh, axis=-1, keepdims=True), sink)
            p = jnp.exp(sh - mx)
            den = jnp.sum(p, axis=-1, keepdims=True) + jnp.exp(sink - mx)
            ps.append(p * (1.0 / den))
        og = _dot(jnp.concatenate(ps, axis=0).astype(BF16), v2[g])
        otiles.append(jnp.where(lo, og[0:L], og[L:2 * L]))
        otiles.append(jnp.where(lo, og[2 * L:3 * L], og[3 * L:4 * L]))
    o_ref[...] = jnp.concatenate(otiles, axis=1).astype(BF16)


def _outproj_rows(ys_ref, o_ref, yc_ref, x_ref, gate_ref, sc_ref, sh_ref, w_ref, nw_ref, x1_ref, h2_ref):
    n = x_ref.shape[0]
    sub = min(n, 256)
    for r in range(n // sub):
        rs = slice(r * sub, (r + 1) * sub)
        gs = rs if gate_ref.shape[0] == n else slice(None)
        acc = _dot(ys_ref[rs, :], w_ref[0:D_SSD, :])
        acc += _dot(o_ref[rs, :], w_ref[D_SSD:D_SSD + D_ATTN, :])
        acc += _dot(yc_ref[rs, :], w_ref[D_SSD + D_ATTN:, :])
        x1 = x_ref[rs, :] + gate_ref[gs, :] * acc
        x1_ref[rs, :] = x1
        h2_ref[rs, :] = _rms_mod(x1, nw_ref[...], sc_ref[gs, :], sh_ref[gs, :]).astype(BF16)


def _row_specs(widths, tm):
    return [pl.BlockSpec((tm, w), lambda i: (i, 0)) for w in widths]


def _const_specs(widths, rows):
    return [pl.BlockSpec((rows, w), lambda i: (0, 0)) for w in widths]


def _mix_out_kernel(*refs, layer, with_mod, nc, nsteps):
    it = iter(refs)
    take = lambda n: [next(it) for _ in range(n)]
    (sink_ref, zx_ref, g_ref, kvdt_ref, q_ref, cos_ref, sin_ref,
     cw_ref, cb_ref, dtb_ref, alog_ref, dsk_ref, nw_ref, scw_ref, qw_ref, kw_ref) = take(16)
    x_ref, gate_ref, sc_ref, sh_ref = take(4)
    sample_in = take(7)
    w_ref, n2_ref = take(2)
    if with_mod:
        a_ref, wada_ref, bada_ref = take(3)
    st_out_ref, cst_ref, ust_ref, kwin_ref, vwin_ref, x1_ref, h2_ref, x1s_ref, h2s_ref = take(9)
    if with_mod:
        (modn_ref,) = take(1)
        modn_ref[...] = _dot(a_ref[...], wada_ref[...].astype(BF16)) + bada_ref[...]
    y_ref, yc_ref, o_ref, xpad_ref, upad_ref, st_ref, kprev_ref, vprev_ref = take(8)
    s = pl.program_id(0)
    sm = jnp.minimum(s, nsteps - 1)
    c = sm % nc
    cur = s % 2
    prev = 1 - cur

    @pl.when((c == 0) & (s < nsteps))
    def _():
        xpad_ref[0:8, :] = jnp.zeros((8, D_XBC), F32)
        upad_ref[0:8, :] = jnp.zeros((8, D_CONV), F32)
        st_ref[...] = jnp.zeros_like(st_ref)
        kprev_ref[...] = jnp.zeros_like(kprev_ref)
        vprev_ref[...] = jnp.zeros_like(vprev_ref)

    @pl.when(s == 0)
    def _():
        y_ref[...] = jnp.zeros_like(y_ref)
        yc_ref[...] = jnp.zeros_like(yc_ref)
        o_ref[...] = jnp.zeros_like(o_ref)
        _outproj_rows(*sample_in, w_ref, n2_ref, x1s_ref, h2s_ref)

    nparts = 4 * MIX_CHUNKS
    wcol = D_MODEL // nparts

    def proj_part(k):
        cols = slice(k * wcol, (k + 1) * wcol)
        acc = _dot(y_ref[prev], w_ref[0:D_SSD, cols])
        acc += _dot(o_ref[prev], w_ref[D_SSD:D_SSD + D_ATTN, cols])
        acc += _dot(yc_ref[prev], w_ref[D_SSD + D_ATTN:, cols])
        x1_ref[:, cols] = acc

    for r in range(MIX_CHUNKS):
        rows = lambda ref: ref.at[r * CHUNK:(r + 1) * CHUNK]
        ssd = _ssd_body(rows(zx_ref), rows(g_ref), rows(kvdt_ref), cw_ref, cb_ref, dtb_ref, alog_ref, dsk_ref,
                        nw_ref, scw_ref, rows(y_ref.at[cur]), rows(yc_ref.at[cur]), cst_ref, ust_ref,
                        xpad_ref, upad_ref, st_ref)
        attn = _attn_body(sink_ref, rows(q_ref), rows(kvdt_ref), rows(cos_ref), rows(sin_ref), qw_ref,
                          kw_ref, rows(o_ref.at[cur]), kprev_ref, vprev_ref, layer=layer,
                          has_prev=(c > 0) if r == 0 else True)
        for k, stage in enumerate((ssd, ssd, attn, attn)):
            proj_part(4 * r + k)
            next(stage, None)
    x1 = x_ref[...] + gate_ref[...] * x1_ref[...]
    x1_ref[...] = x1
    h2_ref[...] = _rms_mod(x1, n2_ref[...], sc_ref[...], sh_ref[...]).astype(BF16)
    kwin_ref[...] = kprev_ref[...]
    vwin_ref[...] = vprev_ref[...]

    @pl.when((c == nc - 1) & (s < nsteps))
    def _():
        st_out_ref[...] = st_ref[...].T


def _mix_out(proj, x_p, mod_p, mix_s, x_s, mod_s, w_out_l, n2_all, nb, seq, l, sinks, cos_t, sin_t,
             cw, cb, dtb, alog, dsk, nw, scw, qw, kw, mod_next=None):
    tr = MIX_CHUNKS * CHUNK
    nc = seq // tr
    m = nb * seq
    ms = x_s.shape[0]
    nsteps = nb * nc
    cur = lambda s: jnp.minimum(s, nsteps - 1)
    prv = lambda s: jnp.maximum(s - 1, 0)
    blk = lambda width, col: pl.BlockSpec((tr, width), lambda s: (cur(s), col))
    prv_blk = lambda width: pl.BlockSpec((tr, width), lambda s: (prv(s), 0))
    per_batch = lambda rows, width: pl.BlockSpec((None, rows, width), lambda s: (cur(s) // nc, 0, 0))
    const = lambda rows, width: pl.BlockSpec((rows, width), lambda s: (0, 0))
    mod_blk = lambda mod, j, bsel: pl.BlockSpec((None, mod.shape[1], D_MODEL), lambda s: (bsel(s), 0, j))
    mods = lambda mod, bsel: [mod_blk(mod, 2, bsel), mod_blk(mod, 4, bsel), mod_blk(mod, 3, bsel)]
    with_mod = mod_next is not None
    mod_in, mod_out, mod_shape, mod_args = [], [], [], []
    if with_mod:
        a16, w_ada, b_ada3 = mod_next
        r, d = a16.shape
        n = w_ada.shape[2]
        cw_mod = n // nsteps
        assert cw_mod * nsteps == n and cw_mod % 128 == 0
        mod_in = [const(r, d),
                  pl.BlockSpec((None, d, cw_mod), lambda s: (l + 1, 0, cur(s))),
                  pl.BlockSpec((None, 1, cw_mod), lambda s: (l + 1, 0, cur(s)))]
        mod_out = [pl.BlockSpec((r, cw_mod), lambda s: (0, cur(s)))]
        mod_shape = [jax.ShapeDtypeStruct((r, n), F32)]
        mod_args = [a16, w_ada, b_ada3]
    return pl.pallas_call(
        functools.partial(_mix_out_kernel, layer=l, with_mod=with_mod, nc=nc, nsteps=nsteps),
        grid=(nsteps + 1,),
        in_specs=([
            pl.BlockSpec(memory_space=pltpu.SMEM),
            blk(2560, 0), blk(1536, 2), blk(512, 9), blk(512, 5),
            pl.BlockSpec((tr, 128), lambda s: (cur(s) % nc, 0)),
            pl.BlockSpec((tr, 128), lambda s: (cur(s) % nc, 0)),
            _layer_spec(cw, l), _layer_spec(cb, l), _layer_spec(dtb, l), _layer_spec(alog, l),
            _layer_spec(dsk, l), _layer_spec(nw, l), _layer_spec(scw, l), _layer_spec(qw, l), _layer_spec(kw, l),
            prv_blk(D_MODEL)] + mods(mod_p, lambda s: prv(s) // nc)
            + [const(ms, D_SSD), const(ms, D_ATTN), const(ms, D_CONV), const(ms, D_MODEL)]
            + mods(mod_s, lambda s: 0)
            + [pl.BlockSpec((D_MODEL, D_MODEL), lambda s: (0, 0), pipeline_mode=pl.Buffered(1)),
               _layer_spec(n2_all, l)]
            + mod_in),
        out_specs=[
            per_batch(D_SSD, D_STATE), per_batch(8, D_XBC), per_batch(8, D_CONV),
            per_batch(CHUNK, 128), per_batch(CHUNK, 128),
            prv_blk(D_MODEL), prv_blk(D_MODEL), const(ms, D_MODEL), const(ms, D_MODEL),
        ] + mod_out,
        out_shape=[
            jax.ShapeDtypeStruct((nb, D_SSD, D_STATE), F32),
            jax.ShapeDtypeStruct((nb, 8, D_XBC), F32),
            jax.ShapeDtypeStruct((nb, 8, D_CONV), F32),
            jax.ShapeDtypeStruct((nb, CHUNK, 128), F32),
            jax.ShapeDtypeStruct((nb, CHUNK, 128), F32),
            jax.ShapeDtypeStruct((m, D_MODEL), F32), jax.ShapeDtypeStruct((m, D_MODEL), BF16),
            jax.ShapeDtypeStruct((ms, D_MODEL), F32), jax.ShapeDtypeStruct((ms, D_MODEL), BF16),
        ] + mod_shape,
        scratch_shapes=[
            pltpu.VMEM((2, tr, D_SSD), BF16), pltpu.VMEM((2, tr, D_CONV), BF16), pltpu.VMEM((2, tr, D_ATTN), BF16),
            pltpu.VMEM((8 + CHUNK, D_XBC), F32),
            pltpu.VMEM((8 + CHUNK, D_CONV), F32),
            pltpu.VMEM((D_STATE, D_SSD), F32),
            pltpu.VMEM((CHUNK, 128), F32),
            pltpu.VMEM((CHUNK, 128), F32),
        ],
        compiler_params=_cparams(("arbitrary",)),
        name="mix_out",
    )(sinks, proj, proj, proj, proj, cos_t, sin_t, cw, cb, dtb, alog, dsk, nw, scw, qw, kw,
      x_p, mod_p, mod_p, mod_p, *mix_s, x_s, mod_s, mod_s, mod_s, w_out_l, n2_all, *mod_args)


def _up_seq_kernel(h_ref, hs_ref, wg_ref, wu_ref, cw_ref, cb_ref, prev_ref, wd_ref,
                   a_ref, gst_ref, as_ref, gts_ref, wd16_ref, pad_ref, wg16_ref, wu16_ref, *, tm, rpb):
    i = pl.program_id(1)
    cw = cw_ref[...]
    cb = cb_ref[...]
    wd16_ref[...] = wd_ref[...].astype(BF16)

    @pl.when(i == 0)
    def _():
        wg16_ref[...] = wg_ref[...].astype(BF16)
        wu16_ref[...] = wu_ref[...].astype(BF16)
        hs = hs_ref[...]
        gt = _dot(hs, wg16_ref[...])
        u = _dot(hs, wu16_ref[...])
        conv = cw[2:3] * gt + cw[1:2] * prev_ref[1] + cw[0:1] * prev_ref[0]
        gts_ref[...] = gt
        as_ref[...] = (_silu(conv + cb) * u).astype(BF16)

    @pl.when((i * tm) % rpb == 0)
    def _():
        pad_ref[0:8, :] = jnp.zeros((8, pad_ref.shape[1]), F32)

    sub = 512
    for r in range(tm // sub):
        h = h_ref[r * sub:(r + 1) * sub, :]
        gt = _dot(h, wg16_ref[...])
        u = _dot(h, wu16_ref[...])
        base = 8 + r * sub
        pad_ref[base:base + sub, :] = gt
        conv = (cw[2:3] * gt + cw[1:2] * pad_ref[base - 1:base - 1 + sub, :]
                + cw[0:1] * pad_ref[base - 2:base - 2 + sub, :])
        a_ref[r * sub:(r + 1) * sub, :] = (_silu(conv + cb) * u).astype(BF16)
    tail = pad_ref[tm:tm + 8, :]
    pad_ref[0:8, :] = tail
    gst_ref[...] = tail


def _up_proj(h, hs, w_up_all, l, cw_all, cb_all, prev_all, w_down_all, nb, tm, tn):
    m = h.shape[0]
    ms = hs.shape[0]
    rpb = m // nb
    nt = D_FF // tn
    ni = m // tm
    slab = D_FF // (nt * ni)
    assert slab * nt * ni == D_FF and slab % 16 == 0
    return pl.pallas_call(
        functools.partial(_up_seq_kernel, tm=tm, rpb=rpb),
        grid=(nt, ni),
        in_specs=[
            pl.BlockSpec((tm, D_MODEL), lambda j, i: (i, 0)),
            pl.BlockSpec((ms, D_MODEL), lambda j, i: (0, 0)),
            pl.BlockSpec((None, D_MODEL, tn), lambda j, i: (l, 0, j)),
            pl.BlockSpec((None, D_MODEL, tn), lambda j, i: (l, 0, j + nt)),
            pl.BlockSpec((None, 3, tn), lambda j, i: (l, 0, j)),
            pl.BlockSpec((None, 1, tn), lambda j, i: (l, 0, j)),
            pl.BlockSpec((None, 2, ms, tn), lambda j, i: (l, 0, 0, j)),
            pl.BlockSpec((None, slab, D_MODEL), lambda j, i: (l, j * ni + i, 0)),
        ],
        out_specs=[
            pl.BlockSpec((tm, tn), lambda j, i: (i, j)),
            pl.BlockSpec((None, 8, tn), lambda j, i: ((i * tm) // rpb, 0, j)),
            pl.BlockSpec((ms, tn), lambda j, i: (0, j)),
            pl.BlockSpec((ms, tn), lambda j, i: (0, j)),
            pl.BlockSpec((slab, D_MODEL), lambda j, i: (j * ni + i, 0)),
        ],
        out_shape=[
            jax.ShapeDtypeStruct((m, D_FF), BF16),
            jax.ShapeDtypeStruct((nb, 8, D_FF), F32),
            jax.ShapeDtypeStruct((ms, D_FF), BF16),
            jax.ShapeDtypeStruct((ms, D_FF), F32),
            jax.ShapeDtypeStruct((D_FF, D_MODEL), BF16),
        ],
        scratch_shapes=[
            pltpu.VMEM((8 + tm, tn), F32),
            pltpu.VMEM((D_MODEL, tn), BF16),
            pltpu.VMEM((D_MODEL, tn), BF16),
        ],
        compiler_params=_cparams(("arbitrary", "arbitrary")),
        name="up_proj",
    )(h, hs, w_up_all, w_up_all, cw_all, cb_all, prev_all, w_down_all)


def _down_rows(a_ref, x_ref, gate_ref, mods, w_ref, nw_ref, x2_ref, hn_ref):
    x2 = x_ref[...] + gate_ref[...] * _dot(a_ref[...], w_ref[...])
    x2_ref[...] = x2
    if hn_ref is not None:
        hn_ref[...] = _rms_mod(x2, nw_ref[...], mods[0][...], mods[1][...]).astype(BF16)


def _down_kernel(*refs, with_norm):
    refs = list(refs)
    n_in = 5 if with_norm else 3
    p_in, s_in = refs[0:n_in], refs[n_in:2 * n_in]
    w_ref = refs[2 * n_in]
    nw_ref = refs[2 * n_in + 1] if with_norm else None
    outs = refs[2 * n_in + (2 if with_norm else 1):]
    p_out = (outs[0], outs[1]) if with_norm else (outs[0], None)
    s_out = (outs[2], outs[3]) if with_norm else (outs[1], None)

    @pl.when(pl.program_id(0) == 0)
    def _():
        _down_rows(s_in[0], s_in[1], s_in[2], s_in[3:], w_ref, nw_ref, *s_out)

    _down_rows(p_in[0], p_in[1], p_in[2], p_in[3:], w_ref, nw_ref, *p_out)


def _down(a_p, x_p, mod_p, a_s, x_s, mod_s, w_all, l, nw1_all, tm, rpb):
    m, ms = x_p.shape[0], x_s.shape[0]
    with_norm = l + 1 < nw1_all.shape[0]

    def mods(mod, t, r):
        sp = [_mod_spec(mod[0], 5, t, r)]
        if with_norm:
            sp += [_mod_spec(mod[1], 1, t, r), _mod_spec(mod[1], 0, t, r)]
        return sp

    def mod_args(mod):
        return [mod[0], mod[1], mod[1]] if with_norm else [mod[0]]

    in_specs = (_row_specs((D_FF, D_MODEL), tm) + mods(mod_p, tm, rpb)
                + _const_specs((D_FF, D_MODEL), ms) + mods(mod_s, 0, 1)
                + [pl.BlockSpec((D_FF, D_MODEL), lambda i: (0, 0), pipeline_mode=pl.Buffered(1))])
    args = [a_p, x_p] + mod_args(mod_p) + [a_s, x_s] + mod_args(mod_s) + [w_all]
    out_w = (D_MODEL, D_MODEL) if with_norm else (D_MODEL,)
    out_dt = (F32, BF16) if with_norm else (F32,)
    if with_norm:
        in_specs.append(_layer_spec(nw1_all, l + 1))
        args.append(nw1_all)
    res = pl.pallas_call(
        functools.partial(_down_kernel, with_norm=with_norm),
        grid=(m // tm,),
        in_specs=in_specs,
        out_specs=_row_specs(out_w, tm) + _const_specs(out_w, ms),
        out_shape=([jax.ShapeDtypeStruct((m, D_MODEL), d) for d in out_dt]
                   + [jax.ShapeDtypeStruct((ms, D_MODEL), d) for d in out_dt]),
        compiler_params=_cparams(("arbitrary",)),
        name="down_proj",
    )(*args)
    if with_norm:
        return res[0], res[1], res[2], res[3]
    return res[0], None, res[1], None


def _expand_matrix():
    r = lax.broadcasted_iota(jnp.int32, (128, D_SSD), 0)
    c = lax.broadcasted_iota(jnp.int32, (128, D_SSD), 1)
    return jnp.where(r == (c >> 6), 1.0, 0.0).astype(BF16)


def _ssd_step_kernel(p_ref, st_ref, cprev_ref, sprev_ref, cw_ref, cb_ref, dtb_ref, alog_ref, dsk_ref, nw_ref,
                     scw_ref, y_ref, yc_ref, u_ref, stn_ref):
    nbb = p_ref.shape[0]
    z = p_ref[:, C_Z:C_Z + D_SSD]
    raw = p_ref[:, C_XBC:C_XBC + D_XBC]
    cw = cw_ref[...]
    conv = cw[3:4] * raw + cw[2:3] * cprev_ref[2] + cw[1:2] * cprev_ref[1] + cw[0:1] * cprev_ref[0]
    xbc = _silu(conv + cb_ref[...])
    xs = xbc[:, :D_SSD]
    bm = xbc[:, D_SSD:D_SSD + 256]
    cm = xbc[:, D_SSD + 256:]

    lane = lax.broadcasted_iota(jnp.int32, (1, 128), 1)
    a_row = jnp.where(lane < SSM_HEADS, -jnp.exp(alog_ref[...]), 0.0)
    dtr = p_ref[:, C_DT:C_DT + 128] + dtb_ref[...]
    dt = jnp.maximum(dtr, 0.0) + jnp.log1p(jnp.exp(-jnp.abs(dtr)))
    dec = jnp.exp(dt * a_row)
    em = _expand_matrix()

    def expand(v):
        v1, v2, v3 = _split3(v)
        return _dot(v1, em) + _dot(v2, em) + _dot(v3, em)

    xdt_t = (xs * expand(dt)).T
    dec_t = expand(dec).T
    rowid = lax.broadcasted_iota(jnp.int32, (nbb, 512), 0)
    ysum = [jnp.zeros((nbb, 512), F32), jnp.zeros((nbb, 512), F32)]
    for i in range(nbb):
        for g in range(2):
            rs = slice(512 * g, 512 * (g + 1))
            hn = st_ref[i, rs, :] * dec_t[rs, i:i + 1] + xdt_t[rs, i:i + 1] * bm[i:i + 1, 128 * g:128 * (g + 1)]
            stn_ref[i, rs, :] = hn
            r = _dot_nt(cm[:, 128 * g:128 * (g + 1)].astype(BF16), hn.astype(BF16))
            ysum[g] = ysum[g] + jnp.where(rowid == i, r, 0.0)
    y = jnp.concatenate(ysum, axis=1) + dsk_ref[...] * xs
    y = y * _silu(z)
    nw = nw_ref[...]
    outs = []
    for g in range(2):
        yg = y[:, 512 * g:512 * (g + 1)]
        ms = jnp.mean(yg * yg, axis=-1, keepdims=True)
        outs.append(yg * lax.rsqrt(ms + EPS) * nw[:, 512 * g:512 * (g + 1)])
    y_ref[...] = jnp.concatenate(outs, axis=1).astype(BF16)

    u = p_ref[:, C_G + D_CONV:C_G + 2 * D_CONV] * p_ref[:, C_G + 2 * D_CONV:C_G + 3 * D_CONV]
    sw = scw_ref[...]
    sconv = sw[2:3] * u + sw[1:2] * sprev_ref[1] + sw[0:1] * sprev_ref[0]
    u_ref[...] = u
    yc_ref[...] = (p_ref[:, C_G:C_G + D_CONV] * sconv).astype(BF16)


def _ssd_step(proj, state_all, cprev_all, sprev_all, l, cw, cb, dtb, alog, dsk, nw, scw, nbb=8):
    nb = proj.shape[0]
    return pl.pallas_call(
        _ssd_step_kernel,
        grid=(nb // nbb,),
        in_specs=[
            pl.BlockSpec((nbb, PROJ_P), lambda i: (i, 0)),
            pl.BlockSpec((None, nbb, D_SSD, D_STATE), lambda i: (l, i, 0, 0)),
            pl.BlockSpec((None, 3, nbb, D_XBC), lambda i: (l, 0, i, 0)),
            pl.BlockSpec((None, 2, nbb, D_CONV), lambda i: (l, 0, i, 0)),
            _layer_spec(cw, l), _layer_spec(cb, l), _layer_spec(dtb, l), _layer_spec(alog, l),
            _layer_spec(dsk, l), _layer_spec(nw, l), _layer_spec(scw, l),
        ],
        out_specs=[
            pl.BlockSpec((nbb, D_SSD), lambda i: (i, 0)),
            pl.BlockSpec((nbb, D_CONV), lambda i: (i, 0)),
            pl.BlockSpec((nbb, D_CONV), lambda i: (i, 0)),
            pl.BlockSpec((nbb, D_SSD, D_STATE), lambda i: (i, 0, 0)),
        ],
        out_shape=[
            jax.ShapeDtypeStruct((nb, D_SSD), BF16),
            jax.ShapeDtypeStruct((nb, D_CONV), BF16),
            jax.ShapeDtypeStruct((nb, D_CONV), F32),
            jax.ShapeDtypeStruct((nb, D_SSD, D_STATE), F32),
        ],
        compiler_params=_cparams(("arbitrary",)),
        name="ssd_step",
    )(proj, state_all, cprev_all, sprev_all, cw, cb, dtb, alog, dsk, nw, scw)


def _qk_step_kernel(p_ref, cos_ref, sin_ref, qw_ref, kw_ref, q_ref, k_ref):
    bd = _seg_mean_matrix()
    lane = lax.broadcasted_iota(jnp.int32, (p_ref.shape[0], 128), 1)
    first_half = (lane & 32) == 0
    cos = cos_ref[...]
    sin = sin_ref[...]
    for t in range(4):
        qn = _head_norm(p_ref[:, C_Q + 128 * t:C_Q + 128 * (t + 1)], qw_ref[...], bd)
        q_ref[:, 128 * t:128 * (t + 1)] = _rope(qn, cos, sin, first_half) * (HEAD_DIM ** -0.5)
    k_ref[...] = _rope(_head_norm(p_ref[:, C_K:C_K + 128], kw_ref[...], bd), cos, sin, first_half)


def _qk_step(proj, cos_r, sin_r, l, qw, kw):
    nb = proj.shape[0]
    full = lambda a: pl.BlockSpec(a.shape, lambda i: (0,) * a.ndim)
    return pl.pallas_call(
        _qk_step_kernel,
        grid=(1,),
        in_specs=[full(proj), full(cos_r), full(sin_r), _layer_spec(qw, l), _layer_spec(kw, l)],
        out_specs=[pl.BlockSpec((nb, D_ATTN), lambda i: (0, 0)), pl.BlockSpec((nb, 128), lambda i: (0, 0))],
        out_shape=[jax.ShapeDtypeStruct((nb, D_ATTN), F32), jax.ShapeDtypeStruct((nb, 128), F32)],
        name="qk_step",
    )(proj, cos_r, sin_r, qw, kw)


def _attn_step_kernel(q_ref, kc_ref, vc_ref, knew_ref, vnew_ref, sink_ref, o_ref):
    q = q_ref[...]
    s = lax.dot_general(q.astype(BF16), kc_ref[...].astype(BF16), (((2,), (1,)), ((0,), (0,))),
                        preferred_element_type=F32)
    s_new = jnp.sum(q * knew_ref[...], axis=-1, keepdims=True)
    sink = sink_ref[...]
    mx = jnp.maximum(jnp.maximum(jnp.max(s, axis=-1, keepdims=True), s_new), sink)
    p = jnp.exp(s - mx)
    p_new = jnp.exp(s_new - mx)
    den = jnp.sum(p, axis=-1, keepdims=True) + p_new + jnp.exp(sink - mx)
    o = lax.dot_general((p / den).astype(BF16), vc_ref[...].astype(BF16), (((2,), (2,)), ((0,), (0,))),
                        preferred_element_type=F32)
    o_ref[...] = o + (p_new / den) * vnew_ref[...]


def _attn_step(q3, kc_all, vc_all, l, knew, vnew, sink_all):
    nb = q3.shape[0]
    full = lambda a: pl.BlockSpec(a.shape, lambda i: (0,) * a.ndim)
    return pl.pallas_call(
        _attn_step_kernel,
        grid=(1,),
        in_specs=[full(q3), _layer_spec(kc_all, l), _layer_spec(vc_all, l), full(knew), full(vnew),
                  _layer_spec(sink_all, l)],
        out_specs=pl.BlockSpec((nb, ATT_HEADS, 128), lambda i: (0, 0, 0)),
        out_shape=jax.ShapeDtypeStruct((nb, ATT_HEADS, 128), F32),
        name="attn_step",
    )(q3, kc_all, vc_all, knew, vnew, sink_all)


def _rope_tables(pos):
    half = HEAD_DIM // 2
    inv = np.power(ROPE_THETA, -np.arange(half, dtype=np.float64) / half)
    ang = np.asarray(pos, np.float64)[:, None] * inv[None, :]
    cos = np.cos(ang)
    sin = np.sin(ang)
    cos_t = np.concatenate([cos, cos, cos, cos], axis=-1).astype(np.float32)
    sin_t = np.concatenate([-sin, sin, -sin, sin], axis=-1).astype(np.float32)
    return jnp.asarray(cos_t), jnp.asarray(sin_t)


def kernel(x_prompt, x_sample, state_ssm, state_ssm_conv, cache_win_k, cache_win_v, state_short_conv,
           state_ffn_conv, c_prompt, c_sample, norm1_w, norm2_w, w_ada, b_ada, w_in, ssm_conv_w, ssm_conv_b,
           dt_bias, a_log, d_skip, ssm_norm_w, q_norm_w, k_norm_w, sinks, sconv_w, w_out, w_up, ffn_conv_w,
           ffn_conv_b, w_down):
    nbp, seq, _ = x_prompt.shape
    nbs = x_sample.shape[0]
    depth = w_in.shape[0]
    mp = nbp * seq
    nbuf = cache_win_k.shape[2]

    in_tab = _in_proj_table()
    w_in_t = jnp.swapaxes(w_in, 1, 2)

    n1 = norm1_w.reshape(depth, 1, D_MODEL)
    n2 = norm2_w.reshape(depth, 1, D_MODEL)
    cb = ssm_conv_b.reshape(depth, 1, D_XBC)
    dtb = jnp.pad(dt_bias, ((0, 0), (0, 128 - SSM_HEADS))).reshape(depth, 1, 128)
    alog = jnp.pad(a_log, ((0, 0), (0, 128 - SSM_HEADS))).reshape(depth, 1, 128)
    dsk = jnp.repeat(d_skip, SSM_HEAD_DIM, axis=1).reshape(depth, 1, D_SSD)
    nw = ssm_norm_w.reshape(depth, 1, D_SSD)
    qw = jnp.tile(q_norm_w, (1, 2)).reshape(depth, 1, 128)
    kw = jnp.tile(k_norm_w, (1, 2)).reshape(depth, 1, 128)
    fcb = ffn_conv_b.reshape(depth, 1, D_FF)
    sink3 = sinks.reshape(depth, ATT_HEADS, 1)

    st_all = state_ssm.reshape(depth, nbs, D_SSD, D_STATE)
    cprev = jnp.transpose(state_ssm_conv, (0, 2, 1, 3))
    sprev = jnp.transpose(state_short_conv, (0, 2, 1, 3))
    fprev = jnp.transpose(state_ffn_conv, (0, 2, 1, 3))
    ck = cache_win_k.reshape(depth, nbs, nbuf, 128)
    cv = cache_win_v.reshape(depth, nbs, nbuf, 128)
    ck_t = jnp.swapaxes(ck, 2, 3)
    cv_t = jnp.swapaxes(cv, 2, 3)

    c_all = jnp.concatenate([c_prompt, jnp.zeros((8 - nbp, D_MODEL), F32), c_sample], axis=0)
    b_ada3 = b_ada.reshape(depth, 1, 6 * D_MODEL)
    mod0, a16 = _modulation_first(c_all, w_ada, b_ada3)

    def split_mod(mod):
        return mod[:nbp].reshape(nbp, 1, 6 * D_MODEL), mod[8:].reshape(1, nbs, 6 * D_MODEL)

    mod_p, mod_s = [None] * (depth + 1), [None] * (depth + 1)
    mod_p[0], mod_s[0] = split_mod(mod0)

    cos_p, sin_p = _rope_tables(np.arange(seq))
    cos_s, sin_s = _rope_tables(np.full((1,), PAST_LEN))

    head_g = jnp.arange(ATT_HEADS) // (ATT_HEADS // KV_HEADS)
    q_place = (head_g[:, None] == jnp.arange(KV_HEADS)[None, :]).astype(F32)[None, :, :, None]

    xp = x_prompt.reshape(mp, D_MODEL)
    xs = x_sample.reshape(nbs, D_MODEL)
    TM = 512
    hp = _norm_mod(xp, n1, mod_p[0], 0, 1, 0, TM, seq)
    hs = _norm_mod(xs, n1, mod_s[0], 0, 1, 0, nbs, nbs)

    outs_p = [[] for _ in range(6)]
    outs_s = [[] for _ in range(6)]
    for l in range(depth):
        proj, projs, w_out_l = _in_proj(hp, hs, w_in_t, l, in_tab, w_out, min(1024, seq))

        y_s, yc_s, u_s, st_new = _ssd_step(projs, st_all, cprev, sprev, l, ssm_conv_w, cb, dtb, alog, dsk, nw,
                                           sconv_w)
        q_r, k_r = _qk_step(projs, cos_s, sin_s, l, qw, kw)
        v_r = projs[:, C_V:C_V + 128]
        q3 = (q_r.reshape(nbs, ATT_HEADS, 1, HEAD_DIM) * q_place).reshape(nbs, ATT_HEADS, 128)
        o3 = _attn_step(q3, ck_t, cv_t, l, k_r[:, None, :], v_r[:, None, :], sink3)
        o4 = o3.reshape(nbs, ATT_HEADS, KV_HEADS, HEAD_DIM)
        o_s = jnp.concatenate([o4[:, :4, 0], o4[:, 4:, 1]], axis=1).reshape(nbs, D_ATTN).astype(BF16)

        mixed = _mix_out(proj, xp, mod_p[l], (y_s, o_s, yc_s), xs, mod_s[l], w_out_l, n2, nbp, seq, l,
                         sinks, cos_p, sin_p, ssm_conv_w, cb, dtb, alog, dsk, nw, sconv_w, qw, kw,
                         mod_next=(a16, w_ada, b_ada3) if l + 1 < depth else None)
        st_t, cst, ust, kwin, vwin, xp, h2, xs, h2s = mixed[:9]
        if l + 1 < depth:
            mod_p[l + 1], mod_s[l + 1] = split_mod(mixed[9])
        act, gst, act_s, gt_s, w_down_l = _up_proj(h2, h2s, w_up, l, ffn_conv_w, fcb, fprev, w_down, nbp,
                                                   min(2048, seq), 512)
        xp, hp, xs, hs = _down(act, xp, mod_p[l:l + 2], act_s, xs, mod_s[l:l + 2], w_down_l, l, n1, 256, seq)

        outs_p[0].append(st_t.reshape(nbp, SSM_HEADS, SSM_HEAD_DIM, D_STATE))
        outs_p[1].append(cst[:, 8 - (SSM_CONV - 1):])
        outs_p[2].append(kwin.reshape(nbp, WINDOW, KV_HEADS, HEAD_DIM))
        outs_p[3].append(vwin.reshape(nbp, WINDOW, KV_HEADS, HEAD_DIM))
        outs_p[4].append(ust[:, 6:])
        outs_p[5].append(gst[:, 6:])

        outs_s[0].append(st_new.reshape(nbs, SSM_HEADS, SSM_HEAD_DIM, D_STATE))
        outs_s[1].append(projs[:, C_XBC:C_XBC + D_XBC])
        outs_s[2].append(k_r)
        outs_s[3].append(v_r)
        outs_s[4].append(u_s)
        outs_s[5].append(gt_s)

    res_p = [jnp.stack(a, axis=0) for a in outs_p]
    new_rows = [jnp.stack(a, axis=0) for a in outs_s]
    s_ssm = new_rows[0]
    s_ssm_conv = jnp.concatenate([state_ssm_conv[:, :, 1:], new_rows[1][:, :, None]], axis=2)
    s_win_k = jnp.concatenate([ck[:, :, 1:], new_rows[2][:, :, None]], axis=2).reshape(cache_win_k.shape)
    s_win_v = jnp.concatenate([cv[:, :, 1:], new_rows[3][:, :, None]], axis=2).reshape(cache_win_v.shape)
    s_short = jnp.concatenate([state_short_conv[:, :, 1:], new_rows[4][:, :, None]], axis=2)
    s_ffn = jnp.concatenate([state_ffn_conv[:, :, 1:], new_rows[5][:, :, None]], axis=2)
    return (xp.reshape(nbp, seq, D_MODEL), xs.reshape(nbs, 1, D_MODEL), *res_p,
            s_ssm, s_ssm_conv, s_win_k, s_win_v, s_short, s_ffn)
```

```python
import functools

import jax
import jax.numpy as jnp
import numpy as np
from jax import lax
from jax.experimental import pallas as pl
from jax.experimental.pallas import tpu as pltpu

F32 = jnp.float32
BF16 = jnp.bfloat16

D_MODEL = 2048
D_SSD = 1024
SSM_HEADS = 16
SSM_HEAD_DIM = 64
D_STATE = 128
SSM_CONV = 4
CHUNK = 128
D_XBC = 1536
HEAD_DIM = 64
D_ATTN = 512
ATT_HEADS = 8
KV_HEADS = 2
WINDOW = 128
ROPE_THETA = 10000.0
D_CONV = 512
D_FF = 5632
PROJ_W = 4880
EPS = 1e-6
PAST_LEN = 16384

C_Z, C_XBC, C_Q, C_G, C_K, C_V, C_DT, PROJ_P = 0, 1024, 2560, 3072, 4608, 4736, 4864, 5120

VMEM_LIMIT = 56 * 1024 * 1024


def _cparams(sem):
    return pltpu.CompilerParams(dimension_semantics=sem, vmem_limit_bytes=VMEM_LIMIT)


def _silu(x):
    return x * jax.nn.sigmoid(x)


def _rms_mod(x, w, scale, shift):
    ms = jnp.mean(x * x, axis=-1, keepdims=True)
    return (x * lax.rsqrt(ms + EPS) * w) * (1.0 + scale) + shift


def _split3(x):
    h1 = x.astype(BF16)
    r1 = x - h1.astype(F32)
    h2 = r1.astype(BF16)
    r2 = r1 - h2.astype(F32)
    return h1, h2, r2.astype(BF16)


def _dot(a, b):
    return jnp.dot(a, b, preferred_element_type=F32)


def _dot_nt(a, b):
    return lax.dot_general(a, b, (((1,), (1,)), ((), ())), preferred_element_type=F32)


def _layer_spec(arr, l):
    shape = arr.shape[1:]
    return pl.BlockSpec((None,) + shape, lambda *_: (l,) + (0,) * len(shape))


def _mod_kernel(c_ref, w_ref, b_ref, o_ref, a_ref):
    a = _silu(c_ref[...]).astype(BF16)
    a_ref[...] = a
    o_ref[...] = _dot(a, w_ref[...].astype(BF16)) + b_ref[...]


def _modulation_first(c_all, w_ada, b_ada3, tn=1024):
    _, d, n = w_ada.shape
    r = c_all.shape[0]
    return pl.pallas_call(
        _mod_kernel,
        grid=(n // tn,),
        in_specs=[
            pl.BlockSpec((r, d), lambda j: (0, 0)),
            pl.BlockSpec((None, d, tn), lambda j: (0, 0, j)),
            pl.BlockSpec((None, 1, tn), lambda j: (0, 0, j)),
        ],
        out_specs=[pl.BlockSpec((r, tn), lambda j: (0, j)), pl.BlockSpec((r, d), lambda j: (0, 0))],
        out_shape=[jax.ShapeDtypeStruct((r, n), F32), jax.ShapeDtypeStruct((r, d), BF16)],
        compiler_params=_cparams(("arbitrary",)),
        name="modulation",
    )(c_all, w_ada, b_ada3)


def _norm_kernel(x_ref, w_ref, sc_ref, sh_ref, o_ref):
    o_ref[...] = _rms_mod(x_ref[...], w_ref[...], sc_ref[...], sh_ref[...]).astype(BF16)


def _mod_spec(mod, j, tm, rpb):
    r = mod.shape[1]
    return pl.BlockSpec((None, r, D_MODEL), lambda i: ((i * tm) // rpb, 0, j))


def _norm_mod(x, w_all, mod, l, j_scale, j_shift, tm, rpb):
    m = x.shape[0]
    return pl.pallas_call(
        _norm_kernel,
        grid=(m // tm,),
        in_specs=[
            pl.BlockSpec((tm, D_MODEL), lambda i: (i, 0)),
            _layer_spec(w_all, l),
            _mod_spec(mod, j_scale, tm, rpb),
            _mod_spec(mod, j_shift, tm, rpb),
        ],
        out_specs=pl.BlockSpec((tm, D_MODEL), lambda i: (i, 0)),
        out_shape=jax.ShapeDtypeStruct((m, D_MODEL), BF16),
        compiler_params=_cparams(("arbitrary",)),
        name="norm_mod",
    )(x, w_all, mod, mod)


MIX_CHUNKS = 2
IN_TN = 1024
IN_SRC = 256
IN_SUB = 256
IN_ROW_UNIT = 16


def _in_proj_table():
    segs = [(C_Z, C_Q, 0), (C_Q, C_G, 2576), (C_G, C_K, 3344), (C_K, C_DT, 3088), (C_DT, PROJ_P, 2560)]
    rows = []
    for st in range(PROJ_P // IN_SRC):
        c = st * IN_SRC
        lo, _, src = [g for g in segs if g[0] <= c < g[1]][0]
        rows.append((src + c - lo) // IN_ROW_UNIT)
    return jnp.array(rows, jnp.int32)


def _in_proj_kernel(tab_ref, h_ref, hs_ref, a0_ref, a1_ref, a2_ref, a3_ref, wo_ref,
                    o_ref, os_ref, wo16_ref, w16_ref, *, tm):
    del tab_ref
    j = pl.program_id(0)
    i = pl.program_id(1)
    nsub = tm // IN_SUB
    wo16_ref[...] = wo_ref[...].astype(BF16)

    @pl.when(i == 0)
    def _():
        for s, a_ref in enumerate((a0_ref, a1_ref, a2_ref, a3_ref)):
            w16_ref[:, IN_SRC * s:IN_SRC * (s + 1)] = a_ref[0].T.astype(BF16)
        os_ref[...] = _dot(hs_ref[...], w16_ref[...])

    def rows(r):
        return slice(r * IN_SUB, (r + 1) * IN_SUB)

    def tile(r):
        return _dot(h_ref[rows(r), :], w16_ref[...])

    @pl.when(j == 0)
    def _():
        for r in range(nsub):
            o_ref[rows(r), :] = _silu(tile(r))

    @pl.when(j > 0)
    def _():
        for r in range(nsub):
            o_ref[rows(r), :] = tile(r)


def _in_proj(h, hs, w_in_t, l, tab, w_out_all, tm):
    m, k = h.shape
    ms = hs.shape[0]
    nsrc = IN_TN // IN_SRC
    ni = m // tm
    nsteps = (PROJ_P // IN_TN) * ni
    nslab = 1 << (nsteps.bit_length() - 1)
    slab = D_MODEL // nslab
    a_spec = lambda s: pl.BlockSpec((pl.Element(1), pl.Element(IN_SRC), pl.Element(k)),
                                    lambda j, i, t: (l, t[nsrc * j + s] * IN_ROW_UNIT, 0))
    slab_idx = lambda j, i: jnp.minimum(j * ni + i, nslab - 1)
    return pl.pallas_call(
        functools.partial(_in_proj_kernel, tm=tm),
        grid_spec=pltpu.PrefetchScalarGridSpec(
            num_scalar_prefetch=1,
            grid=(PROJ_P // IN_TN, ni),
            in_specs=[
                pl.BlockSpec((tm, k), lambda j, i, t: (i, 0)),
                pl.BlockSpec((ms, k), lambda j, i, t: (0, 0)),
                a_spec(0), a_spec(1), a_spec(2), a_spec(3),
                pl.BlockSpec((None, slab, D_MODEL), lambda j, i, t: (l, slab_idx(j, i), 0)),
            ],
            out_specs=[
                pl.BlockSpec((tm, IN_TN), lambda j, i, t: (i, j)),
                pl.BlockSpec((ms, IN_TN), lambda j, i, t: (0, j)),
                pl.BlockSpec((slab, D_MODEL), lambda j, i, t: (slab_idx(j, i), 0)),
            ],
            scratch_shapes=[pltpu.VMEM((k, IN_TN), BF16)],
        ),
        out_shape=[jax.ShapeDtypeStruct((m, PROJ_P), F32), jax.ShapeDtypeStruct((ms, PROJ_P), F32),
                   jax.ShapeDtypeStruct((D_MODEL, D_MODEL), BF16)],
        compiler_params=_cparams(("arbitrary", "arbitrary")),
        name="in_proj",
    )(tab, h, hs, w_in_t, w_in_t, w_in_t, w_in_t, w_out_all)


def _ssd_body(zx_ref, g_ref, kvdt_ref, cw_ref, cb_ref, dtb_ref, alog_ref, dsk_ref, nw_ref, scw_ref,
              y_ref, yc_ref, cst_ref, ust_ref, xpad_ref, upad_ref, st_ref):
    L = CHUNK
    zg = zx_ref[:, :D_SSD]
    raw = zx_ref[:, D_SSD:]

    xpad_ref[8:8 + L, :] = raw
    cw = cw_ref[...]
    conv = (cw[3:4] * raw + cw[2:3] * xpad_ref[7:7 + L, :] + cw[1:2] * xpad_ref[6:6 + L, :]
            + cw[0:1] * xpad_ref[5:5 + L, :])
    xpad_ref[0:8, :] = raw[L - 8:, :]
    cst_ref[...] = raw[L - 8:, :]
    xbc = _silu(conv + cb_ref[...])
    xs = xbc[:, :D_SSD]

    lane = lax.broadcasted_iota(jnp.int32, (L, 128), 1)
    row = lax.broadcasted_iota(jnp.int32, (L, 128), 0)
    lo = (lane & 64) == 0
    causal = row >= lane

    head_ok = lane[0:1] < SSM_HEADS
    a_row = jnp.where(head_ok, -jnp.exp(alog_ref[...]), 0.0)
    dtr = kvdt_ref[:, 256:384] + dtb_ref[...]
    dt = jnp.maximum(dtr, 0.0) + jnp.log1p(jnp.exp(-jnp.abs(dtr)))
    da = dt * a_row

    tril = jnp.where(causal, 1.0, 0.0).astype(BF16)
    d1, d2, d3 = _split3(da)
    cs = _dot(tril, d1) + _dot(tril, d2) + _dot(tril, d3)
    cs_t = cs.T
    ecs = jnp.exp(cs)
    te = jnp.exp(cs[L - 1:L, :] - cs)
    ecl = ecs[L - 1:L, :]

    def pair(arr, j):
        rows = arr.shape[0]
        return jnp.where(lo[:rows], arr[:, 2 * j:2 * j + 1], arr[:, 2 * j + 1:2 * j + 2])

    yield
    cbs, yoffs, updates = [], [], []
    xdts = []
    for j in range(8):
        xdts.append(xs[:, 128 * j:128 * (j + 1)] * pair(dt, j))
    for g in range(2):
        bm = xbc[:, D_SSD + 128 * g:D_SSD + 128 * (g + 1)]
        cm = xbc[:, D_SSD + 256 + 128 * g:D_SSD + 256 + 128 * (g + 1)]
        cm16 = cm.astype(BF16)
        cbs.append(jnp.where(causal, _dot_nt(cm16, bm.astype(BF16)), 0.0))
        yoffs.append(_dot(cm16, st_ref[:, 512 * g:512 * (g + 1)].astype(BF16)))
        xe = jnp.concatenate([xdts[4 * g + i] * pair(te, 4 * g + i) for i in range(4)], axis=1)
        s_new = _dot(bm.T.astype(BF16), xe.astype(BF16))
        cd = jnp.concatenate([pair(ecl, 4 * g + i) for i in range(4)], axis=1)
        updates.append((cd, s_new))
    for g in range(2):
        cd, s_new = updates[g]
        st_ref[:, 512 * g:512 * (g + 1)] = cd * st_ref[:, 512 * g:512 * (g + 1)] + s_new

    ys = []
    for j in range(8):
        g = j // 4
        ca, cb_ = cs[:, 2 * j:2 * j + 1], cs[:, 2 * j + 1:2 * j + 2]
        ra, rb = cs_t[2 * j:2 * j + 1, :], cs_t[2 * j + 1:2 * j + 2, :]
        dec_a = jnp.exp(jnp.minimum(ca - ra, 0.0))
        dec_b = jnp.exp(jnp.minimum(cb_ - rb, 0.0))
        mm = jnp.concatenate([cbs[g] * dec_a, cbs[g] * dec_b], axis=1).astype(BF16)
        xp = xdts[j]
        rhs = jnp.concatenate([jnp.where(lo, xp, 0.0), jnp.where(lo, 0.0, xp)], axis=0).astype(BF16)
        yd = _dot(mm, rhs)
        yo = yoffs[g][:, 128 * (j % 4):128 * (j % 4 + 1)] * pair(ecs, j)
        ys.append(yd + yo + dsk_ref[:, 128 * j:128 * (j + 1)] * xs[:, 128 * j:128 * (j + 1)])
    y = jnp.concatenate(ys, axis=1) * zg
    nw = nw_ref[...]
    outs = []
    for g in range(2):
        yg = y[:, 512 * g:512 * (g + 1)]
        ms = jnp.mean(yg * yg, axis=-1, keepdims=True)
        outs.append(yg * lax.rsqrt(ms + EPS) * nw[:, 512 * g:512 * (g + 1)])
    y_ref[...] = jnp.concatenate(outs, axis=1).astype(BF16)

    gg = g_ref[...]
    u = gg[:, D_CONV:2 * D_CONV] * gg[:, 2 * D_CONV:]
    upad_ref[8:8 + L, :] = u
    sw = scw_ref[...]
    sconv = sw[2:3] * u + sw[1:2] * upad_ref[7:7 + L, :] + sw[0:1] * upad_ref[6:6 + L, :]
    upad_ref[0:8, :] = u[L - 8:, :]
    ust_ref[...] = u[L - 8:, :]
    yc_ref[...] = (gg[:, :D_CONV] * sconv).astype(BF16)


def _head_norm(x, w, bd):
    sq = x * x
    hi = sq.astype(BF16)
    lo = (sq - hi.astype(F32)).astype(BF16)
    ms = _dot(hi, bd) + _dot(lo, bd)
    return x * lax.rsqrt(ms + EPS) * w


def _rope(x, cos, sin_signed, first_half):
    partner = jnp.where(first_half, pltpu.roll(x, 96, axis=1), pltpu.roll(x, 32, axis=1))
    return x * cos + partner * sin_signed


def _seg_mean_matrix():
    r = lax.broadcasted_iota(jnp.int32, (128, 128), 0)
    c = lax.broadcasted_iota(jnp.int32, (128, 128), 1)
    return jnp.where((r & 64) == (c & 64), 1.0 / HEAD_DIM, 0.0).astype(BF16)


def _attn_body(sink_ref, q_ref, kv_ref, cos_ref, sin_ref, qw_ref, kw_ref,
               o_ref, kprev_ref, vprev_ref, *, layer, has_prev):
    L = CHUNK
    bd = _seg_mean_matrix()
    lane = lax.broadcasted_iota(jnp.int32, (L, 128), 1)
    lo = (lane & 64) == 0
    first_half = (lane & 32) == 0
    cos = cos_ref[...]
    sin = sin_ref[...]

    k_new = _rope(_head_norm(kv_ref[:, 0:128], kw_ref[...], bd), cos, sin, first_half)
    v_new = kv_ref[:, 128:256]
    kband = jnp.concatenate([kprev_ref[...], k_new], axis=0)
    vband = jnp.concatenate([vprev_ref[...], v_new], axis=0)
    kprev_ref[...] = k_new
    vprev_ref[...] = v_new
    lo2 = jnp.concatenate([lo, lo], axis=0)
    kswap = pltpu.roll(kband, 64, axis=1)
    vswap = pltpu.roll(vband, 64, axis=1)
    k2 = [jnp.where(lo2, kband, kswap).astype(BF16), jnp.where(lo2, kswap, kband).astype(BF16)]
    v2 = [jnp.where(lo2, vband, vswap).astype(BF16), jnp.where(lo2, vswap, vband).astype(BF16)]

    qi = lax.broadcasted_iota(jnp.int32, (L, 2 * L), 0)
    kj = lax.broadcasted_iota(jnp.int32, (L, 2 * L), 1)
    diff = L + qi - kj
    valid = (diff >= 0) & (diff <= WINDOW) & ((kj >= L) | has_prev)

    qt = []
    for t in range(4):
        qn = _head_norm(q_ref[:, 128 * t:128 * (t + 1)], qw_ref[...], bd)
        qt.append(_rope(qn, cos, sin, first_half) * (HEAD_DIM ** -0.5))

    otiles = []
    for g in range(KV_HEADS):
        if g == 1:
            yield
        lhs = []
        for t in (2 * g, 2 * g + 1):
            lhs.append(jnp.where(lo, qt[t], 0.0))
            lhs.append(jnp.where(lo, 0.0, qt[t]))
        s = _dot_nt(jnp.concatenate(lhs, axis=0).astype(BF16), k2[g])
        ps = []
        for i in range(4):
            sink = sink_ref[layer, 4 * g + i]
            sh = jnp.where(valid, s[L * i:L * (i + 1)], -1e30)
            mx = jnp.maximum(jnp.max(sh, axis=-1, keepdims=True), sink)
            p = jnp.exp(sh - mx)
            den = jnp.sum(p, axis=-1, keepdims=True) + jnp.exp(sink - mx)
            ps.append(p * (1.0 / den))
        og = _dot(jnp.concatenate(ps, axis=0).astype(BF16), v2[g])
        otiles.append(jnp.where(lo, og[0:L], og[L:2 * L]))
        otiles.append(jnp.where(lo, og[2 * L:3 * L], og[3 * L:4 * L]))
    o_ref[...] = jnp.concatenate(otiles, axis=1).astype(BF16)


def _outproj_rows(ys_ref, o_ref, yc_ref, x_ref, gate_ref, sc_ref, sh_ref, w_ref, nw_ref, x1_ref, h2_ref):
    n = x_ref.shape[0]
    sub = min(n, 256)
    for r in range(n // sub):
        rs = slice(r * sub, (r + 1) * sub)
        gs = rs if gate_ref.shape[0] == n else slice(None)
        acc = _dot(ys_ref[rs, :], w_ref[0:D_SSD, :])
        acc += _dot(o_ref[rs, :], w_ref[D_SSD:D_SSD + D_ATTN, :])
        acc += _dot(yc_ref[rs, :], w_ref[D_SSD + D_ATTN:, :])
        x1 = x_ref[rs, :] + gate_ref[gs, :] * acc
        x1_ref[rs, :] = x1
        h2_ref[rs, :] = _rms_mod(x1, nw_ref[...], sc_ref[gs, :], sh_ref[gs, :]).astype(BF16)


def _row_specs(widths, tm):
    return [pl.BlockSpec((tm, w), lambda i: (i, 0)) for w in widths]


def _const_specs(widths, rows):
    return [pl.BlockSpec((rows, w), lambda i: (0, 0)) for w in widths]


def _mix_out_kernel(*refs, layer, with_mod, nc, nsteps):
    it = iter(refs)
    take = lambda n: [next(it) for _ in range(n)]
    (sink_ref, zx_ref, g_ref, kvdt_ref, q_ref, cos_ref, sin_ref,
     cw_ref, cb_ref, dtb_ref, alog_ref, dsk_ref, nw_ref, scw_ref, qw_ref, kw_ref) = take(16)
    x_ref, gate_ref, sc_ref, sh_ref = take(4)
    sample_in = take(7)
    w_ref, n2_ref = take(2)
    if with_mod:
        a_ref, wada_ref, bada_ref = take(3)
    st_out_ref, cst_ref, ust_ref, kwin_ref, vwin_ref, x1_ref, h2_ref, x1s_ref, h2s_ref = take(9)
    if with_mod:
        (modn_ref,) = take(1)
        modn_ref[...] = _dot(a_ref[...], wada_ref[...].astype(BF16)) + bada_ref[...]
    y_ref, yc_ref, o_ref, xpad_ref, upad_ref, st_ref, kprev_ref, vprev_ref = take(8)
    s = pl.program_id(0)
    sm = jnp.minimum(s, nsteps - 1)
    c = sm % nc
    cur = s % 2
    prev = 1 - cur

    @pl.when((c == 0) & (s < nsteps))
    def _():
        xpad_ref[0:8, :] = jnp.zeros((8, D_XBC), F32)
        upad_ref[0:8, :] = jnp.zeros((8, D_CONV), F32)
        st_ref[...] = jnp.zeros_like(st_ref)
        kprev_ref[...] = jnp.zeros_like(kprev_ref)
        vprev_ref[...] = jnp.zeros_like(vprev_ref)

    @pl.when(s == 0)
    def _():
        y_ref[...] = jnp.zeros_like(y_ref)
        yc_ref[...] = jnp.zeros_like(yc_ref)
        o_ref[...] = jnp.zeros_like(o_ref)
        _outproj_rows(*sample_in, w_ref, n2_ref, x1s_ref, h2s_ref)

    nparts = 4 * MIX_CHUNKS
    wcol = D_MODEL // nparts

    def proj_part(k):
        cols = slice(k * wcol, (k + 1) * wcol)
        acc = _dot(y_ref[prev], w_ref[0:D_SSD, cols])
        acc += _dot(o_ref[prev], w_ref[D_SSD:D_SSD + D_ATTN, cols])
        acc += _dot(yc_ref[prev], w_ref[D_SSD + D_ATTN:, cols])
        x1_ref[:, cols] = acc

    for r in range(MIX_CHUNKS):
        rows = lambda ref: ref.at[r * CHUNK:(r + 1) * CHUNK]
        ssd = _ssd_body(rows(zx_ref), rows(g_ref), rows(kvdt_ref), cw_ref, cb_ref, dtb_ref, alog_ref, dsk_ref,
                        nw_ref, scw_ref, rows(y_ref.at[cur]), rows(yc_ref.at[cur]), cst_ref, ust_ref,
                        xpad_ref, upad_ref, st_ref)
        attn = _attn_body(sink_ref, rows(q_ref), rows(kvdt_ref), rows(cos_ref), rows(sin_ref), qw_ref,
                          kw_ref, rows(o_ref.at[cur]), kprev_ref, vprev_ref, layer=layer,
                          has_prev=(c > 0) if r == 0 else True)
        for k, stage in enumerate((ssd, ssd, attn, attn)):
            proj_part(4 * r + k)
            next(stage, None)
    x1 = x_ref[...] + gate_ref[...] * x1_ref[...]
    x1_ref[...] = x1
    h2_ref[...] = _rms_mod(x1, n2_ref[...], sc_ref[...], sh_ref[...]).astype(BF16)
    kwin_ref[...] = kprev_ref[...]
    vwin_ref[...] = vprev_ref[...]

    @pl.when((c == nc - 1) & (s < nsteps))
    def _():
        st_out_ref[...] = st_ref[...].T


def _mix_out(proj, x_p, mod_p, mix_s, x_s, mod_s, w_out_l, n2_all, nb, seq, l, sinks, cos_t, sin_t,
             cw, cb, dtb, alog, dsk, nw, scw, qw, kw, mod_next=None):
    tr = MIX_CHUNKS * CHUNK
    nc = seq // tr
    m = nb * seq
    ms = x_s.shape[0]
    nsteps = nb * nc
    cur = lambda s: jnp.minimum(s, nsteps - 1)
    prv = lambda s: jnp.maximum(s - 1, 0)
    blk = lambda width, col: pl.BlockSpec((tr, width), lambda s: (cur(s), col))
    prv_blk = lambda width: pl.BlockSpec((tr, width), lambda s: (prv(s), 0))
    per_batch = lambda rows, width: pl.BlockSpec((None, rows, width), lambda s: (cur(s) // nc, 0, 0))
    const = lambda rows, width: pl.BlockSpec((rows, width), lambda s: (0, 0))
    mod_blk = lambda mod, j, bsel: pl.BlockSpec((None, mod.shape[1], D_MODEL), lambda s: (bsel(s), 0, j))
    mods = lambda mod, bsel: [mod_blk(mod, 2, bsel), mod_blk(mod, 4, bsel), mod_blk(mod, 3, bsel)]
    with_mod = mod_next is not None
    mod_in, mod_out, mod_shape, mod_args = [], [], [], []
    if with_mod:
        a16, w_ada, b_ada3 = mod_next
        r, d = a16.shape
        n = w_ada.shape[2]
        cw_mod = n // nsteps
        assert cw_mod * nsteps == n and cw_mod % 128 == 0
        mod_in = [const(r, d),
                  pl.BlockSpec((None, d, cw_mod), lambda s: (l + 1, 0, cur(s))),
                  pl.BlockSpec((None, 1, cw_mod), lambda s: (l + 1, 0, cur(s)))]
        mod_out = [pl.BlockSpec((r, cw_mod), lambda s: (0, cur(s)))]
        mod_shape = [jax.ShapeDtypeStruct((r, n), F32)]
        mod_args = [a16, w_ada, b_ada3]
    return pl.pallas_call(
        functools.partial(_mix_out_kernel, layer=l, with_mod=with_mod, nc=nc, nsteps=nsteps),
        grid=(nsteps + 1,),
        in_specs=([
            pl.BlockSpec(memory_space=pltpu.SMEM),
            blk(2560, 0), blk(1536, 2), blk(512, 9), blk(512, 5),
            pl.BlockSpec((tr, 128), lambda s: (cur(s) % nc, 0)),
            pl.BlockSpec((tr, 128), lambda s: (cur(s) % nc, 0)),
            _layer_spec(cw, l), _layer_spec(cb, l), _layer_spec(dtb, l), _layer_spec(alog, l),
            _layer_spec(dsk, l), _layer_spec(nw, l), _layer_spec(scw, l), _layer_spec(qw, l), _layer_spec(kw, l),
            prv_blk(D_MODEL)] + mods(mod_p, lambda s: prv(s) // nc)
            + [const(ms, D_SSD), const(ms, D_ATTN), const(ms, D_CONV), const(ms, D_MODEL)]
            + mods(mod_s, lambda s: 0)
            + [pl.BlockSpec((D_MODEL, D_MODEL), lambda s: (0, 0), pipeline_mode=pl.Buffered(1)),
               _layer_spec(n2_all, l)]
            + mod_in),
        out_specs=[
            per_batch(D_SSD, D_STATE), per_batch(8, D_XBC), per_batch(8, D_CONV),
            per_batch(CHUNK, 128), per_batch(CHUNK, 128),
            prv_blk(D_MODEL), prv_blk(D_MODEL), const(ms, D_MODEL), const(ms, D_MODEL),
        ] + mod_out,
        out_shape=[
            jax.ShapeDtypeStruct((nb, D_SSD, D_STATE), F32),
            jax.ShapeDtypeStruct((nb, 8, D_XBC), F32),
            jax.ShapeDtypeStruct((nb, 8, D_CONV), F32),
            jax.ShapeDtypeStruct((nb, CHUNK, 128), F32),
            jax.ShapeDtypeStruct((nb, CHUNK, 128), F32),
            jax.ShapeDtypeStruct((m, D_MODEL), F32), jax.ShapeDtypeStruct((m, D_MODEL), BF16),
            jax.ShapeDtypeStruct((ms, D_MODEL), F32), jax.ShapeDtypeStruct((ms, D_MODEL), BF16),
        ] + mod_shape,
        scratch_shapes=[
            pltpu.VMEM((2, tr, D_SSD), BF16), pltpu.VMEM((2, tr, D_CONV), BF16), pltpu.VMEM((2, tr, D_ATTN), BF16),
            pltpu.VMEM((8 + CHUNK, D_XBC), F32),
            pltpu.VMEM((8 + CHUNK, D_CONV), F32),
            pltpu.VMEM((D_STATE, D_SSD), F32),
            pltpu.VMEM((CHUNK, 128), F32),
            pltpu.VMEM((CHUNK, 128), F32),
        ],
        compiler_params=_cparams(("arbitrary",)),
        name="mix_out",
    )(sinks, proj, proj, proj, proj, cos_t, sin_t, cw, cb, dtb, alog, dsk, nw, scw, qw, kw,
      x_p, mod_p, mod_p, mod_p, *mix_s, x_s, mod_s, mod_s, mod_s, w_out_l, n2_all, *mod_args)


def _up_seq_kernel(h_ref, hs_ref, wg_ref, wu_ref, cw_ref, cb_ref, prev_ref, wd_ref,
                   a_ref, gst_ref, as_ref, gts_ref, wd16_ref, pad_ref, wg16_ref, wu16_ref, *, tm, rpb):
    i = pl.program_id(1)
    cw = cw_ref[...]
    cb = cb_ref[...]
    wd16_ref[...] = wd_ref[...].astype(BF16)

    @pl.when(i == 0)
    def _():
        wg16_ref[...] = wg_ref[...].astype(BF16)
        wu16_ref[...] = wu_ref[...].astype(BF16)
        hs = hs_ref[...]
        gt = _dot(hs, wg16_ref[...])
        u = _dot(hs, wu16_ref[...])
        conv = cw[2:3] * gt + cw[1:2] * prev_ref[1] + cw[0:1] * prev_ref[0]
        gts_ref[...] = gt
        as_ref[...] = (_silu(conv + cb) * u).astype(BF16)

    @pl.when((i * tm) % rpb == 0)
    def _():
        pad_ref[0:8, :] = jnp.zeros((8, pad_ref.shape[1]), F32)

    sub = 512
    for r in range(tm // sub):
        h = h_ref[r * sub:(r + 1) * sub, :]
        gt = _dot(h, wg16_ref[...])
        u = _dot(h, wu16_ref[...])
        base = 8 + r * sub
        pad_ref[base:base + sub, :] = gt
        conv = (cw[2:3] * gt + cw[1:2] * pad_ref[base - 1:base - 1 + sub, :]
                + cw[0:1] * pad_ref[base - 2:base - 2 + sub, :])
        a_ref[r * sub:(r + 1) * sub, :] = (_silu(conv + cb) * u).astype(BF16)
    tail = pad_ref[tm:tm + 8, :]
    pad_ref[0:8, :] = tail
    gst_ref[...] = tail


def _up_proj(h, hs, w_up_all, l, cw_all, cb_all, prev_all, w_down_all, nb, tm, tn):
    m = h.shape[0]
    ms = hs.shape[0]
    rpb = m // nb
    nt = D_FF // tn
    ni = m // tm
    slab = D_FF // (nt * ni)
    assert slab * nt * ni == D_FF and slab % 16 == 0
    return pl.pallas_call(
        functools.partial(_up_seq_kernel, tm=tm, rpb=rpb),
        grid=(nt, ni),
        in_specs=[
            pl.BlockSpec((tm, D_MODEL), lambda j, i: (i, 0)),
            pl.BlockSpec((ms, D_MODEL), lambda j, i: (0, 0)),
            pl.BlockSpec((None, D_MODEL, tn), lambda j, i: (l, 0, j)),
            pl.BlockSpec((None, D_MODEL, tn), lambda j, i: (l, 0, j + nt)),
            pl.BlockSpec((None, 3, tn), lambda j, i: (l, 0, j)),
            pl.BlockSpec((None, 1, tn), lambda j, i: (l, 0, j)),
            pl.BlockSpec((None, 2, ms, tn), lambda j, i: (l, 0, 0, j)),
            pl.BlockSpec((None, slab, D_MODEL), lambda j, i: (l, j * ni + i, 0)),
        ],
        out_specs=[
            pl.BlockSpec((tm, tn), lambda j, i: (i, j)),
            pl.BlockSpec((None, 8, tn), lambda j, i: ((i * tm) // rpb, 0, j)),
            pl.BlockSpec((ms, tn), lambda j, i: (0, j)),
            pl.BlockSpec((ms, tn), lambda j, i: (0, j)),
            pl.BlockSpec((slab, D_MODEL), lambda j, i: (j * ni + i, 0)),
        ],
        out_shape=[
            jax.ShapeDtypeStruct((m, D_FF), BF16),
            jax.ShapeDtypeStruct((nb, 8, D_FF), F32),
            jax.ShapeDtypeStruct((ms, D_FF), BF16),
            jax.ShapeDtypeStruct((ms, D_FF), F32),
            jax.ShapeDtypeStruct((D_FF, D_MODEL), BF16),
        ],
        scratch_shapes=[
            pltpu.VMEM((8 + tm, tn), F32),
            pltpu.VMEM((D_MODEL, tn), BF16),
            pltpu.VMEM((D_MODEL, tn), BF16),
        ],
        compiler_params=_cparams(("arbitrary", "arbitrary")),
        name="up_proj",
    )(h, hs, w_up_all, w_up_all, cw_all, cb_all, prev_all, w_down_all)


def _down_rows(a_ref, x_ref, gate_ref, mods, w_ref, nw_ref, x2_ref, hn_ref):
    x2 = x_ref[...] + gate_ref[...] * _dot(a_ref[...], w_ref[...])
    x2_ref[...] = x2
    if hn_ref is not None:
        hn_ref[...] = _rms_mod(x2, nw_ref[...], mods[0][...], mods[1][...]).astype(BF16)


def _down_kernel(*refs, with_norm):
    refs = list(refs)
    n_in = 5 if with_norm else 3
    p_in, s_in = refs[0:n_in], refs[n_in:2 * n_in]
    w_ref = refs[2 * n_in]
    nw_ref = refs[2 * n_in + 1] if with_norm else None
    outs = refs[2 * n_in + (2 if with_norm else 1):]
    p_out = (outs[0], outs[1]) if with_norm else (outs[0], None)
    s_out = (outs[2], outs[3]) if with_norm else (outs[1], None)

    @pl.when(pl.program_id(0) == 0)
    def _():
        _down_rows(s_in[0], s_in[1], s_in[2], s_in[3:], w_ref, nw_ref, *s_out)

    _down_rows(p_in[0], p_in[1], p_in[2], p_in[3:], w_ref, nw_ref, *p_out)


def _down(a_p, x_p, mod_p, a_s, x_s, mod_s, w_all, l, nw1_all, tm, rpb):
    m, ms = x_p.shape[0], x_s.shape[0]
    with_norm = l + 1 < nw1_all.shape[0]

    def mods(mod, t, r):
        sp = [_mod_spec(mod[0], 5, t, r)]
        if with_norm:
            sp += [_mod_spec(mod[1], 1, t, r), _mod_spec(mod[1], 0, t, r)]
        return sp

    def mod_args(mod):
        return [mod[0], mod[1], mod[1]] if with_norm else [mod[0]]

    in_specs = (_row_specs((D_FF, D_MODEL), tm) + mods(mod_p, tm, rpb)
                + _const_specs((D_FF, D_MODEL), ms) + mods(mod_s, 0, 1)
                + [pl.BlockSpec((D_FF, D_MODEL), lambda i: (0, 0), pipeline_mode=pl.Buffered(1))])
    args = [a_p, x_p] + mod_args(mod_p) + [a_s, x_s] + mod_args(mod_s) + [w_all]
    out_w = (D_MODEL, D_MODEL) if with_norm else (D_MODEL,)
    out_dt = (F32, BF16) if with_norm else (F32,)
    if with_norm:
        in_specs.append(_layer_spec(nw1_all, l + 1))
        args.append(nw1_all)
    res = pl.pallas_call(
        functools.partial(_down_kernel, with_norm=with_norm),
        grid=(m // tm,),
        in_specs=in_specs,
        out_specs=_row_specs(out_w, tm) + _const_specs(out_w, ms),
        out_shape=([jax.ShapeDtypeStruct((m, D_MODEL), d) for d in out_dt]
                   + [jax.ShapeDtypeStruct((ms, D_MODEL), d) for d in out_dt]),
        compiler_params=_cparams(("arbitrary",)),
        name="down_proj",
    )(*args)
    if with_norm:
        return res[0], res[1], res[2], res[3]
    return res[0], None, res[1], None


def _expand_matrix():
    r = lax.broadcasted_iota(jnp.int32, (128, D_SSD), 0)
    c = lax.broadcasted_iota(jnp.int32, (128, D_SSD), 1)
    return jnp.where(r == (c >> 6), 1.0, 0.0).astype(BF16)


def _ssd_step_kernel(p_ref, st_ref, cprev_ref, sprev_ref, cw_ref, cb_ref, dtb_ref, alog_ref, dsk_ref, nw_ref,
                     scw_ref, acc_ref, y_ref, yc_ref, u_ref, stn_ref):
    del acc_ref
    nbb = p_ref.shape[0]
    z = p_ref[:, C_Z:C_Z + D_SSD]
    raw = p_ref[:, C_XBC:C_XBC + D_XBC]
    cw = cw_ref[...]
    conv = cw[3:4] * raw + cw[2:3] * cprev_ref[2] + cw[1:2] * cprev_ref[1] + cw[0:1] * cprev_ref[0]
    xbc = _silu(conv + cb_ref[...])
    xs = xbc[:, :D_SSD]
    bm = xbc[:, D_SSD:D_SSD + 256]
    cm = xbc[:, D_SSD + 256:]

    lane = lax.broadcasted_iota(jnp.int32, (1, 128), 1)
    a_row = jnp.where(lane < SSM_HEADS, -jnp.exp(alog_ref[...]), 0.0)
    dtr = p_ref[:, C_DT:C_DT + 128] + dtb_ref[...]
    dt = jnp.maximum(dtr, 0.0) + jnp.log1p(jnp.exp(-jnp.abs(dtr)))
    dec = jnp.exp(dt * a_row)
    em = _expand_matrix()

    def expand(v):
        v1, v2, v3 = _split3(v)
        return _dot(v1, em) + _dot(v2, em) + _dot(v3, em)

    xdt_t = (xs * expand(dt)).T
    dec_t = expand(dec).T
    rowid = lax.broadcasted_iota(jnp.int32, (nbb, 512), 0)
    ysum = [jnp.zeros((nbb, 512), F32), jnp.zeros((nbb, 512), F32)]
    for i in range(nbb):
        for g in range(2):
            rs = slice(512 * g, 512 * (g + 1))
            hn = st_ref[i, rs, :] * dec_t[rs, i:i + 1] + xdt_t[rs, i:i + 1] * bm[i:i + 1, 128 * g:128 * (g + 1)]
            stn_ref[i, rs, :] = hn
            r = _dot_nt(cm[:, 128 * g:128 * (g + 1)].astype(BF16), hn.astype(BF16))
            ysum[g] = ysum[g] + jnp.where(rowid == i, r, 0.0)
    y = jnp.concatenate(ysum, axis=1) + dsk_ref[...] * xs
    y = y * _silu(z)
    nw = nw_ref[...]
    outs = []
    for g in range(2):
        yg = y[:, 512 * g:512 * (g + 1)]
        ms = jnp.mean(yg * yg, axis=-1, keepdims=True)
        outs.append(yg * lax.rsqrt(ms + EPS) * nw[:, 512 * g:512 * (g + 1)])
    y_ref[...] = jnp.concatenate(outs, axis=1).astype(BF16)

    u = p_ref[:, C_G + D_CONV:C_G + 2 * D_CONV] * p_ref[:, C_G + 2 * D_CONV:C_G + 3 * D_CONV]
    sw = scw_ref[...]
    sconv = sw[2:3] * u + sw[1:2] * sprev_ref[1] + sw[0:1] * sprev_ref[0]
    u_ref[...] = u
    yc_ref[...] = (p_ref[:, C_G:C_G + D_CONV] * sconv).astype(BF16)


def _ssd_step(proj, state_all, cprev_all, sprev_all, l, cw, cb, dtb, alog, dsk, nw, scw, acc, nbb=8):
    nb = proj.shape[0]
    return pl.pallas_call(
        _ssd_step_kernel,
        grid=(nb // nbb,),
        input_output_aliases={11: 3},
        in_specs=[
            pl.BlockSpec((nbb, PROJ_P), lambda i: (i, 0)),
            pl.BlockSpec((None, nbb, D_SSD, D_STATE), lambda i: (l, i, 0, 0)),
            pl.BlockSpec((None, 3, nbb, D_XBC), lambda i: (l, 0, i, 0)),
            pl.BlockSpec((None, 2, nbb, D_CONV), lambda i: (l, 0, i, 0)),
            _layer_spec(cw, l), _layer_spec(cb, l), _layer_spec(dtb, l), _layer_spec(alog, l),
            _layer_spec(dsk, l), _layer_spec(nw, l), _layer_spec(scw, l),
            pl.BlockSpec(memory_space=pl.ANY),
        ],
        out_specs=[
            pl.BlockSpec((nbb, D_SSD), lambda i: (i, 0)),
            pl.BlockSpec((nbb, D_CONV), lambda i: (i, 0)),
            pl.BlockSpec((nbb, D_CONV), lambda i: (i, 0)),
            pl.BlockSpec((None, nbb, D_SSD, D_STATE), lambda i: (l, i, 0, 0)),
        ],
        out_shape=[
            jax.ShapeDtypeStruct((nb, D_SSD), BF16),
            jax.ShapeDtypeStruct((nb, D_CONV), BF16),
            jax.ShapeDtypeStruct((nb, D_CONV), F32),
            jax.ShapeDtypeStruct(acc.shape, F32),
        ],
        compiler_params=_cparams(("arbitrary",)),
        name="ssd_step",
    )(proj, state_all, cprev_all, sprev_all, cw, cb, dtb, alog, dsk, nw, scw, acc)


def _qk_step_kernel(p_ref, cos_ref, sin_ref, qw_ref, kw_ref, q_ref, k_ref):
    bd = _seg_mean_matrix()
    lane = lax.broadcasted_iota(jnp.int32, (p_ref.shape[0], 128), 1)
    first_half = (lane & 32) == 0
    cos = cos_ref[...]
    sin = sin_ref[...]
    for t in range(4):
        qn = _head_norm(p_ref[:, C_Q + 128 * t:C_Q + 128 * (t + 1)], qw_ref[...], bd)
        q_ref[:, 128 * t:128 * (t + 1)] = _rope(qn, cos, sin, first_half) * (HEAD_DIM ** -0.5)
    k_ref[...] = _rope(_head_norm(p_ref[:, C_K:C_K + 128], kw_ref[...], bd), cos, sin, first_half)


def _qk_step(proj, cos_r, sin_r, l, qw, kw):
    nb = proj.shape[0]
    full = lambda a: pl.BlockSpec(a.shape, lambda i: (0,) * a.ndim)
    return pl.pallas_call(
        _qk_step_kernel,
        grid=(1,),
        in_specs=[full(proj), full(cos_r), full(sin_r), _layer_spec(qw, l), _layer_spec(kw, l)],
        out_specs=[pl.BlockSpec((nb, D_ATTN), lambda i: (0, 0)), pl.BlockSpec((nb, 128), lambda i: (0, 0))],
        out_shape=[jax.ShapeDtypeStruct((nb, D_ATTN), F32), jax.ShapeDtypeStruct((nb, 128), F32)],
        name="qk_step",
    )(proj, cos_r, sin_r, qw, kw)


def _attn_step_kernel(q_ref, kc_ref, vc_ref, knew_ref, vnew_ref, sink_ref, o_ref):
    q = q_ref[...]
    s = lax.dot_general(q.astype(BF16), kc_ref[...].astype(BF16), (((2,), (1,)), ((0,), (0,))),
                        preferred_element_type=F32)
    s_new = jnp.sum(q * knew_ref[...], axis=-1, keepdims=True)
    sink = sink_ref[...]
    mx = jnp.maximum(jnp.maximum(jnp.max(s, axis=-1, keepdims=True), s_new), sink)
    p = jnp.exp(s - mx)
    p_new = jnp.exp(s_new - mx)
    den = jnp.sum(p, axis=-1, keepdims=True) + p_new + jnp.exp(sink - mx)
    o = lax.dot_general((p / den).astype(BF16), vc_ref[...].astype(BF16), (((2,), (2,)), ((0,), (0,))),
                        preferred_element_type=F32)
    o_ref[...] = o + (p_new / den) * vnew_ref[...]


def _attn_step(q3, kc_all, vc_all, l, knew, vnew, sink_all):
    nb = q3.shape[0]
    full = lambda a: pl.BlockSpec(a.shape, lambda i: (0,) * a.ndim)
    return pl.pallas_call(
        _attn_step_kernel,
        grid=(1,),
        in_specs=[full(q3), _layer_spec(kc_all, l), _layer_spec(vc_all, l), full(knew), full(vnew),
                  _layer_spec(sink_all, l)],
        out_specs=pl.BlockSpec((nb, ATT_HEADS, 128), lambda i: (0, 0, 0)),
        out_shape=jax.ShapeDtypeStruct((nb, ATT_HEADS, 128), F32),
        name="attn_step",
    )(q3, kc_all, vc_all, knew, vnew, sink_all)


def _rope_tables(pos):
    half = HEAD_DIM // 2
    inv = np.power(ROPE_THETA, -np.arange(half, dtype=np.float64) / half)
    ang = np.asarray(pos, np.float64)[:, None] * inv[None, :]
    cos = np.cos(ang)
    sin = np.sin(ang)
    cos_t = np.concatenate([cos, cos, cos, cos], axis=-1).astype(np.float32)
    sin_t = np.concatenate([-sin, sin, -sin, sin], axis=-1).astype(np.float32)
    return jnp.asarray(cos_t), jnp.asarray(sin_t)


def kernel(x_prompt, x_sample, state_ssm, state_ssm_conv, cache_win_k, cache_win_v, state_short_conv,
           state_ffn_conv, c_prompt, c_sample, norm1_w, norm2_w, w_ada, b_ada, w_in, ssm_conv_w, ssm_conv_b,
           dt_bias, a_log, d_skip, ssm_norm_w, q_norm_w, k_norm_w, sinks, sconv_w, w_out, w_up, ffn_conv_w,
           ffn_conv_b, w_down):
    nbp, seq, _ = x_prompt.shape
    nbs = x_sample.shape[0]
    depth = w_in.shape[0]
    mp = nbp * seq
    nbuf = cache_win_k.shape[2]

    in_tab = _in_proj_table()
    w_in_t = jnp.swapaxes(w_in, 1, 2)

    n1 = norm1_w.reshape(depth, 1, D_MODEL)
    n2 = norm2_w.reshape(depth, 1, D_MODEL)
    cb = ssm_conv_b.reshape(depth, 1, D_XBC)
    dtb = jnp.pad(dt_bias, ((0, 0), (0, 128 - SSM_HEADS))).reshape(depth, 1, 128)
    alog = jnp.pad(a_log, ((0, 0), (0, 128 - SSM_HEADS))).reshape(depth, 1, 128)
    dsk = jnp.repeat(d_skip, SSM_HEAD_DIM, axis=1).reshape(depth, 1, D_SSD)
    nw = ssm_norm_w.reshape(depth, 1, D_SSD)
    qw = jnp.tile(q_norm_w, (1, 2)).reshape(depth, 1, 128)
    kw = jnp.tile(k_norm_w, (1, 2)).reshape(depth, 1, 128)
    fcb = ffn_conv_b.reshape(depth, 1, D_FF)
    sink3 = sinks.reshape(depth, ATT_HEADS, 1)

    st_all = state_ssm.reshape(depth, nbs, D_SSD, D_STATE)
    cprev = jnp.transpose(state_ssm_conv, (0, 2, 1, 3))
    sprev = jnp.transpose(state_short_conv, (0, 2, 1, 3))
    fprev = jnp.transpose(state_ffn_conv, (0, 2, 1, 3))
    ck = cache_win_k.reshape(depth, nbs, nbuf, 128)
    cv = cache_win_v.reshape(depth, nbs, nbuf, 128)
    ck_t = jnp.swapaxes(ck, 2, 3)
    cv_t = jnp.swapaxes(cv, 2, 3)

    c_all = jnp.concatenate([c_prompt, jnp.zeros((8 - nbp, D_MODEL), F32), c_sample], axis=0)
    b_ada3 = b_ada.reshape(depth, 1, 6 * D_MODEL)
    mod0, a16 = _modulation_first(c_all, w_ada, b_ada3)

    def split_mod(mod):
        return mod[:nbp].reshape(nbp, 1, 6 * D_MODEL), mod[8:].reshape(1, nbs, 6 * D_MODEL)

    mod_p, mod_s = [None] * (depth + 1), [None] * (depth + 1)
    mod_p[0], mod_s[0] = split_mod(mod0)

    cos_p, sin_p = _rope_tables(np.arange(seq))
    cos_s, sin_s = _rope_tables(np.full((1,), PAST_LEN))

    head_g = jnp.arange(ATT_HEADS) // (ATT_HEADS // KV_HEADS)
    q_place = (head_g[:, None] == jnp.arange(KV_HEADS)[None, :]).astype(F32)[None, :, :, None]

    xp = x_prompt.reshape(mp, D_MODEL)
    xs = x_sample.reshape(nbs, D_MODEL)
    TM = 512
    hp = _norm_mod(xp, n1, mod_p[0], 0, 1, 0, TM, seq)
    hs = _norm_mod(xs, n1, mod_s[0], 0, 1, 0, nbs, nbs)

    outs_p = [[] for _ in range(6)]
    outs_s = [[] for _ in range(6)]
    st_acc = jnp.zeros(st_all.shape, F32)
    for l in range(depth):
        proj, projs, w_out_l = _in_proj(hp, hs, w_in_t, l, in_tab, w_out, min(1024, seq))

        y_s, yc_s, u_s, st_acc = _ssd_step(projs, st_all, cprev, sprev, l, ssm_conv_w, cb, dtb, alog, dsk, nw,
                                           sconv_w, st_acc)
        q_r, k_r = _qk_step(projs, cos_s, sin_s, l, qw, kw)
        v_r = projs[:, C_V:C_V + 128]
        q3 = (q_r.reshape(nbs, ATT_HEADS, 1, HEAD_DIM) * q_place).reshape(nbs, ATT_HEADS, 128)
        o3 = _attn_step(q3, ck_t, cv_t, l, k_r[:, None, :], v_r[:, None, :], sink3)
        o4 = o3.reshape(nbs, ATT_HEADS, KV_HEADS, HEAD_DIM)
        o_s = jnp.concatenate([o4[:, :4, 0], o4[:, 4:, 1]], axis=1).reshape(nbs, D_ATTN).astype(BF16)

        mixed = _mix_out(proj, xp, mod_p[l], (y_s, o_s, yc_s), xs, mod_s[l], w_out_l, n2, nbp, seq, l,
                         sinks, cos_p, sin_p, ssm_conv_w, cb, dtb, alog, dsk, nw, sconv_w, qw, kw,
                         mod_next=(a16, w_ada, b_ada3) if l + 1 < depth else None)
        st_t, cst, ust, kwin, vwin, xp, h2, xs, h2s = mixed[:9]
        if l + 1 < depth:
            mod_p[l + 1], mod_s[l + 1] = split_mod(mixed[9])
        act, gst, act_s, gt_s, w_down_l = _up_proj(h2, h2s, w_up, l, ffn_conv_w, fcb, fprev, w_down, nbp,
                                                   min(2048, seq), 512)
        xp, hp, xs, hs = _down(act, xp, mod_p[l:l + 2], act_s, xs, mod_s[l:l + 2], w_down_l, l, n1, 256, seq)

        outs_p[0].append(st_t.reshape(nbp, SSM_HEADS, SSM_HEAD_DIM, D_STATE))
        outs_p[1].append(cst[:, 8 - (SSM_CONV - 1):])
        outs_p[2].append(kwin.reshape(nbp, WINDOW, KV_HEADS, HEAD_DIM))
        outs_p[3].append(vwin.reshape(nbp, WINDOW, KV_HEADS, HEAD_DIM))
        outs_p[4].append(ust[:, 6:])
        outs_p[5].append(gst[:, 6:])

        outs_s[1].append(projs[:, C_XBC:C_XBC + D_XBC])
        outs_s[2].append(k_r)
        outs_s[3].append(v_r)
        outs_s[4].append(u_s)
        outs_s[5].append(gt_s)

    res_p = [jnp.stack(a, axis=0) for a in outs_p]
    new_rows = [None] + [jnp.stack(a, axis=0) for a in outs_s[1:]]
    s_ssm = st_acc.reshape(state_ssm.shape)
    s_ssm_conv = jnp.concatenate([state_ssm_conv[:, :, 1:], new_rows[1][:, :, None]], axis=2)
    s_win_k = jnp.concatenate([ck[:, :, 1:], new_rows[2][:, :, None]], axis=2).reshape(cache_win_k.shape)
    s_win_v = jnp.concatenate([cv[:, :, 1:], new_rows[3][:, :, None]], axis=2).reshape(cache_win_v.shape)
    s_short = jnp.concatenate([state_short_conv[:, :, 1:], new_rows[4][:, :, None]], axis=2)
    s_ffn = jnp.concatenate([state_ffn_conv[:, :, 1:], new_rows[5][:, :, None]], axis=2)
    return (xp.reshape(nbp, seq, D_MODEL), xs.reshape(nbs, 1, D_MODEL), *res_p,
            s_ssm, s_ssm_conv, s_win_k, s_win_v, s_short, s_ffn)
```
